```python
import math
import jax, jax.numpy as jnp
from jax import lax
import numpy as np

D_MODEL = 1024
BATCH = 4
SEQ = 4096
DEPTH = 1

HEAD_DIM = 64
A_HEADS = 4
A_VDIM = 2 * HEAD_DIM
A_WIDTH = A_HEADS * A_VDIM
B_HEADS = 8
B_WIDTH = B_HEADS * HEAD_DIM
B_BLOCK = 256
B_TOPK = 3
Q_BLOCK = 128
MOBA_Q_BLOCK = 32
D_FF = 2816
CONV_W = 3
ROPE_THETA = 10000.0
EPS = 1e-6
GATE_WIDTH = 2 * D_MODEL
IN_COLS = 3 * A_WIDTH + 3 * B_WIDTH + GATE_WIDTH
SPLITS = (A_WIDTH, 2 * A_WIDTH, 3 * A_WIDTH,
          3 * A_WIDTH + B_WIDTH, 3 * A_WIDTH + 2 * B_WIDTH, 3 * A_WIDTH + 3 * B_WIDTH)

kernel_name = "hybrid_diffattn_moba_convffn"


def rms_norm(x, g):
    xf = x.astype(jnp.float32)
    y = xf * lax.rsqrt(jnp.mean(xf * xf, axis=-1, keepdims=True) + EPS)
    return (y * g.astype(jnp.float32)).astype(x.dtype)


def rope_tables(seq):
    inv = 1.0 / (ROPE_THETA ** (jnp.arange(0, HEAD_DIM, 2, dtype=jnp.float32) / HEAD_DIM))
    ang = jnp.arange(seq, dtype=jnp.float32)[:, None] * inv[None, :]
    return jnp.cos(ang), jnp.sin(ang)


def apply_rope(x, cos, sin):
    x1, x2 = jnp.split(x, 2, axis=-1)
    c = cos.astype(x.dtype)
    s = sin.astype(x.dtype)
    return jnp.concatenate([x1 * c - x2 * s, x2 * c + x1 * s], axis=-1)


def diff_attention(q, k, v, lam):
    Bsz, H, _, S, _ = q.shape
    VD = v.shape[-1]
    scale = HEAD_DIM ** -0.5
    kpos = jnp.arange(S)

    def one_block(i):
        start = i * Q_BLOCK
        qb = lax.dynamic_slice_in_dim(q, start, Q_BLOCK, axis=3)
        s = jnp.einsum('bhcqd,bhckd->bhcqk', qb, k).astype(jnp.float32) * scale
        qpos = start + jnp.arange(Q_BLOCK)
        s = jnp.where(kpos[None, :] <= qpos[:, None], s, -jnp.inf)
        p = jax.nn.softmax(s, axis=-1)
        a = p[:, :, 0] - lam * p[:, :, 1]
        return jnp.einsum('bhqk,bhkv->bhqv', a.astype(v.dtype), v)

    out = lax.map(one_block, jnp.arange(S // Q_BLOCK))
    return jnp.moveaxis(out, 0, 2).reshape(Bsz, H, S, VD)


def moba_attention(q, k, v):
    Bsz, H, S, D = q.shape
    nb = -(-S // B_BLOCK)
    pad = nb * B_BLOCK - S
    kp = jnp.pad(k, ((0, 0), (0, 0), (0, pad), (0, 0)))
    vp = jnp.pad(v, ((0, 0), (0, 0), (0, pad), (0, 0)))
    kblk = kp.reshape(Bsz, H, nb, B_BLOCK, D)
    vblk = vp.reshape(Bsz, H, nb, B_BLOCK, D)
    kmean = jnp.mean(kblk.astype(jnp.float32), axis=3)
    n_sel = min(B_TOPK, max(nb - 1, 1))
    blk_ids = jnp.arange(nb)
    b_idx = jnp.arange(Bsz)[:, None, None, None]
    h_idx = jnp.arange(H)[None, :, None, None]
    scale = D ** -0.5

    def one_block(i):
        start = i * MOBA_Q_BLOCK
        qb = lax.dynamic_slice_in_dim(q, start, MOBA_Q_BLOCK, axis=2)
        qpos = start + jnp.arange(MOBA_Q_BLOCK)
        own = start // B_BLOCK
        g = jnp.einsum('bhqd,bhnd->bhqn', qb.astype(jnp.float32), kmean)
        g = jnp.where(blk_ids[None, :] < own, g, -jnp.inf)
        _, sel = lax.top_k(g, n_sel)
        valid = jnp.arange(n_sel) < own
        kg = kblk[b_idx, h_idx, sel]
        vg = vblk[b_idx, h_idx, sel]
        s_sel = jnp.einsum('bhqd,bhqnkd->bhqnk', qb, kg).astype(jnp.float32) * scale
        s_sel = jnp.where(valid[None, None, None, :, None], s_sel, -jnp.inf)
        s_sel = s_sel.reshape(Bsz, H, MOBA_Q_BLOCK, n_sel * B_BLOCK)
        k_own = lax.dynamic_slice_in_dim(kp, own * B_BLOCK, B_BLOCK, axis=2)
        v_own = lax.dynamic_slice_in_dim(vp, own * B_BLOCK, B_BLOCK, axis=2)
        s_own = jnp.einsum('bhqd,bhkd->bhqk', qb, k_own).astype(jnp.float32) * scale
        kpos_own = own * B_BLOCK + jnp.arange(B_BLOCK)
        s_own = jnp.where(kpos_own[None, :] <= qpos[:, None], s_own, -jnp.inf)
        p = jax.nn.softmax(jnp.concatenate([s_sel, s_own], axis=-1), axis=-1).astype(v.dtype)
        p_sel = p[..., :n_sel * B_BLOCK].reshape(Bsz, H, MOBA_Q_BLOCK, n_sel, B_BLOCK)
        p_own = p[..., n_sel * B_BLOCK:]
        return (jnp.einsum('bhqnk,bhqnkd->bhqd', p_sel, vg)
                + jnp.einsum('bhqk,bhkd->bhqd', p_own, v_own))

    out = lax.map(one_block, jnp.arange(S // MOBA_Q_BLOCK))
    return jnp.moveaxis(out, 0, 2).reshape(Bsz, H, S, D)


def causal_dwconv(u, w, b):
    S = u.shape[1]
    up = jnp.pad(u, ((0, 0), (CONV_W - 1, 0), (0, 0)))
    y = b
    for j in range(CONV_W):
        y = y + up[:, j:j + S] * w[j]
    return y


def setup_inputs(seed: int = 0) -> dict:
    key = jax.random.key(seed)
    ks = jax.random.split(key, 24)
    n = jax.random.normal
    f32 = jnp.float32
    L, D = DEPTH, D_MODEL
    return {
        "x": n(ks[0], (BATCH, SEQ, D), f32),
        "norm1_g": 1.0 + 0.02 * n(ks[1], (L, D), f32),
        "w_in": n(ks[2], (L, D, IN_COLS), f32) * D ** -0.5,
        "b_gate": 0.02 * n(ks[3], (L, GATE_WIDTH), f32),
        "qn_a": 1.0 + 0.02 * n(ks[4], (L, HEAD_DIM), f32),
        "kn_a": 1.0 + 0.02 * n(ks[5], (L, HEAD_DIM), f32),
        "lam_q1": 0.1 * n(ks[6], (L, HEAD_DIM), f32),
        "lam_k1": 0.1 * n(ks[7], (L, HEAD_DIM), f32),
        "lam_q2": 0.1 * n(ks[8], (L, HEAD_DIM), f32),
        "lam_k2": 0.1 * n(ks[9], (L, HEAD_DIM), f32),
        "subln_g": 1.0 + 0.02 * n(ks[10], (L, A_VDIM), f32),
        "qn_b": 1.0 + 0.02 * n(ks[11], (L, HEAD_DIM), f32),
        "kn_b": 1.0 + 0.02 * n(ks[12], (L, HEAD_DIM), f32),
        "w_a_proj": n(ks[13], (L, A_WIDTH, D), f32) * A_WIDTH ** -0.5,
        "w_b_proj": n(ks[14], (L, B_WIDTH, D), f32) * B_WIDTH ** -0.5,
        "w_out": n(ks[15], (L, D, D), f32) * D ** -0.5,
        "norm2_g": 1.0 + 0.02 * n(ks[16], (L, D), f32),
        "w_up": n(ks[17], (L, D, 2 * D_FF), f32) * D ** -0.5,
        "conv_w": n(ks[18], (L, CONV_W, D_FF), f32) * CONV_W ** -0.5,
        "conv_b": 0.02 * n(ks[19], (L, D_FF), f32),
        "w_down": n(ks[20], (L, D_FF, D), f32) * D_FF ** -0.5,
    }


def reference(x, norm1_g, w_in, b_gate, qn_a, kn_a, lam_q1, lam_k1, lam_q2, lam_k2, subln_g,
              qn_b, kn_b, w_a_proj, w_b_proj, w_out, norm2_g, w_up, conv_w, conv_b, w_down):
    Bsz, S, _ = x.shape
    cos, sin = rope_tables(S)
    for l in range(DEPTH):
        lam_init = 0.8 - 0.6 * math.exp(-0.3 * l)
        h = rms_norm(x, norm1_g[l])
        proj = h @ w_in[l]
        qa, ka, va, qm, km, vm, gates = jnp.split(proj, SPLITS, axis=-1)
        ga, gb = jnp.split(gates + b_gate[l], 2, axis=-1)

        qa = qa.reshape(Bsz, S, A_HEADS, 2, HEAD_DIM).transpose(0, 2, 3, 1, 4)
        ka = ka.reshape(Bsz, S, A_HEADS, 2, HEAD_DIM).transpose(0, 2, 3, 1, 4)
        qa = apply_rope(rms_norm(qa, qn_a[l]), cos, sin)
        ka = apply_rope(rms_norm(ka, kn_a[l]), cos, sin)
        va = va.reshape(Bsz, S, A_HEADS, A_VDIM).transpose(0, 2, 1, 3)
        lam = (jnp.exp(jnp.sum(lam_q1[l].astype(jnp.float32) * lam_k1[l].astype(jnp.float32)))
               - jnp.exp(jnp.sum(lam_q2[l].astype(jnp.float32) * lam_k2[l].astype(jnp.float32)))
               + lam_init)
        oa = diff_attention(qa, ka, va, lam)
        oa = rms_norm(oa, subln_g[l]) * (1.0 - lam_init)
        oa = oa.transpose(0, 2, 1, 3).reshape(Bsz, S, A_WIDTH)

        qm = qm.reshape(Bsz, S, B_HEADS, HEAD_DIM).transpose(0, 2, 1, 3)
        km = km.reshape(Bsz, S, B_HEADS, HEAD_DIM).transpose(0, 2, 1, 3)
        vm = vm.reshape(Bsz, S, B_HEADS, HEAD_DIM).transpose(0, 2, 1, 3)
        qm = apply_rope(rms_norm(qm, qn_b[l]), cos, sin)
        km = apply_rope(rms_norm(km, kn_b[l]), cos, sin)
        ob = moba_attention(qm, km, vm).transpose(0, 2, 1, 3).reshape(Bsz, S, B_WIDTH)

        merged = jax.nn.sigmoid(ga) * (oa @ w_a_proj[l]) + jax.nn.sigmoid(gb) * (ob @ w_b_proj[l])
        x = x + merged @ w_out[l]

        h2 = rms_norm(x, norm2_g[l])
        u, g = jnp.split(h2 @ w_up[l], 2, axis=-1)
        u = causal_dwconv(u, conv_w[l], conv_b[l])
        x = x + (jax.nn.gelu(u) * g) @ w_down[l]
    return x
```

```python
import functools
import math

import jax
import jax.numpy as jnp
from jax import lax
from jax.experimental import pallas as pl
from jax.experimental.pallas import tpu as pltpu

HEAD_DIM = 64
A_HEADS = 4
A_VDIM = 2 * HEAD_DIM
A_WIDTH = A_HEADS * A_VDIM
B_HEADS = 8
B_WIDTH = B_HEADS * HEAD_DIM
B_BLOCK = 256
B_TOPK = 3
CONV_W = 3
ROPE_THETA = 10000.0
EPS = 1e-6

LANES = 128
BLK = B_BLOCK
PAIRS = A_WIDTH // LANES
LOG2E = 1.4426950408889634
NEG_BIG = -1e30
POS_BIG = 1e30
VMEM_LIMIT = 56 * 1024 * 1024

f32 = jnp.float32
bf16 = jnp.bfloat16


def _nt_dot(a, b, **kw):
    return lax.dot_general(a, b, (((1,), (1,)), ((), ())), preferred_element_type=f32, **kw)


def _inproj_kernel(x_ref, g1_ref, w_ref, bg_ref, gain_ref, cos_ref, sin_ref, bd_ref,
                   q_ref, k_ref, vt_ref, kmean_ref, sig_ref, h_scr, *, tm):
    nblk = tm // BLK
    x = x_ref[0]
    ms = jnp.mean(x * x, axis=-1, keepdims=True)
    h_scr[...] = (x * lax.rsqrt(ms + EPS) * g1_ref[...]).astype(bf16)
    cos = cos_ref[...]
    sin = sin_ref[...]
    bd = bd_ref[...]
    lane = lax.broadcasted_iota(jnp.int32, (tm, LANES), 1)
    first_half = (lane & (HEAD_DIM - 1)) < (HEAD_DIM // 2)

    def col_tile(j):
        return jnp.dot(h_scr[...], w_ref[:, j * A_WIDTH:(j + 1) * A_WIDTH],
                       preferred_element_type=f32)

    def norm_rope(y, gain):
        y2 = y * y
        hi = y2.astype(bf16)
        lo = (y2 - hi.astype(f32)).astype(bf16)
        ss = (jnp.dot(hi, bd, preferred_element_type=f32)
              + jnp.dot(lo, bd, preferred_element_type=f32))
        yn = y * lax.rsqrt(ss * (1.0 / HEAD_DIM) + EPS) * gain
        rot = jnp.where(first_half, pltpu.roll(yn, LANES - HEAD_DIM // 2, 1),
                        pltpu.roll(yn, HEAD_DIM // 2, 1))
        return yn * cos + rot * sin

    for branch in range(2):
        acc_q = col_tile(3 * branch)
        for c in range(PAIRS):
            out = norm_rope(acc_q[:, c * LANES:(c + 1) * LANES], gain_ref[2 * branch:2 * branch + 1, :])
            for bi in range(nblk):
                q_ref[0, branch * PAIRS + c, bi] = out[bi * BLK:(bi + 1) * BLK].astype(bf16)
        acc_k = col_tile(3 * branch + 1)
        for c in range(PAIRS):
            out = norm_rope(acc_k[:, c * LANES:(c + 1) * LANES],
                            gain_ref[2 * branch + 1:2 * branch + 2, :])
            for bi in range(nblk):
                blk = out[bi * BLK:(bi + 1) * BLK]
                k_ref[0, branch * PAIRS + c, bi] = blk.astype(bf16)
                if branch == 1:
                    kmean_ref[0, bi:bi + 1, c * LANES:(c + 1) * LANES] = jnp.mean(
                        blk, axis=0, keepdims=True)
        acc_v = col_tile(3 * branch + 2)
        for c in range(PAIRS):
            for bi in range(nblk):
                blk = acc_v[bi * BLK:(bi + 1) * BLK, c * LANES:(c + 1) * LANES]
                vt_ref[0, branch * PAIRS + c, bi] = blk.T.astype(bf16)

    for jj in range(4):
        acc = col_tile(6 + jj) + bg_ref[:, jj * A_WIDTH:(jj + 1) * A_WIDTH]
        sig_ref[0, :, jj * A_WIDTH:(jj + 1) * A_WIDTH] = jax.nn.sigmoid(acc).astype(bf16)


def _inproj(x, g1, w_in_bf, b_gate, gains, cos_t, sin_t, bd, *, tm):
    bsz, seq, d = x.shape
    n_cols = w_in_bf.shape[1]
    nblk = tm // BLK
    spt = seq // tm
    nb = seq // BLK
    qk_shape = jax.ShapeDtypeStruct((bsz, 2 * PAIRS, nb, BLK, LANES), bf16)
    vt_shape = jax.ShapeDtypeStruct((bsz, 2 * PAIRS, nb, LANES, BLK), bf16)
    qk_spec = pl.BlockSpec((1, 2 * PAIRS, nblk, BLK, LANES), lambda b, i: (b, 0, i, 0, 0))
    vt_spec = pl.BlockSpec((1, 2 * PAIRS, nblk, LANES, BLK), lambda b, i: (b, 0, i, 0, 0))
    return pl.pallas_call(
        functools.partial(_inproj_kernel, tm=tm),
        grid=(bsz, spt),
        in_specs=[
            pl.BlockSpec((1, tm, d), lambda b, i: (b, i, 0)),
            pl.BlockSpec((1, d), lambda b, i: (0, 0)),
            pl.BlockSpec((d, n_cols), lambda b, i: (0, 0)),
            pl.BlockSpec((1, b_gate.shape[1]), lambda b, i: (0, 0)),
            pl.BlockSpec((4, LANES), lambda b, i: (0, 0)),
            pl.BlockSpec((tm, LANES), lambda b, i: (i, 0)),
            pl.BlockSpec((tm, LANES), lambda b, i: (i, 0)),
            pl.BlockSpec((LANES, LANES), lambda b, i: (0, 0)),
        ],
        out_specs=[
            qk_spec, qk_spec, vt_spec,
            pl.BlockSpec((1, nblk, B_WIDTH), lambda b, i: (b * spt + i, 0, 0)),
            pl.BlockSpec((1, tm, b_gate.shape[1]), lambda b, i: (b, i, 0)),
        ],
        out_shape=[
            qk_shape, qk_shape, vt_shape,
            jax.ShapeDtypeStruct((bsz * spt, nblk, B_WIDTH), f32),
            jax.ShapeDtypeStruct((bsz, seq, b_gate.shape[1]), bf16),
        ],
        scratch_shapes=[pltpu.VMEM((tm, d), bf16)],
        compiler_params=pltpu.CompilerParams(
            dimension_semantics=("arbitrary", "arbitrary"), vmem_limit_bytes=VMEM_LIMIT),
        name="inproj",
    )(x, g1, w_in_bf, b_gate, gains, cos_t, sin_t, bd)


def _attn_kernel(*refs, moba, lam_init):
    if moba:
        q_ref, k_ref, vt_ref, kmean_ref, o_ref, m_scr, l_scr, acc_scr, sel_scr = refs
    else:
        q_ref, k_ref, vt_ref, lam_ref, subg_ref, o_ref, m_scr, l_scr, acc_scr = refs
    qi = pl.program_id(2)
    nb = k_ref.shape[2]

    q = q_ref[0, 0, 0].astype(f32)
    lane = lax.broadcasted_iota(jnp.int32, (BLK, LANES), 1)
    qs = jnp.concatenate([jnp.where(lane < HEAD_DIM, q, 0.0),
                          jnp.where(lane >= HEAD_DIM, q, 0.0)], axis=0).astype(bf16)

    m_scr[...] = jnp.full(m_scr.shape, NEG_BIG, f32)
    l_scr[...] = jnp.zeros(l_scr.shape, f32)
    acc_scr[...] = jnp.zeros(acc_scr.shape, f32)

    if moba:
        g = _nt_dot(kmean_ref[0], qs.astype(f32), precision=lax.Precision.HIGHEST)
        nidx = lax.broadcasted_iota(jnp.int32, (nb, 2 * BLK), 0)
        past = nidx < qi
        g = jnp.where(past, g, NEG_BIG)
        cnt = jnp.zeros((nb, 2 * BLK), f32)
        for mm in range(nb):
            row = g[mm:mm + 1, :]
            beats = (row > g) | ((row == g) & (nidx > mm))
            cnt = cnt + jnp.where(beats, 1.0, 0.0)
        sel_scr[...] = jnp.where((cnt < float(B_TOPK)) & past, 1.0, 0.0)

    def step(j, diag):
        s = _nt_dot(k_ref[0, 0, j], qs)
        if diag:
            kidx = lax.broadcasted_iota(jnp.int32, (BLK, 2 * BLK), 0)
            ridx = lax.broadcasted_iota(jnp.int32, (BLK, 2 * BLK), 1) & (BLK - 1)
            s = jnp.where(kidx <= ridx, s, NEG_BIG)
        m_old = m_scr[...]
        m_new = jnp.maximum(m_old, jnp.max(s, axis=0, keepdims=True))
        m_sub = m_new
        if moba and not diag:
            chosen = sel_scr[pl.ds(j, 1), :] > 0.5
            m_new = jnp.where(chosen, m_new, m_old)
            m_sub = jnp.where(chosen, m_new, POS_BIG)
        alpha = jnp.exp2(m_old - m_new)
        p = jnp.exp2(s - m_sub)
        l_scr[...] = alpha * l_scr[...] + jnp.sum(p, axis=0, keepdims=True)
        acc_scr[...] = alpha * acc_scr[...] + jnp.dot(vt_ref[0, 0, j], p.astype(bf16),
                                                      preferred_element_type=f32)
        m_scr[...] = m_new

    def body(j, carry):
        step(j, False)
        return carry

    lax.fori_loop(0, qi, body, 0)
    step(qi, True)

    o = acc_scr[...] * (1.0 / l_scr[...])
    if moba:
        ot = jnp.concatenate([o[:HEAD_DIM, :BLK], o[HEAD_DIM:, BLK:]], axis=0).T
        o_ref[0] = ot.astype(bf16)
    else:
        lv = lam_ref[...]
        lam = (jnp.exp(jnp.sum(lv[0:1] * lv[1:2], axis=-1, keepdims=True))
               - jnp.exp(jnp.sum(lv[2:3] * lv[3:4], axis=-1, keepdims=True)) + lam_init)
        ot = (o[:, :BLK] - lam * o[:, BLK:]).T
        ms = jnp.mean(ot * ot, axis=-1, keepdims=True)
        o_ref[0] = (ot * lax.rsqrt(ms + EPS) * subg_ref[...] * (1.0 - lam_init)).astype(bf16)


def _attention(q_all, k_all, vt_all, aux, *, moba, lam_init):
    bsz, _, nb, _, _ = q_all.shape
    off = PAIRS if moba else 0
    in_specs = [
        pl.BlockSpec((1, 1, 1, BLK, LANES), lambda b, h, i: (b, h + off, i, 0, 0)),
        pl.BlockSpec((1, 1, nb, BLK, LANES), lambda b, h, i: (b, h + off, 0, 0, 0)),
        pl.BlockSpec((1, 1, nb, LANES, BLK), lambda b, h, i: (b, h + off, 0, 0, 0)),
    ]
    scratch = [pltpu.VMEM((1, 2 * BLK), f32), pltpu.VMEM((1, 2 * BLK), f32),
               pltpu.VMEM((LANES, 2 * BLK), f32)]
    if moba:
        (kmean,) = aux
        in_specs.append(pl.BlockSpec((1, nb, LANES), lambda b, h, i: (b, 0, h)))
        scratch.append(pltpu.VMEM((nb, 2 * BLK), f32))
    else:
        lamvec, subg = aux
        in_specs.append(pl.BlockSpec(lamvec.shape, lambda b, h, i: (0, 0)))
        in_specs.append(pl.BlockSpec(subg.shape, lambda b, h, i: (0, 0)))
    return pl.pallas_call(
        functools.partial(_attn_kernel, moba=moba, lam_init=lam_init),
        grid=(bsz, PAIRS, nb),
        in_specs=in_specs,
        out_specs=pl.BlockSpec((1, BLK, LANES), lambda b, h, i: (b, i, h)),
        out_shape=jax.ShapeDtypeStruct((bsz, nb * BLK, PAIRS * LANES), bf16),
        scratch_shapes=scratch,
        compiler_params=pltpu.CompilerParams(
            dimension_semantics=("arbitrary", "arbitrary", "arbitrary"),
            vmem_limit_bytes=VMEM_LIMIT),
        name="moba_attn" if moba else "diff_attn",
    )(q_all, k_all, vt_all, *aux)


def _merge_kernel(oa_ref, ob_ref, sig_ref, x_ref, wa_ref, wb_ref, wo_ref, g2_ref, x1_ref, h2_ref):
    d = x_ref.shape[-1]
    pa = jnp.dot(oa_ref[0], wa_ref[...], preferred_element_type=f32)
    pb = jnp.dot(ob_ref[0], wb_ref[...], preferred_element_type=f32)
    merged = sig_ref[0, :, :d].astype(f32) * pa + sig_ref[0, :, d:].astype(f32) * pb
    x1 = x_ref[0] + jnp.dot(merged.astype(bf16), wo_ref[...], preferred_element_type=f32)
    x1_ref[0] = x1
    ms = jnp.mean(x1 * x1, axis=-1, keepdims=True)
    h2_ref[0] = (x1 * lax.rsqrt(ms + EPS) * g2_ref[...]).astype(bf16)


def _merge(oa, ob, sig, x, wa, wb, wo, g2, *, tm):
    bsz, seq, d = x.shape
    const = lambda b, i: (0, 0)
    tile = lambda b, i: (b, i, 0)
    return pl.pallas_call(
        _merge_kernel,
        grid=(bsz, seq // tm),
        in_specs=[
            pl.BlockSpec((1, tm, oa.shape[-1]), tile),
            pl.BlockSpec((1, tm, ob.shape[-1]), tile),
            pl.BlockSpec((1, tm, sig.shape[-1]), tile),
            pl.BlockSpec((1, tm, d), tile),
            pl.BlockSpec(wa.shape, const),
            pl.BlockSpec(wb.shape, const),
            pl.BlockSpec(wo.shape, const),
            pl.BlockSpec((1, d), const),
        ],
        out_specs=[pl.BlockSpec((1, tm, d), tile), pl.BlockSpec((1, tm, d), tile)],
        out_shape=[jax.ShapeDtypeStruct((bsz, seq, d), f32),
                   jax.ShapeDtypeStruct((bsz, seq, d), bf16)],
        compiler_params=pltpu.CompilerParams(
            dimension_semantics=("arbitrary", "arbitrary"), vmem_limit_bytes=VMEM_LIMIT),
        name="merge_outproj",
    )(oa, ob, sig, x, wa, wb, wo, g2)


HALO = 16


def _mlp_kernel(h2_ref, halo_ref, x1_ref, wu_ref, wg_ref, cw_ref, cb_ref, wd_ref, o_ref,
                ext_scr, u_scr, acc_scr, *, tm, tiles_per_seq):
    i = pl.program_id(0)
    c = pl.program_id(1)

    @pl.when(c == 0)
    def _():
        ext_scr[HALO:, :] = h2_ref[...]

    @pl.when((c == 0) & (i % tiles_per_seq == 0))
    def _():
        ext_scr[:HALO, :] = jnp.zeros((HALO, ext_scr.shape[1]), bf16)

    @pl.when((c == 0) & (i % tiles_per_seq != 0))
    def _():
        ext_scr[:HALO, :] = halo_ref[...]

    u_scr[...] = jnp.dot(ext_scr[...], wu_ref[...], preferred_element_type=f32)
    g = jnp.dot(h2_ref[...], wg_ref[...], preferred_element_type=f32)
    y = cb_ref[...]
    for j in range(CONV_W):
        y = y + u_scr[pl.ds(HALO - (CONV_W - 1) + j, tm), :] * cw_ref[j:j + 1, :]
    act = (jax.nn.gelu(y) * g).astype(bf16)
    part = jnp.dot(act, wd_ref[...], preferred_element_type=f32)

    @pl.when(c == 0)
    def _():
        acc_scr[...] = part

    @pl.when(c != 0)
    def _():
        acc_scr[...] = acc_scr[...] + part

    @pl.when(c == pl.num_programs(1) - 1)
    def _():
        o_ref[...] = x1_ref[...] + acc_scr[...]


def _mlp(h2, x1, w_up_bf, conv_w, conv_b, w_down_bf, *, seq, tm, tf):
    n, d = h2.shape
    dff = w_down_bf.shape[0]
    nch = dff // tf
    return pl.pallas_call(
        functools.partial(_mlp_kernel, tm=tm, tiles_per_seq=seq // tm),
        grid=(n // tm, nch),
        in_specs=[
            pl.BlockSpec((tm, d), lambda i, c: (i, 0)),
            pl.BlockSpec((HALO, d), lambda i, c: (jnp.maximum(i * (tm // HALO) - 1, 0), 0)),
            pl.BlockSpec((tm, d), lambda i, c: (i, 0)),
            pl.BlockSpec((d, tf), lambda i, c: (0, c)),
            pl.BlockSpec((d, tf), lambda i, c: (0, c + nch)),
            pl.BlockSpec((CONV_W, tf), lambda i, c: (0, c)),
            pl.BlockSpec((1, tf), lambda i, c: (0, c)),
            pl.BlockSpec((tf, d), lambda i, c: (c, 0)),
        ],
        out_specs=pl.BlockSpec((tm, d), lambda i, c: (i, 0)),
        out_shape=jax.ShapeDtypeStruct((n, d), f32),
        scratch_shapes=[pltpu.VMEM((tm + HALO, d), bf16), pltpu.VMEM((tm + HALO, tf), f32),
                        pltpu.VMEM((tm, d), f32)],
        compiler_params=pltpu.CompilerParams(
            dimension_semantics=("arbitrary", "arbitrary"), vmem_limit_bytes=VMEM_LIMIT),
        name="conv_mlp",
    )(h2, h2, x1, w_up_bf, w_up_bf, conv_w, conv_b, w_down_bf)


def _rope_tables(seq):
    inv = 1.0 / (ROPE_THETA ** (jnp.arange(0, HEAD_DIM, 2, dtype=f32) / HEAD_DIM))
    ang = jnp.arange(seq, dtype=f32)[:, None] * inv[None, :]
    cos, sin = jnp.cos(ang), jnp.sin(ang)
    reps = LANES // HEAD_DIM
    return (jnp.tile(cos, (1, 2 * reps)), jnp.tile(jnp.concatenate([-sin, sin], axis=1), (1, reps)))


def kernel(x, norm1_g, w_in, b_gate, qn_a, kn_a, lam_q1, lam_k1, lam_q2, lam_k2, subln_g, qn_b, kn_b,
           w_a_proj, w_b_proj, w_out, norm2_g, w_up, conv_w, conv_b, w_down):
    bsz, seq, d = x.shape
    depth = w_in.shape[0]
    assert seq % 512 == 0 and d % LANES == 0
    cos_t, sin_t = _rope_tables(seq)
    seg = jnp.arange(LANES) // HEAD_DIM
    bd = (seg[:, None] == seg[None, :]).astype(bf16)
    reps = LANES // HEAD_DIM
    qscale = HEAD_DIM ** -0.5 * LOG2E
    for l in range(depth):
        lam_init = 0.8 - 0.6 * math.exp(-0.3 * l)
        gains = jnp.stack([jnp.tile(qn_a[l], reps) * qscale, jnp.tile(kn_a[l], reps),
                           jnp.tile(qn_b[l], reps) * qscale, jnp.tile(kn_b[l], reps)])
        q_all, k_all, vt_all, kmean, sig = _inproj(
            x, norm1_g[l][None], w_in[l].astype(bf16), b_gate[l][None], gains, cos_t, sin_t, bd, tm=512)
        kmean = kmean.reshape(bsz, seq // BLK, B_WIDTH)
        lamvec = jnp.stack([lam_q1[l], lam_k1[l], lam_q2[l], lam_k2[l]])
        oa = _attention(q_all, k_all, vt_all, (lamvec, subln_g[l][None]), moba=False, lam_init=lam_init)
        ob = _attention(q_all, k_all, vt_all, (kmean,), moba=True, lam_init=lam_init)
        x1, h2 = _merge(oa, ob, sig, x, w_a_proj[l].astype(bf16), w_b_proj[l].astype(bf16),
                        w_out[l].astype(bf16), norm2_g[l][None], tm=512)
        dff = w_down.shape[1]
        x = _mlp(h2.reshape(bsz * seq, d), x1.reshape(bsz * seq, d), w_up[l].astype(bf16), conv_w[l],
                 conv_b[l][None], w_down[l].astype(bf16), seq=seq, tm=512, tf=dff // 2).reshape(bsz, seq, d)
    return x
```

```python
import functools
import math

import jax
import jax.numpy as jnp
from jax import lax
from jax.experimental import pallas as pl
from jax.experimental.pallas import tpu as pltpu

HEAD_DIM = 64
A_HEADS = 4
A_VDIM = 2 * HEAD_DIM
A_WIDTH = A_HEADS * A_VDIM
B_HEADS = 8
B_WIDTH = B_HEADS * HEAD_DIM
B_BLOCK = 256
B_TOPK = 3
CONV_W = 3
ROPE_THETA = 10000.0
EPS = 1e-6

LANES = 128
BLK = B_BLOCK
PAIRS = A_WIDTH // LANES
LOG2E = 1.4426950408889634
NEG_BIG = -1e30
POS_BIG = 1e30
VMEM_LIMIT = 56 * 1024 * 1024

f32 = jnp.float32
bf16 = jnp.bfloat16


def _nt_dot(a, b, **kw):
    return lax.dot_general(a, b, (((1,), (1,)), ((), ())), preferred_element_type=f32, **kw)


def _inproj_kernel(x_ref, g1_ref, w_ref, bg_ref, gain_ref, cos_ref, sin_ref, bd_ref,
                   q_ref, k_ref, vt_ref, kmean_ref, sig_ref, h_scr, *, tm):
    nblk = tm // BLK
    x = x_ref[0]
    ms = jnp.mean(x * x, axis=-1, keepdims=True)
    h_scr[...] = (x * lax.rsqrt(ms + EPS) * g1_ref[...]).astype(bf16)
    cos = cos_ref[...]
    sin = sin_ref[...]
    bd = bd_ref[...]
    lane = lax.broadcasted_iota(jnp.int32, (tm, LANES), 1)
    first_half = (lane & (HEAD_DIM - 1)) < (HEAD_DIM // 2)

    def col_tile(j):
        return jnp.dot(h_scr[...], w_ref[:, j * A_WIDTH:(j + 1) * A_WIDTH],
                       preferred_element_type=f32)

    def norm_rope(y, gain):
        y2 = y * y
        hi = y2.astype(bf16)
        lo = (y2 - hi.astype(f32)).astype(bf16)
        ss = (jnp.dot(hi, bd, preferred_element_type=f32)
              + jnp.dot(lo, bd, preferred_element_type=f32))
        yn = y * lax.rsqrt(ss * (1.0 / HEAD_DIM) + EPS) * gain
        rot = jnp.where(first_half, pltpu.roll(yn, LANES - HEAD_DIM // 2, 1),
                        pltpu.roll(yn, HEAD_DIM // 2, 1))
        return yn * cos + rot * sin

    for branch in range(2):
        acc_q = col_tile(3 * branch)
        for c in range(PAIRS):
            out = norm_rope(acc_q[:, c * LANES:(c + 1) * LANES], gain_ref[2 * branch:2 * branch + 1, :])
            for bi in range(nblk):
                q_ref[0, branch * PAIRS + c, bi] = out[bi * BLK:(bi + 1) * BLK].astype(bf16)
        acc_k = col_tile(3 * branch + 1)
        for c in range(PAIRS):
            out = norm_rope(acc_k[:, c * LANES:(c + 1) * LANES],
                            gain_ref[2 * branch + 1:2 * branch + 2, :])
            for bi in range(nblk):
                blk = out[bi * BLK:(bi + 1) * BLK]
                k_ref[0, branch * PAIRS + c, bi] = blk.astype(bf16)
                if branch == 1:
                    kmean_ref[0, bi:bi + 1, c * LANES:(c + 1) * LANES] = jnp.mean(
                        blk, axis=0, keepdims=True)
        acc_v = col_tile(3 * branch + 2)
        for c in range(PAIRS):
            for bi in range(nblk):
                blk = acc_v[bi * BLK:(bi + 1) * BLK, c * LANES:(c + 1) * LANES]
                vt_ref[0, branch * PAIRS + c, bi] = blk.T.astype(bf16)

    for jj in range(4):
        acc = col_tile(6 + jj) + bg_ref[:, jj * A_WIDTH:(jj + 1) * A_WIDTH]
        sig_ref[0, :, jj * A_WIDTH:(jj + 1) * A_WIDTH] = jax.nn.sigmoid(acc).astype(bf16)


def _inproj(x, g1, w_in_bf, b_gate, gains, cos_t, sin_t, bd, *, tm):
    bsz, seq, d = x.shape
    n_cols = w_in_bf.shape[1]
    nblk = tm // BLK
    spt = seq // tm
    nb = seq // BLK
    qk_shape = jax.ShapeDtypeStruct((bsz, 2 * PAIRS, nb, BLK, LANES), bf16)
    vt_shape = jax.ShapeDtypeStruct((bsz, 2 * PAIRS, nb, LANES, BLK), bf16)
    qk_spec = pl.BlockSpec((1, 2 * PAIRS, nblk, BLK, LANES), lambda b, i: (b, 0, i, 0, 0))
    vt_spec = pl.BlockSpec((1, 2 * PAIRS, nblk, LANES, BLK), lambda b, i: (b, 0, i, 0, 0))
    return pl.pallas_call(
        functools.partial(_inproj_kernel, tm=tm),
        grid=(bsz, spt),
        in_specs=[
            pl.BlockSpec((1, tm, d), lambda b, i: (b, i, 0)),
            pl.BlockSpec((1, d), lambda b, i: (0, 0)),
            pl.BlockSpec((d, n_cols), lambda b, i: (0, 0)),
            pl.BlockSpec((1, b_gate.shape[1]), lambda b, i: (0, 0)),
            pl.BlockSpec((4, LANES), lambda b, i: (0, 0)),
            pl.BlockSpec((tm, LANES), lambda b, i: (i, 0)),
            pl.BlockSpec((tm, LANES), lambda b, i: (i, 0)),
            pl.BlockSpec((LANES, LANES), lambda b, i: (0, 0)),
        ],
        out_specs=[
            qk_spec, qk_spec, vt_spec,
            pl.BlockSpec((1, nblk, B_WIDTH), lambda b, i: (b * spt + i, 0, 0)),
            pl.BlockSpec((1, tm, b_gate.shape[1]), lambda b, i: (b, i, 0)),
        ],
        out_shape=[
            qk_shape, qk_shape, vt_shape,
            jax.ShapeDtypeStruct((bsz * spt, nblk, B_WIDTH), f32),
            jax.ShapeDtypeStruct((bsz, seq, b_gate.shape[1]), bf16),
        ],
        scratch_shapes=[pltpu.VMEM((tm, d), bf16)],
        compiler_params=pltpu.CompilerParams(
            dimension_semantics=("arbitrary", "arbitrary"), vmem_limit_bytes=VMEM_LIMIT),
        name="inproj",
    )(x, g1, w_in_bf, b_gate, gains, cos_t, sin_t, bd)


PIPE_U = 2


def _attn_kernel(itab_ref, jtab_ref, *refs, moba, lam_init, n_steps):
    if moba:
        (q_ref, k_ref, vt_ref, kmean_ref, o_ref,
         qs_scr, m_scr, l_scr, acc_scr, s_scr, sel_scr) = refs
    else:
        (q_ref, k_ref, vt_ref, lam_ref, subg_ref, o_ref,
         qs_scr, m_scr, l_scr, acc_scr, s_scr) = refs
    nb = k_ref.shape[2]
    lane = lax.broadcasted_iota(jnp.int32, (BLK, LANES), 1)
    kidx = lax.broadcasted_iota(jnp.int32, (BLK, 2 * BLK), 0)
    ridx = lax.broadcasted_iota(jnp.int32, (BLK, 2 * BLK), 1) & (BLK - 1)
    causal = kidx <= ridx

    def init_tile(i, carry):
        q = q_ref[0, 0, i].astype(f32)
        qs = jnp.concatenate([jnp.where(lane < HEAD_DIM, q, 0.0),
                              jnp.where(lane >= HEAD_DIM, q, 0.0)], axis=0).astype(bf16)
        qs_scr[i] = qs
        if moba:
            g = _nt_dot(kmean_ref[0], qs.astype(f32), precision=lax.Precision.HIGHEST)
            nidx = lax.broadcasted_iota(jnp.int32, (nb, 2 * BLK), 0)
            past = nidx < i
            g = jnp.where(past, g, NEG_BIG)
            cnt = jnp.zeros((nb, 2 * BLK), f32)
            for mm in range(nb):
                row = g[mm:mm + 1, :]
                beats = (row > g) | ((row == g) & (nidx > mm))
                cnt = cnt + jnp.where(beats, 1.0, 0.0)
            sel_scr[i] = jnp.where((cnt < float(B_TOPK)) & past, 1.0, 0.0)
        s = jnp.where(causal, _nt_dot(k_ref[0, 0, i], qs), NEG_BIG)
        m = jnp.max(s, axis=0, keepdims=True)
        p = jnp.exp2(s - m)
        m_scr[i] = m
        l_scr[i] = jnp.sum(p, axis=0, keepdims=True)
        acc_scr[i] = jnp.dot(vt_ref[0, 0, i], p.astype(bf16), preferred_element_type=f32)
        return carry

    lax.fori_loop(0, nb, init_tile, 0)

    def qk(step, slot, u):
        step = jnp.minimum(step, n_steps - 1)
        s_scr[slot, u] = _nt_dot(k_ref[0, 0, jtab_ref[step]], qs_scr[itab_ref[step]])

    def update(step, slot, u):
        i = itab_ref[step]
        j = jtab_ref[step]
        s = s_scr[slot, u]
        m_old = m_scr[i]
        m_new = jnp.maximum(m_old, jnp.max(s, axis=0, keepdims=True))
        m_sub = m_new
        if moba:
            chosen = sel_scr[i, pl.ds(j, 1), :] > 0.5
            m_new = jnp.where(chosen, m_new, m_old)
            m_sub = jnp.where(chosen, m_new, POS_BIG)
        alpha = jnp.exp2(m_old - m_new)
        p = jnp.exp2(s - m_sub)
        l_scr[i] = alpha * l_scr[i] + jnp.sum(p, axis=0, keepdims=True)
        acc_scr[i] = alpha * acc_scr[i] + jnp.dot(vt_ref[0, 0, j], p.astype(bf16),
                                                  preferred_element_type=f32)
        m_scr[i] = m_new

    for u in range(PIPE_U):
        qk(u, 0, u)

    def tick_pair(it, carry):
        base = it * (2 * PIPE_U)
        for u in range(PIPE_U):
            qk(base + PIPE_U + u, 1, u)
        for u in range(PIPE_U):
            update(base + u, 0, u)
        for u in range(PIPE_U):
            qk(base + 2 * PIPE_U + u, 0, u)
        for u in range(PIPE_U):
            update(base + PIPE_U + u, 1, u)
        return carry

    lax.fori_loop(0, n_steps // (2 * PIPE_U), tick_pair, 0)

    if not moba:
        lv = lam_ref[...]
        lam = (jnp.exp(jnp.sum(lv[0:1] * lv[1:2], axis=-1, keepdims=True))
               - jnp.exp(jnp.sum(lv[2:3] * lv[3:4], axis=-1, keepdims=True)) + lam_init)

    def finish_tile(i, carry):
        o = acc_scr[i] * (1.0 / l_scr[i])
        rows = pl.ds(pl.multiple_of(i * BLK, BLK), BLK)
        if moba:
            ot = jnp.concatenate([o[:HEAD_DIM, :BLK], o[HEAD_DIM:, BLK:]], axis=0).T
            o_ref[0, rows, :] = ot.astype(bf16)
        else:
            ot = (o[:, :BLK] - lam * o[:, BLK:]).T
            ms = jnp.mean(ot * ot, axis=-1, keepdims=True)
            o_ref[0, rows, :] = (ot * lax.rsqrt(ms + EPS) * subg_ref[...]
                                 * (1.0 - lam_init)).astype(bf16)
        return carry

    lax.fori_loop(0, nb, finish_tile, 0)


def _attention(q_all, k_all, vt_all, aux, *, moba, lam_init):
    bsz, _, nb, _, _ = q_all.shape
    off = PAIRS if moba else 0
    pairs = [(i, j) for j in range(nb) for i in range(j + 1, nb)]
    n_steps = len(pairs)
    assert n_steps % (2 * PIPE_U) == 0
    itab = jnp.asarray([p[0] for p in pairs], jnp.int32)
    jtab = jnp.asarray([p[1] for p in pairs], jnp.int32)
    in_specs = [
        pl.BlockSpec((1, 1, nb, BLK, LANES), lambda b, h, it, jt: (b, h + off, 0, 0, 0)),
        pl.BlockSpec((1, 1, nb, BLK, LANES), lambda b, h, it, jt: (b, h + off, 0, 0, 0)),
        pl.BlockSpec((1, 1, nb, LANES, BLK), lambda b, h, it, jt: (b, h + off, 0, 0, 0)),
    ]
    scratch = [pltpu.VMEM((nb, 2 * BLK, LANES), bf16),
               pltpu.VMEM((nb, 1, 2 * BLK), f32), pltpu.VMEM((nb, 1, 2 * BLK), f32),
               pltpu.VMEM((nb, LANES, 2 * BLK), f32),
               pltpu.VMEM((2, PIPE_U, BLK, 2 * BLK), f32)]
    if moba:
        (kmean,) = aux
        in_specs.append(pl.BlockSpec((1, nb, LANES), lambda b, h, it, jt: (b, 0, h)))
        scratch.append(pltpu.VMEM((nb, nb, 2 * BLK), f32))
    else:
        lamvec, subg = aux
        in_specs.append(pl.BlockSpec(lamvec.shape, lambda b, h, it, jt: (0, 0)))
        in_specs.append(pl.BlockSpec(subg.shape, lambda b, h, it, jt: (0, 0)))
    return pl.pallas_call(
        functools.partial(_attn_kernel, moba=moba, lam_init=lam_init, n_steps=n_steps),
        grid_spec=pltpu.PrefetchScalarGridSpec(
            num_scalar_prefetch=2,
            grid=(bsz, PAIRS),
            in_specs=in_specs,
            out_specs=pl.BlockSpec((1, nb * BLK, LANES), lambda b, h, it, jt: (b, 0, h)),
            scratch_shapes=scratch),
        out_shape=jax.ShapeDtypeStruct((bsz, nb * BLK, PAIRS * LANES), bf16),
        compiler_params=pltpu.CompilerParams(
            dimension_semantics=("arbitrary", "arbitrary"), vmem_limit_bytes=VMEM_LIMIT),
        name="moba_attn" if moba else "diff_attn",
    )(itab, jtab, q_all, k_all, vt_all, *aux)


def _merge_kernel(oa_ref, ob_ref, sig_ref, x_ref, wa_ref, wb_ref, wo_ref, g2_ref, x1_ref, h2_ref):
    d = x_ref.shape[-1]
    pa = jnp.dot(oa_ref[0], wa_ref[...], preferred_element_type=f32)
    pb = jnp.dot(ob_ref[0], wb_ref[...], preferred_element_type=f32)
    merged = sig_ref[0, :, :d].astype(f32) * pa + sig_ref[0, :, d:].astype(f32) * pb
    x1 = x_ref[0] + jnp.dot(merged.astype(bf16), wo_ref[...], preferred_element_type=f32)
    x1_ref[0] = x1
    ms = jnp.mean(x1 * x1, axis=-1, keepdims=True)
    h2_ref[0] = (x1 * lax.rsqrt(ms + EPS) * g2_ref[...]).astype(bf16)


def _merge(oa, ob, sig, x, wa, wb, wo, g2, *, tm):
    bsz, seq, d = x.shape
    const = lambda b, i: (0, 0)
    tile = lambda b, i: (b, i, 0)
    return pl.pallas_call(
        _merge_kernel,
        grid=(bsz, seq // tm),
        in_specs=[
            pl.BlockSpec((1, tm, oa.shape[-1]), tile),
            pl.BlockSpec((1, tm, ob.shape[-1]), tile),
            pl.BlockSpec((1, tm, sig.shape[-1]), tile),
            pl.BlockSpec((1, tm, d), tile),
            pl.BlockSpec(wa.shape, const),
            pl.BlockSpec(wb.shape, const),
            pl.BlockSpec(wo.shape, const),
            pl.BlockSpec((1, d), const),
        ],
        out_specs=[pl.BlockSpec((1, tm, d), tile), pl.BlockSpec((1, tm, d), tile)],
        out_shape=[jax.ShapeDtypeStruct((bsz, seq, d), f32),
                   jax.ShapeDtypeStruct((bsz, seq, d), bf16)],
        compiler_params=pltpu.CompilerParams(
            dimension_semantics=("arbitrary", "arbitrary"), vmem_limit_bytes=VMEM_LIMIT),
        name="merge_outproj",
    )(oa, ob, sig, x, wa, wb, wo, g2)


HALO = 16


def _mlp_kernel(h2_ref, halo_ref, x1_ref, wu_ref, wg_ref, cw_ref, cb_ref, wd_ref, o_ref,
                ext_scr, u_scr, acc_scr, *, tm, tiles_per_seq):
    i = pl.program_id(0)
    c = pl.program_id(1)

    @pl.when(c == 0)
    def _():
        ext_scr[HALO:, :] = h2_ref[...]

    @pl.when((c == 0) & (i % tiles_per_seq == 0))
    def _():
        ext_scr[:HALO, :] = jnp.zeros((HALO, ext_scr.shape[1]), bf16)

    @pl.when((c == 0) & (i % tiles_per_seq != 0))
    def _():
        ext_scr[:HALO, :] = halo_ref[...]

    u_scr[...] = jnp.dot(ext_scr[...], wu_ref[...], preferred_element_type=f32)
    g = jnp.dot(h2_ref[...], wg_ref[...], preferred_element_type=f32)
    y = cb_ref[...]
    for j in range(CONV_W):
        y = y + u_scr[pl.ds(HALO - (CONV_W - 1) + j, tm), :] * cw_ref[j:j + 1, :]
    act = (jax.nn.gelu(y) * g).astype(bf16)
    part = jnp.dot(act, wd_ref[...], preferred_element_type=f32)

    @pl.when(c == 0)
    def _():
        acc_scr[...] = part

    @pl.when(c != 0)
    def _():
        acc_scr[...] = acc_scr[...] + part

    @pl.when(c == pl.num_programs(1) - 1)
    def _():
        o_ref[...] = x1_ref[...] + acc_scr[...]


def _mlp(h2, x1, w_up_bf, conv_w, conv_b, w_down_bf, *, seq, tm, tf):
    n, d = h2.shape
    dff = w_down_bf.shape[0]
    nch = dff // tf
    return pl.pallas_call(
        functools.partial(_mlp_kernel, tm=tm, tiles_per_seq=seq // tm),
        grid=(n // tm, nch),
        in_specs=[
            pl.BlockSpec((tm, d), lambda i, c: (i, 0)),
            pl.BlockSpec((HALO, d), lambda i, c: (jnp.maximum(i * (tm // HALO) - 1, 0), 0)),
            pl.BlockSpec((tm, d), lambda i, c: (i, 0)),
            pl.BlockSpec((d, tf), lambda i, c: (0, c)),
            pl.BlockSpec((d, tf), lambda i, c: (0, c + nch)),
            pl.BlockSpec((CONV_W, tf), lambda i, c: (0, c)),
            pl.BlockSpec((1, tf), lambda i, c: (0, c)),
            pl.BlockSpec((tf, d), lambda i, c: (c, 0)),
        ],
        out_specs=pl.BlockSpec((tm, d), lambda i, c: (i, 0)),
        out_shape=jax.ShapeDtypeStruct((n, d), f32),
        scratch_shapes=[pltpu.VMEM((tm + HALO, d), bf16), pltpu.VMEM((tm + HALO, tf), f32),
                        pltpu.VMEM((tm, d), f32)],
        compiler_params=pltpu.CompilerParams(
            dimension_semantics=("arbitrary", "arbitrary"), vmem_limit_bytes=VMEM_LIMIT),
        name="conv_mlp",
    )(h2, h2, x1, w_up_bf, w_up_bf, conv_w, conv_b, w_down_bf)


def _rope_tables(seq):
    inv = 1.0 / (ROPE_THETA ** (jnp.arange(0, HEAD_DIM, 2, dtype=f32) / HEAD_DIM))
    ang = jnp.arange(seq, dtype=f32)[:, None] * inv[None, :]
    cos, sin = jnp.cos(ang), jnp.sin(ang)
    reps = LANES // HEAD_DIM
    return (jnp.tile(cos, (1, 2 * reps)), jnp.tile(jnp.concatenate([-sin, sin], axis=1), (1, reps)))


def kernel(x, norm1_g, w_in, b_gate, qn_a, kn_a, lam_q1, lam_k1, lam_q2, lam_k2, subln_g, qn_b, kn_b,
           w_a_proj, w_b_proj, w_out, norm2_g, w_up, conv_w, conv_b, w_down):
    bsz, seq, d = x.shape
    depth = w_in.shape[0]
    assert seq % 512 == 0 and d % LANES == 0
    cos_t, sin_t = _rope_tables(seq)
    seg = jnp.arange(LANES) // HEAD_DIM
    bd = (seg[:, None] == seg[None, :]).astype(bf16)
    reps = LANES // HEAD_DIM
    qscale = HEAD_DIM ** -0.5 * LOG2E
    for l in range(depth):
        lam_init = 0.8 - 0.6 * math.exp(-0.3 * l)
        gains = jnp.stack([jnp.tile(qn_a[l], reps) * qscale, jnp.tile(kn_a[l], reps),
                           jnp.tile(qn_b[l], reps) * qscale, jnp.tile(kn_b[l], reps)])
        q_all, k_all, vt_all, kmean, sig = _inproj(
            x, norm1_g[l][None], w_in[l].astype(bf16), b_gate[l][None], gains, cos_t, sin_t, bd, tm=512)
        kmean = kmean.reshape(bsz, seq // BLK, B_WIDTH)
        lamvec = jnp.stack([lam_q1[l], lam_k1[l], lam_q2[l], lam_k2[l]])
        oa = _attention(q_all, k_all, vt_all, (lamvec, subln_g[l][None]), moba=False, lam_init=lam_init)
        ob = _attention(q_all, k_all, vt_all, (kmean,), moba=True, lam_init=lam_init)
        x1, h2 = _merge(oa, ob, sig, x, w_a_proj[l].astype(bf16), w_b_proj[l].astype(bf16),
                        w_out[l].astype(bf16), norm2_g[l][None], tm=512)
        dff = w_down.shape[1]
        x = _mlp(h2.reshape(bsz * seq, d), x1.reshape(bsz * seq, d), w_up[l].astype(bf16), conv_w[l],
                 conv_b[l][None], w_down[l].astype(bf16), seq=seq, tm=512, tf=dff // 2).reshape(bsz, seq, d)
    return x
```

```python
import functools
import math

import jax
import jax.numpy as jnp
from jax import lax
from jax.experimental import pallas as pl
from jax.experimental.pallas import tpu as pltpu

HEAD_DIM = 64
A_HEADS = 4
A_VDIM = 2 * HEAD_DIM
A_WIDTH = A_HEADS * A_VDIM
B_HEADS = 8
B_WIDTH = B_HEADS * HEAD_DIM
B_BLOCK = 256
B_TOPK = 3
CONV_W = 3
ROPE_THETA = 10000.0
EPS = 1e-6

LANES = 128
BLK = B_BLOCK
PAIRS = A_WIDTH // LANES
LOG2E = 1.4426950408889634
NEG_BIG = -1e30
POS_BIG = 1e30
VMEM_LIMIT = 56 * 1024 * 1024

f32 = jnp.float32
bf16 = jnp.bfloat16


def _nt_dot(a, b, **kw):
    return lax.dot_general(a, b, (((1,), (1,)), ((), ())), preferred_element_type=f32, **kw)


def _inproj_kernel(x_ref, g1_ref, w_ref, bg_ref, gain_ref, cos_ref, sin_ref, bd_ref,
                   q_ref, k_ref, vt_ref, kmean_ref, sig_ref, h_scr, *, tm):
    nblk = tm // BLK
    x = x_ref[0]
    ms = jnp.mean(x * x, axis=-1, keepdims=True)
    h_scr[...] = (x * lax.rsqrt(ms + EPS) * g1_ref[...]).astype(bf16)
    cos = cos_ref[...]
    sin = sin_ref[...]
    bd = bd_ref[...]
    lane = lax.broadcasted_iota(jnp.int32, (tm, LANES), 1)
    first_half = (lane & (HEAD_DIM - 1)) < (HEAD_DIM // 2)

    def col_tile(j):
        return jnp.dot(h_scr[...], w_ref[:, j * A_WIDTH:(j + 1) * A_WIDTH],
                       preferred_element_type=f32)

    def norm_rope(y, gain):
        y2 = y * y
        hi = y2.astype(bf16)
        lo = (y2 - hi.astype(f32)).astype(bf16)
        ss = (jnp.dot(hi, bd, preferred_element_type=f32)
              + jnp.dot(lo, bd, preferred_element_type=f32))
        yn = y * lax.rsqrt(ss * (1.0 / HEAD_DIM) + EPS) * gain
        rot = jnp.where(first_half, pltpu.roll(yn, LANES - HEAD_DIM // 2, 1),
                        pltpu.roll(yn, HEAD_DIM // 2, 1))
        return yn * cos + rot * sin

    for branch in range(2):
        acc_q = col_tile(3 * branch)
        for c in range(PAIRS):
            out = norm_rope(acc_q[:, c * LANES:(c + 1) * LANES], gain_ref[2 * branch:2 * branch + 1, :])
            for bi in range(nblk):
                q_ref[0, branch * PAIRS + c, bi] = out[bi * BLK:(bi + 1) * BLK].astype(bf16)
        acc_k = col_tile(3 * branch + 1)
        for c in range(PAIRS):
            out = norm_rope(acc_k[:, c * LANES:(c + 1) * LANES],
                            gain_ref[2 * branch + 1:2 * branch + 2, :])
            for bi in range(nblk):
                blk = out[bi * BLK:(bi + 1) * BLK]
                k_ref[0, branch * PAIRS + c, bi] = blk.astype(bf16)
                if branch == 1:
                    kmean_ref[0, bi:bi + 1, c * LANES:(c + 1) * LANES] = jnp.mean(
                        blk, axis=0, keepdims=True)
        acc_v = col_tile(3 * branch + 2)
        for c in range(PAIRS):
            for bi in range(nblk):
                blk = acc_v[bi * BLK:(bi + 1) * BLK, c * LANES:(c + 1) * LANES]
                vt_ref[0, branch * PAIRS + c, bi] = blk.T.astype(bf16)

    for jj in range(4):
        acc = col_tile(6 + jj) + bg_ref[:, jj * A_WIDTH:(jj + 1) * A_WIDTH]
        sig_ref[0, :, jj * A_WIDTH:(jj + 1) * A_WIDTH] = jax.nn.sigmoid(acc).astype(bf16)


def _inproj(x, g1, w_in_bf, b_gate, gains, cos_t, sin_t, bd, *, tm):
    bsz, seq, d = x.shape
    n_cols = w_in_bf.shape[1]
    nblk = tm // BLK
    spt = seq // tm
    nb = seq // BLK
    qk_shape = jax.ShapeDtypeStruct((bsz, 2 * PAIRS, nb, BLK, LANES), bf16)
    vt_shape = jax.ShapeDtypeStruct((bsz, 2 * PAIRS, nb, LANES, BLK), bf16)
    qk_spec = pl.BlockSpec((1, 2 * PAIRS, nblk, BLK, LANES), lambda b, i: (b, 0, i, 0, 0))
    vt_spec = pl.BlockSpec((1, 2 * PAIRS, nblk, LANES, BLK), lambda b, i: (b, 0, i, 0, 0))
    return pl.pallas_call(
        functools.partial(_inproj_kernel, tm=tm),
        grid=(bsz, spt),
        in_specs=[
            pl.BlockSpec((1, tm, d), lambda b, i: (b, i, 0)),
            pl.BlockSpec((1, d), lambda b, i: (0, 0)),
            pl.BlockSpec((d, n_cols), lambda b, i: (0, 0)),
            pl.BlockSpec((1, b_gate.shape[1]), lambda b, i: (0, 0)),
            pl.BlockSpec((4, LANES), lambda b, i: (0, 0)),
            pl.BlockSpec((tm, LANES), lambda b, i: (i, 0)),
            pl.BlockSpec((tm, LANES), lambda b, i: (i, 0)),
            pl.BlockSpec((LANES, LANES), lambda b, i: (0, 0)),
        ],
        out_specs=[
            qk_spec, qk_spec, vt_spec,
            pl.BlockSpec((1, nblk, B_WIDTH), lambda b, i: (b * spt + i, 0, 0)),
            pl.BlockSpec((1, tm, b_gate.shape[1]), lambda b, i: (b, i, 0)),
        ],
        out_shape=[
            qk_shape, qk_shape, vt_shape,
            jax.ShapeDtypeStruct((bsz * spt, nblk, B_WIDTH), f32),
            jax.ShapeDtypeStruct((bsz, seq, b_gate.shape[1]), bf16),
        ],
        scratch_shapes=[pltpu.VMEM((tm, d), bf16)],
        compiler_params=pltpu.CompilerParams(
            dimension_semantics=("arbitrary", "arbitrary"), vmem_limit_bytes=VMEM_LIMIT),
        name="inproj",
    )(x, g1, w_in_bf, b_gate, gains, cos_t, sin_t, bd)


PIPE_U = 4
TILE_UNROLL = 4


def _attn_kernel(itab_ref, jtab_ref, otab_ref, *refs, moba, lam_init, n_steps):
    if moba:
        (q_ref, k_ref, vt_ref, kmean_ref, o_ref,
         qs_scr, m_scr, l_scr, acc_scr, s_scr, sel_scr) = refs
    else:
        (q_ref, k_ref, vt_ref, lam_ref, subg_ref, o_ref,
         qs_scr, m_scr, l_scr, acc_scr, s_scr) = refs
    nb = k_ref.shape[2]
    lane = lax.broadcasted_iota(jnp.int32, (BLK, LANES), 1)
    key_minus_row = (lax.broadcasted_iota(jnp.int32, (BLK, 2 * BLK), 0)
                     - (lax.broadcasted_iota(jnp.int32, (BLK, 2 * BLK), 1) & (BLK - 1)))

    def init_tiles(t, carry):
        for r in range(TILE_UNROLL):
            i = t * TILE_UNROLL + r
            q = q_ref[0, 0, i].astype(f32)
            qs = jnp.concatenate([jnp.where(lane < HEAD_DIM, q, 0.0),
                                  jnp.where(lane >= HEAD_DIM, q, 0.0)], axis=0).astype(bf16)
            qs_scr[i] = qs
            m_scr[i] = jnp.full((1, 2 * BLK), NEG_BIG, f32)
            l_scr[i] = jnp.zeros((1, 2 * BLK), f32)
            acc_scr[i] = jnp.zeros((LANES, 2 * BLK), f32)
            if moba:
                g = _nt_dot(kmean_ref[0], qs.astype(f32), precision=lax.Precision.HIGHEST)
                nidx = lax.broadcasted_iota(jnp.int32, (nb, 2 * BLK), 0)
                past = nidx < i
                g = jnp.where(past, g, NEG_BIG)
                cnt = jnp.zeros((nb, 2 * BLK), f32)
                for mm in range(nb):
                    row = g[mm:mm + 1, :]
                    beats = (row > g) | ((row == g) & (nidx > mm))
                    cnt = cnt + jnp.where(beats, 1.0, 0.0)
                keep = ((cnt < float(B_TOPK)) & past) | (nidx == i)
                sel_scr[i] = jnp.where(keep, 1.0, 0.0)
        return carry

    lax.fori_loop(0, nb // TILE_UNROLL, init_tiles, 0)

    def qk(step, slot, u):
        step = jnp.minimum(step, n_steps - 1)
        s = _nt_dot(k_ref[0, 0, jtab_ref[step]], qs_scr[itab_ref[step]])
        if u == 0 and slot == 0:
            s = jnp.where(key_minus_row <= otab_ref[step], s, NEG_BIG)
        s_scr[slot, u] = s

    def update(step, slot, u):
        i = itab_ref[step]
        j = jtab_ref[step]
        s = s_scr[slot, u]
        m_old = m_scr[i]
        m_new = jnp.maximum(m_old, jnp.max(s, axis=0, keepdims=True))
        m_sub = m_new
        if moba:
            chosen = sel_scr[i, pl.ds(j, 1), :] > 0.5
            m_new = jnp.where(chosen, m_new, m_old)
            m_sub = jnp.where(chosen, m_new, POS_BIG)
        alpha = jnp.exp2(m_old - m_new)
        p = jnp.exp2(s - m_sub)
        l_scr[i] = alpha * l_scr[i] + jnp.sum(p, axis=0, keepdims=True)
        acc_scr[i] = alpha * acc_scr[i] + jnp.dot(vt_ref[0, 0, j], p.astype(bf16),
                                                  preferred_element_type=f32)
        m_scr[i] = m_new

    for u in range(PIPE_U):
        qk(u, 0, u)

    def tick_pair(it, carry):
        base = it * (2 * PIPE_U)
        for u in range(PIPE_U):
            qk(base + PIPE_U + u, 1, u)
        for u in range(PIPE_U):
            update(base + u, 0, u)
        for u in range(PIPE_U):
            qk(base + 2 * PIPE_U + u, 0, u)
        for u in range(PIPE_U):
            update(base + PIPE_U + u, 1, u)
        return carry

    lax.fori_loop(0, n_steps // (2 * PIPE_U), tick_pair, 0)

    if not moba:
        lv = lam_ref[...]
        lam = (jnp.exp(jnp.sum(lv[0:1] * lv[1:2], axis=-1, keepdims=True))
               - jnp.exp(jnp.sum(lv[2:3] * lv[3:4], axis=-1, keepdims=True)) + lam_init)

    def finish_tiles(t, carry):
        for r in range(TILE_UNROLL):
            i = t * TILE_UNROLL + r
            o = acc_scr[i] * (1.0 / l_scr[i])
            rows = pl.ds(pl.multiple_of(i * BLK, BLK), BLK)
            if moba:
                ot = jnp.concatenate([o[:HEAD_DIM, :BLK], o[HEAD_DIM:, BLK:]], axis=0).T
                o_ref[0, rows, :] = ot.astype(bf16)
            else:
                ot = (o[:, :BLK] - lam * o[:, BLK:]).T
                ms = jnp.mean(ot * ot, axis=-1, keepdims=True)
                o_ref[0, rows, :] = (ot * lax.rsqrt(ms + EPS) * subg_ref[...]
                                     * (1.0 - lam_init)).astype(bf16)
        return carry

    lax.fori_loop(0, nb // TILE_UNROLL, finish_tiles, 0)


def _step_tables(nb):
    per_trip = 2 * PIPE_U
    diag = [(i, i, 0) for i in range(nb)]
    rest = [(i, j, BLK) for j in range(nb) for i in range(j + 1, nb)]
    n_steps = len(diag) + len(rest)
    assert n_steps % per_trip == 0 and n_steps // per_trip >= len(diag)
    steps = []
    for _ in range(n_steps // per_trip):
        steps.append(diag.pop(0) if diag else rest.pop(0))
        for _ in range(per_trip - 1):
            steps.append(rest.pop(0))
    assert not diag and not rest
    return [jnp.asarray([s[c] for s in steps], jnp.int32) for c in range(3)], n_steps


def _attention(q_all, k_all, vt_all, aux, *, moba, lam_init):
    bsz, _, nb, _, _ = q_all.shape
    off = PAIRS if moba else 0
    assert nb % TILE_UNROLL == 0
    (itab, jtab, otab), n_steps = _step_tables(nb)
    in_specs = [
        pl.BlockSpec((1, 1, nb, BLK, LANES), lambda b, h, it, jt, ot: (b, h + off, 0, 0, 0)),
        pl.BlockSpec((1, 1, nb, BLK, LANES), lambda b, h, it, jt, ot: (b, h + off, 0, 0, 0)),
        pl.BlockSpec((1, 1, nb, LANES, BLK), lambda b, h, it, jt, ot: (b, h + off, 0, 0, 0)),
    ]
    scratch = [pltpu.VMEM((nb, 2 * BLK, LANES), bf16),
               pltpu.VMEM((nb, 1, 2 * BLK), f32), pltpu.VMEM((nb, 1, 2 * BLK), f32),
               pltpu.VMEM((nb, LANES, 2 * BLK), f32),
               pltpu.VMEM((2, PIPE_U, BLK, 2 * BLK), f32)]
    if moba:
        (kmean,) = aux
        in_specs.append(pl.BlockSpec((1, nb, LANES), lambda b, h, it, jt, ot: (b, 0, h)))
        scratch.append(pltpu.VMEM((nb, nb, 2 * BLK), f32))
    else:
        lamvec, subg = aux
        in_specs.append(pl.BlockSpec(lamvec.shape, lambda b, h, it, jt, ot: (0, 0)))
        in_specs.append(pl.BlockSpec(subg.shape, lambda b, h, it, jt, ot: (0, 0)))
    return pl.pallas_call(
        functools.partial(_attn_kernel, moba=moba, lam_init=lam_init, n_steps=n_steps),
        grid_spec=pltpu.PrefetchScalarGridSpec(
            num_scalar_prefetch=3,
            grid=(bsz, PAIRS),
            in_specs=in_specs,
            out_specs=pl.BlockSpec((1, nb * BLK, LANES), lambda b, h, it, jt, ot: (b, 0, h)),
            scratch_shapes=scratch),
        out_shape=jax.ShapeDtypeStruct((bsz, nb * BLK, PAIRS * LANES), bf16),
        compiler_params=pltpu.CompilerParams(
            dimension_semantics=("arbitrary", "arbitrary"), vmem_limit_bytes=VMEM_LIMIT),
        name="moba_attn" if moba else "diff_attn",
    )(itab, jtab, otab, q_all, k_all, vt_all, *aux)


def _merge_kernel(oa_ref, ob_ref, sig_ref, x_ref, wa_ref, wb_ref, wo_ref, g2_ref, x1_ref, h2_ref):
    d = x_ref.shape[-1]
    pa = jnp.dot(oa_ref[0], wa_ref[...], preferred_element_type=f32)
    pb = jnp.dot(ob_ref[0], wb_ref[...], preferred_element_type=f32)
    merged = sig_ref[0, :, :d].astype(f32) * pa + sig_ref[0, :, d:].astype(f32) * pb
    x1 = x_ref[0] + jnp.dot(merged.astype(bf16), wo_ref[...], preferred_element_type=f32)
    x1_ref[0] = x1
    ms = jnp.mean(x1 * x1, axis=-1, keepdims=True)
    h2_ref[0] = (x1 * lax.rsqrt(ms + EPS) * g2_ref[...]).astype(bf16)


def _merge(oa, ob, sig, x, wa, wb, wo, g2, *, tm):
    bsz, seq, d = x.shape
    const = lambda b, i: (0, 0)
    tile = lambda b, i: (b, i, 0)
    return pl.pallas_call(
        _merge_kernel,
        grid=(bsz, seq // tm),
        in_specs=[
            pl.BlockSpec((1, tm, oa.shape[-1]), tile),
            pl.BlockSpec((1, tm, ob.shape[-1]), tile),
            pl.BlockSpec((1, tm, sig.shape[-1]), tile),
            pl.BlockSpec((1, tm, d), tile),
            pl.BlockSpec(wa.shape, const),
            pl.BlockSpec(wb.shape, const),
            pl.BlockSpec(wo.shape, const),
            pl.BlockSpec((1, d), const),
        ],
        out_specs=[pl.BlockSpec((1, tm, d), tile), pl.BlockSpec((1, tm, d), tile)],
        out_shape=[jax.ShapeDtypeStruct((bsz, seq, d), f32),
                   jax.ShapeDtypeStruct((bsz, seq, d), bf16)],
        compiler_params=pltpu.CompilerParams(
            dimension_semantics=("arbitrary", "arbitrary"), vmem_limit_bytes=VMEM_LIMIT),
        name="merge_outproj",
    )(oa, ob, sig, x, wa, wb, wo, g2)


HALO = 16


def _mlp_kernel(h2_ref, halo_ref, x1_ref, wu_ref, wg_ref, cw_ref, cb_ref, wd_ref, o_ref,
                ext_scr, u_scr, acc_scr, *, tm, tiles_per_seq):
    i = pl.program_id(0)
    c = pl.program_id(1)

    @pl.when(c == 0)
    def _():
        ext_scr[HALO:, :] = h2_ref[...]

    @pl.when((c == 0) & (i % tiles_per_seq == 0))
    def _():
        ext_scr[:HALO, :] = jnp.zeros((HALO, ext_scr.shape[1]), bf16)

    @pl.when((c == 0) & (i % tiles_per_seq != 0))
    def _():
        ext_scr[:HALO, :] = halo_ref[...]

    u_scr[...] = jnp.dot(ext_scr[...], wu_ref[...], preferred_element_type=f32)
    g = jnp.dot(h2_ref[...], wg_ref[...], preferred_element_type=f32)
    y = cb_ref[...]
    for j in range(CONV_W):
        y = y + u_scr[pl.ds(HALO - (CONV_W - 1) + j, tm), :] * cw_ref[j:j + 1, :]
    act = (jax.nn.gelu(y) * g).astype(bf16)
    part = jnp.dot(act, wd_ref[...], preferred_element_type=f32)

    @pl.when(c == 0)
    def _():
        acc_scr[...] = part

    @pl.when(c != 0)
    def _():
        acc_scr[...] = acc_scr[...] + part

    @pl.when(c == pl.num_programs(1) - 1)
    def _():
        o_ref[...] = x1_ref[...] + acc_scr[...]


def _mlp(h2, x1, w_up_bf, conv_w, conv_b, w_down_bf, *, seq, tm, tf):
    n, d = h2.shape
    dff = w_down_bf.shape[0]
    nch = dff // tf
    return pl.pallas_call(
        functools.partial(_mlp_kernel, tm=tm, tiles_per_seq=seq // tm),
        grid=(n // tm, nch),
        in_specs=[
            pl.BlockSpec((tm, d), lambda i, c: (i, 0)),
            pl.BlockSpec((HALO, d), lambda i, c: (jnp.maximum(i * (tm // HALO) - 1, 0), 0)),
            pl.BlockSpec((tm, d), lambda i, c: (i, 0)),
            pl.BlockSpec((d, tf), lambda i, c: (0, c)),
            pl.BlockSpec((d, tf), lambda i, c: (0, c + nch)),
            pl.BlockSpec((CONV_W, tf), lambda i, c: (0, c)),
            pl.BlockSpec((1, tf), lambda i, c: (0, c)),
            pl.BlockSpec((tf, d), lambda i, c: (c, 0)),
        ],
        out_specs=pl.BlockSpec((tm, d), lambda i, c: (i, 0)),
        out_shape=jax.ShapeDtypeStruct((n, d), f32),
        scratch_shapes=[pltpu.VMEM((tm + HALO, d), bf16), pltpu.VMEM((tm + HALO, tf), f32),
                        pltpu.VMEM((tm, d), f32)],
        compiler_params=pltpu.CompilerParams(
            dimension_semantics=("arbitrary", "arbitrary"), vmem_limit_bytes=VMEM_LIMIT),
        name="conv_mlp",
    )(h2, h2, x1, w_up_bf, w_up_bf, conv_w, conv_b, w_down_bf)


def _rope_tables(seq):
    inv = 1.0 / (ROPE_THETA ** (jnp.arange(0, HEAD_DIM, 2, dtype=f32) / HEAD_DIM))
    ang = jnp.arange(seq, dtype=f32)[:, None] * inv[None, :]
    cos, sin = jnp.cos(ang), jnp.sin(ang)
    reps = LANES // HEAD_DIM
    return (jnp.tile(cos, (1, 2 * reps)), jnp.tile(jnp.concatenate([-sin, sin], axis=1), (1, reps)))


def kernel(x, norm1_g, w_in, b_gate, qn_a, kn_a, lam_q1, lam_k1, lam_q2, lam_k2, subln_g, qn_b, kn_b,
           w_a_proj, w_b_proj, w_out, norm2_g, w_up, conv_w, conv_b, w_down):
    bsz, seq, d = x.shape
    depth = w_in.shape[0]
    assert seq % 512 == 0 and d % LANES == 0
    cos_t, sin_t = _rope_tables(seq)
    seg = jnp.arange(LANES) // HEAD_DIM
    bd = (seg[:, None] == seg[None, :]).astype(bf16)
    reps = LANES // HEAD_DIM
    qscale = HEAD_DIM ** -0.5 * LOG2E
    for l in range(depth):
        lam_init = 0.8 - 0.6 * math.exp(-0.3 * l)
        gains = jnp.stack([jnp.tile(qn_a[l], reps) * qscale, jnp.tile(kn_a[l], reps),
                           jnp.tile(qn_b[l], reps) * qscale, jnp.tile(kn_b[l], reps)])
        q_all, k_all, vt_all, kmean, sig = _inproj(
            x, norm1_g[l][None], w_in[l].astype(bf16), b_gate[l][None], gains, cos_t, sin_t, bd, tm=512)
        kmean = kmean.reshape(bsz, seq // BLK, B_WIDTH)
        lamvec = jnp.stack([lam_q1[l], lam_k1[l], lam_q2[l], lam_k2[l]])
        oa = _attention(q_all, k_all, vt_all, (lamvec, subln_g[l][None]), moba=False, lam_init=lam_init)
        ob = _attention(q_all, k_all, vt_all, (kmean,), moba=True, lam_init=lam_init)
        x1, h2 = _merge(oa, ob, sig, x, w_a_proj[l].astype(bf16), w_b_proj[l].astype(bf16),
                        w_out[l].astype(bf16), norm2_g[l][None], tm=512)
        dff = w_down.shape[1]
        x = _mlp(h2.reshape(bsz * seq, d), x1.reshape(bsz * seq, d), w_up[l].astype(bf16), conv_w[l],
                 conv_b[l][None], w_down[l].astype(bf16), seq=seq, tm=512, tf=dff // 2).reshape(bsz, seq, d)
    return x
```

```python
import functools
import math

import jax
import jax.numpy as jnp
from jax import lax
from jax.experimental import pallas as pl
from jax.experimental.pallas import tpu as pltpu

HEAD_DIM = 64
A_HEADS = 4
A_VDIM = 2 * HEAD_DIM
A_WIDTH = A_HEADS * A_VDIM
B_HEADS = 8
B_WIDTH = B_HEADS * HEAD_DIM
B_BLOCK = 256
B_TOPK = 3
CONV_W = 3
ROPE_THETA = 10000.0
EPS = 1e-6

LANES = 128
BLK = B_BLOCK
PAIRS = A_WIDTH // LANES
LOG2E = 1.4426950408889634
NEG_BIG = -1e30
POS_BIG = 1e30
VMEM_LIMIT = 56 * 1024 * 1024

f32 = jnp.float32
bf16 = jnp.bfloat16


def _nt_dot(a, b, **kw):
    return lax.dot_general(a, b, (((1,), (1,)), ((), ())), preferred_element_type=f32, **kw)


def _inproj_kernel(x_ref, g1_ref, w_ref, bg_ref, gain_ref, cos_ref, sin_ref, bd_ref,
                   q_ref, k_ref, vt_ref, kmean_ref, sig_ref, h_scr, *, tm):
    nblk = tm // BLK
    x = x_ref[0]
    ms = jnp.mean(x * x, axis=-1, keepdims=True)
    h_scr[...] = (x * lax.rsqrt(ms + EPS) * g1_ref[...]).astype(bf16)
    cos = cos_ref[...]
    sin = sin_ref[...]
    bd = bd_ref[...]
    lane = lax.broadcasted_iota(jnp.int32, (tm, LANES), 1)
    first_half = (lane & (HEAD_DIM - 1)) < (HEAD_DIM // 2)

    def col_tile(j):
        return jnp.dot(h_scr[...], w_ref[:, j * A_WIDTH:(j + 1) * A_WIDTH],
                       preferred_element_type=f32)

    def norm_rope(y, gain):
        y2 = y * y
        hi = y2.astype(bf16)
        lo = (y2 - hi.astype(f32)).astype(bf16)
        ss = (jnp.dot(hi, bd, preferred_element_type=f32)
              + jnp.dot(lo, bd, preferred_element_type=f32))
        yn = y * lax.rsqrt(ss * (1.0 / HEAD_DIM) + EPS) * gain
        rot = jnp.where(first_half, pltpu.roll(yn, LANES - HEAD_DIM // 2, 1),
                        pltpu.roll(yn, HEAD_DIM // 2, 1))
        return yn * cos + rot * sin

    for branch in range(2):
        acc_q = col_tile(3 * branch)
        for c in range(PAIRS):
            out = norm_rope(acc_q[:, c * LANES:(c + 1) * LANES], gain_ref[2 * branch:2 * branch + 1, :])
            for bi in range(nblk):
                q_ref[0, branch * PAIRS + c, bi] = out[bi * BLK:(bi + 1) * BLK].astype(bf16)
        acc_k = col_tile(3 * branch + 1)
        for c in range(PAIRS):
            out = norm_rope(acc_k[:, c * LANES:(c + 1) * LANES],
                            gain_ref[2 * branch + 1:2 * branch + 2, :])
            for bi in range(nblk):
                blk = out[bi * BLK:(bi + 1) * BLK]
                k_ref[0, branch * PAIRS + c, bi] = blk.astype(bf16)
                if branch == 1:
                    kmean_ref[0, bi:bi + 1, c * LANES:(c + 1) * LANES] = jnp.mean(
                        blk, axis=0, keepdims=True)
        acc_v = col_tile(3 * branch + 2)
        for c in range(PAIRS):
            for bi in range(nblk):
                blk = acc_v[bi * BLK:(bi + 1) * BLK, c * LANES:(c + 1) * LANES]
                vt_ref[0, branch * PAIRS + c, bi] = blk.T.astype(bf16)

    for jj in range(4):
        acc = col_tile(6 + jj) + bg_ref[:, jj * A_WIDTH:(jj + 1) * A_WIDTH]
        sig_ref[0, :, jj * A_WIDTH:(jj + 1) * A_WIDTH] = jax.nn.sigmoid(acc).astype(bf16)


def _inproj(x, g1, w_in_bf, b_gate, gains, cos_t, sin_t, bd, *, tm):
    bsz, seq, d = x.shape
    n_cols = w_in_bf.shape[1]
    nblk = tm // BLK
    spt = seq // tm
    nb = seq // BLK
    qk_shape = jax.ShapeDtypeStruct((bsz, 2 * PAIRS, nb, BLK, LANES), bf16)
    vt_shape = jax.ShapeDtypeStruct((bsz, 2 * PAIRS, nb, LANES, BLK), bf16)
    qk_spec = pl.BlockSpec((1, 2 * PAIRS, nblk, BLK, LANES), lambda b, i: (b, 0, i, 0, 0))
    vt_spec = pl.BlockSpec((1, 2 * PAIRS, nblk, LANES, BLK), lambda b, i: (b, 0, i, 0, 0))
    return pl.pallas_call(
        functools.partial(_inproj_kernel, tm=tm),
        grid=(bsz, spt),
        in_specs=[
            pl.BlockSpec((1, tm, d), lambda b, i: (b, i, 0)),
            pl.BlockSpec((1, d), lambda b, i: (0, 0)),
            pl.BlockSpec((d, n_cols), lambda b, i: (0, 0)),
            pl.BlockSpec((1, b_gate.shape[1]), lambda b, i: (0, 0)),
            pl.BlockSpec((4, LANES), lambda b, i: (0, 0)),
            pl.BlockSpec((tm, LANES), lambda b, i: (i, 0)),
            pl.BlockSpec((tm, LANES), lambda b, i: (i, 0)),
            pl.BlockSpec((LANES, LANES), lambda b, i: (0, 0)),
        ],
        out_specs=[
            qk_spec, qk_spec, vt_spec,
            pl.BlockSpec((1, nblk, B_WIDTH), lambda b, i: (b * spt + i, 0, 0)),
            pl.BlockSpec((1, tm, b_gate.shape[1]), lambda b, i: (b, i, 0)),
        ],
        out_shape=[
            qk_shape, qk_shape, vt_shape,
            jax.ShapeDtypeStruct((bsz * spt, nblk, B_WIDTH), f32),
            jax.ShapeDtypeStruct((bsz, seq, b_gate.shape[1]), bf16),
        ],
        scratch_shapes=[pltpu.VMEM((tm, d), bf16)],
        compiler_params=pltpu.CompilerParams(
            dimension_semantics=("arbitrary", "arbitrary"), vmem_limit_bytes=VMEM_LIMIT),
        name="inproj",
    )(x, g1, w_in_bf, b_gate, gains, cos_t, sin_t, bd)


PIPE_U = 4
TILE_UNROLL = 4
BOUND_SLACK = 1.0 + 2.0 ** -6
MAX_FIXED_BOUND = 60.0


def _attn_kernel(itab_ref, jtab_ref, otab_ref, *refs, moba, online, lam_init, n_steps):
    if moba:
        (q_ref, k_ref, vt_ref, mref_ref, kmean_ref, o_ref,
         qs_scr, m_scr, l_scr, acc_scr, s_scr, sel_scr) = refs
    else:
        (q_ref, k_ref, vt_ref, mref_ref, lam_ref, subg_ref, o_ref,
         qs_scr, m_scr, l_scr, acc_scr, s_scr) = refs
    nb = k_ref.shape[2]
    acc_rows = acc_scr.shape[1]
    lane = lax.broadcasted_iota(jnp.int32, (BLK, LANES), 1)
    key_minus_row = (lax.broadcasted_iota(jnp.int32, (BLK, 2 * BLK), 0)
                     - (lax.broadcasted_iota(jnp.int32, (BLK, 2 * BLK), 1) & (BLK - 1)))

    def init_tiles(t, carry):
        for r in range(TILE_UNROLL):
            i = t * TILE_UNROLL + r
            q = q_ref[0, 0, i].astype(f32)
            qs = jnp.concatenate([jnp.where(lane < HEAD_DIM, q, 0.0),
                                  jnp.where(lane >= HEAD_DIM, q, 0.0)], axis=0).astype(bf16)
            qs_scr[i] = qs
            if online:
                m_scr[i] = jnp.full((1, 2 * BLK), NEG_BIG, f32)
            l_scr[i] = jnp.zeros((1, 2 * BLK), f32)
            acc_scr[i] = jnp.zeros((acc_rows, 2 * BLK), f32)
            if moba:
                g = _nt_dot(kmean_ref[0], qs.astype(f32), precision=lax.Precision.HIGHEST)
                nidx = lax.broadcasted_iota(jnp.int32, (nb, 2 * BLK), 0)
                past = nidx < i
                g = jnp.where(past, g, NEG_BIG)
                cnt = jnp.zeros((nb, 2 * BLK), f32)
                for mm in range(nb):
                    row = g[mm:mm + 1, :]
                    beats = (row > g) | ((row == g) & (nidx > mm))
                    cnt = cnt + jnp.where(beats, 1.0, 0.0)
                keep = ((cnt < float(B_TOPK)) & past) | (nidx == i)
                sel_scr[i] = jnp.where(keep, 1.0, 0.0)
        return carry

    lax.fori_loop(0, nb // TILE_UNROLL, init_tiles, 0)

    def qk(step, slot, u):
        step = jnp.minimum(step, n_steps - 1)
        s = _nt_dot(k_ref[0, 0, jtab_ref[step]], qs_scr[itab_ref[step]])
        if u == 0 and slot == 0:
            s = jnp.where(key_minus_row <= otab_ref[step], s, NEG_BIG)
        s_scr[slot, u] = s

    def update(step, slot, u):
        i = itab_ref[step]
        j = jtab_ref[step]
        s = s_scr[slot, u]
        if moba:
            chosen = sel_scr[i, pl.ds(j, 1), :] > 0.5
        if online:
            m_old = m_scr[i]
            m_new = jnp.maximum(m_old, jnp.max(s, axis=0, keepdims=True))
            if moba:
                m_new = jnp.where(chosen, m_new, m_old)
            m_sub = m_new
            alpha = jnp.exp2(m_old - m_new)
            m_scr[i] = m_new
        else:
            m_sub = mref_ref[...]
        if moba:
            m_sub = jnp.where(chosen, m_sub, POS_BIG)
        p = jnp.exp2(s - m_sub)
        psum = jnp.sum(p, axis=0, keepdims=True)
        pb = p.astype(bf16)
        vt = vt_ref[0, 0, j]
        if moba:
            pv = jnp.concatenate(
                [jnp.dot(vt[:HEAD_DIM], pb[:, :BLK], preferred_element_type=f32),
                 jnp.dot(vt[HEAD_DIM:], pb[:, BLK:], preferred_element_type=f32)], axis=1)
        else:
            pv = jnp.dot(vt, pb, preferred_element_type=f32)
        if online:
            l_scr[i] = alpha * l_scr[i] + psum
            acc_scr[i] = alpha * acc_scr[i] + pv
        else:
            l_scr[i] = l_scr[i] + psum
            acc_scr[i] = acc_scr[i] + pv

    for u in range(PIPE_U):
        qk(u, 0, u)

    def tick_pair(it, carry):
        base = it * (2 * PIPE_U)
        for u in range(PIPE_U):
            qk(base + PIPE_U + u, 1, u)
        for u in range(PIPE_U):
            update(base + u, 0, u)
        for u in range(PIPE_U):
            qk(base + 2 * PIPE_U + u, 0, u)
        for u in range(PIPE_U):
            update(base + PIPE_U + u, 1, u)
        return carry

    lax.fori_loop(0, n_steps // (2 * PIPE_U), tick_pair, 0)

    if not moba:
        lv = lam_ref[...]
        lam = (jnp.exp(jnp.sum(lv[0:1] * lv[1:2], axis=-1, keepdims=True))
               - jnp.exp(jnp.sum(lv[2:3] * lv[3:4], axis=-1, keepdims=True)) + lam_init)

    def finish_tiles(t, carry):
        for r in range(TILE_UNROLL):
            i = t * TILE_UNROLL + r
            o = acc_scr[i] * (1.0 / l_scr[i])
            rows = pl.ds(pl.multiple_of(i * BLK, BLK), BLK)
            if moba:
                ot = jnp.concatenate([o[:, :BLK], o[:, BLK:]], axis=0).T
                o_ref[0, rows, :] = ot.astype(bf16)
            else:
                ot = (o[:, :BLK] - lam * o[:, BLK:]).T
                ms = jnp.mean(ot * ot, axis=-1, keepdims=True)
                o_ref[0, rows, :] = (ot * lax.rsqrt(ms + EPS) * subg_ref[...]
                                     * (1.0 - lam_init)).astype(bf16)
        return carry

    lax.fori_loop(0, nb // TILE_UNROLL, finish_tiles, 0)


def _step_tables(nb):
    per_trip = 2 * PIPE_U
    diag = [(i, i, 0) for i in range(nb)]
    rest = [(i, j, BLK) for j in range(nb) for i in range(j + 1, nb)]
    n_steps = len(diag) + len(rest)
    assert n_steps % per_trip == 0 and n_steps // per_trip >= len(diag)
    steps = []
    for _ in range(n_steps // per_trip):
        steps.append(diag.pop(0) if diag else rest.pop(0))
        for _ in range(per_trip - 1):
            steps.append(rest.pop(0))
    assert not diag and not rest
    return [jnp.asarray([s[c] for s in steps], jnp.int32) for c in range(3)], n_steps


def _attention(q_all, k_all, vt_all, aux, score_bound, *, moba, lam_init):
    bsz, _, nb, _, _ = q_all.shape
    off = PAIRS if moba else 0
    assert nb % TILE_UNROLL == 0
    (itab, jtab, otab), n_steps = _step_tables(nb)
    const = lambda b, h, it, jt, ot: (0, 0)
    in_specs = [
        pl.BlockSpec((1, 1, nb, BLK, LANES), lambda b, h, it, jt, ot: (b, h + off, 0, 0, 0)),
        pl.BlockSpec((1, 1, nb, BLK, LANES), lambda b, h, it, jt, ot: (b, h + off, 0, 0, 0)),
        pl.BlockSpec((1, 1, nb, LANES, BLK), lambda b, h, it, jt, ot: (b, h + off, 0, 0, 0)),
        pl.BlockSpec((1, 2 * BLK), const),
    ]
    scratch = [pltpu.VMEM((nb, 2 * BLK, LANES), bf16),
               pltpu.VMEM((nb, 1, 2 * BLK), f32), pltpu.VMEM((nb, 1, 2 * BLK), f32),
               pltpu.VMEM((nb, HEAD_DIM if moba else LANES, 2 * BLK), f32),
               pltpu.VMEM((2, PIPE_U, BLK, 2 * BLK), f32)]
    if moba:
        (kmean,) = aux
        in_specs.append(pl.BlockSpec((1, nb, LANES), lambda b, h, it, jt, ot: (b, 0, h)))
        scratch.append(pltpu.VMEM((nb, nb, 2 * BLK), f32))
    else:
        lamvec, subg = aux
        in_specs.append(pl.BlockSpec(lamvec.shape, const))
        in_specs.append(pl.BlockSpec(subg.shape, const))
    mref = jnp.full((1, 2 * BLK), score_bound, f32)

    def run(online):
        return pl.pallas_call(
            functools.partial(_attn_kernel, moba=moba, online=online, lam_init=lam_init,
                              n_steps=n_steps),
            grid_spec=pltpu.PrefetchScalarGridSpec(
                num_scalar_prefetch=3,
                grid=(bsz, PAIRS),
                in_specs=in_specs,
                out_specs=pl.BlockSpec((1, nb * BLK, LANES), lambda b, h, it, jt, ot: (b, 0, h)),
                scratch_shapes=scratch),
            out_shape=jax.ShapeDtypeStruct((bsz, nb * BLK, PAIRS * LANES), bf16),
            compiler_params=pltpu.CompilerParams(
                dimension_semantics=("arbitrary", "arbitrary"), vmem_limit_bytes=VMEM_LIMIT),
            name=("moba_attn" if moba else "diff_attn") + ("_online" if online else ""),
        )(itab, jtab, otab, q_all, k_all, vt_all, mref, *aux)

    return lax.cond(score_bound <= MAX_FIXED_BOUND, lambda: run(False), lambda: run(True))


def _merge_kernel(oa_ref, ob_ref, sig_ref, x_ref, wa_ref, wb_ref, wo_ref, g2_ref, x1_ref, h2_ref):
    d = x_ref.shape[-1]
    pa = jnp.dot(oa_ref[0], wa_ref[...], preferred_element_type=f32)
    pb = jnp.dot(ob_ref[0], wb_ref[...], preferred_element_type=f32)
    merged = sig_ref[0, :, :d].astype(f32) * pa + sig_ref[0, :, d:].astype(f32) * pb
    x1 = x_ref[0] + jnp.dot(merged.astype(bf16), wo_ref[...], preferred_element_type=f32)
    x1_ref[0] = x1
    ms = jnp.mean(x1 * x1, axis=-1, keepdims=True)
    h2_ref[0] = (x1 * lax.rsqrt(ms + EPS) * g2_ref[...]).astype(bf16)


def _merge(oa, ob, sig, x, wa, wb, wo, g2, *, tm):
    bsz, seq, d = x.shape
    const = lambda b, i: (0, 0)
    tile = lambda b, i: (b, i, 0)
    return pl.pallas_call(
        _merge_kernel,
        grid=(bsz, seq // tm),
        in_specs=[
            pl.BlockSpec((1, tm, oa.shape[-1]), tile),
            pl.BlockSpec((1, tm, ob.shape[-1]), tile),
            pl.BlockSpec((1, tm, sig.shape[-1]), tile),
            pl.BlockSpec((1, tm, d), tile),
            pl.BlockSpec(wa.shape, const),
            pl.BlockSpec(wb.shape, const),
            pl.BlockSpec(wo.shape, const),
            pl.BlockSpec((1, d), const),
        ],
        out_specs=[pl.BlockSpec((1, tm, d), tile), pl.BlockSpec((1, tm, d), tile)],
        out_shape=[jax.ShapeDtypeStruct((bsz, seq, d), f32),
                   jax.ShapeDtypeStruct((bsz, seq, d), bf16)],
        compiler_params=pltpu.CompilerParams(
            dimension_semantics=("arbitrary", "arbitrary"), vmem_limit_bytes=VMEM_LIMIT),
        name="merge_outproj",
    )(oa, ob, sig, x, wa, wb, wo, g2)


HALO = 16


def _mlp_kernel(h2_ref, halo_ref, x1_ref, wu_ref, wg_ref, cw_ref, cb_ref, wd_ref, o_ref,
                ext_scr, u_scr, acc_scr, *, tm, tiles_per_seq):
    i = pl.program_id(0)
    c = pl.program_id(1)

    @pl.when(c == 0)
    def _():
        ext_scr[HALO:, :] = h2_ref[...]

    @pl.when((c == 0) & (i % tiles_per_seq == 0))
    def _():
        ext_scr[:HALO, :] = jnp.zeros((HALO, ext_scr.shape[1]), bf16)

    @pl.when((c == 0) & (i % tiles_per_seq != 0))
    def _():
        ext_scr[:HALO, :] = halo_ref[...]

    u_scr[...] = jnp.dot(ext_scr[...], wu_ref[...], preferred_element_type=f32)
    g = jnp.dot(h2_ref[...], wg_ref[...], preferred_element_type=f32)
    y = cb_ref[...]
    for j in range(CONV_W):
        y = y + u_scr[pl.ds(HALO - (CONV_W - 1) + j, tm), :] * cw_ref[j:j + 1, :]
    act = (jax.nn.gelu(y) * g).astype(bf16)
    part = jnp.dot(act, wd_ref[...], preferred_element_type=f32)

    @pl.when(c == 0)
    def _():
        acc_scr[...] = part

    @pl.when(c != 0)
    def _():
        acc_scr[...] = acc_scr[...] + part

    @pl.when(c == pl.num_programs(1) - 1)
    def _():
        o_ref[...] = x1_ref[...] + acc_scr[...]


def _mlp(h2, x1, w_up_bf, conv_w, conv_b, w_down_bf, *, seq, tm, tf):
    n, d = h2.shape
    dff = w_down_bf.shape[0]
    nch = dff // tf
    return pl.pallas_call(
        functools.partial(_mlp_kernel, tm=tm, tiles_per_seq=seq // tm),
        grid=(n // tm, nch),
        in_specs=[
            pl.BlockSpec((tm, d), lambda i, c: (i, 0)),
            pl.BlockSpec((HALO, d), lambda i, c: (jnp.maximum(i * (tm // HALO) - 1, 0), 0)),
            pl.BlockSpec((tm, d), lambda i, c: (i, 0)),
            pl.BlockSpec((d, tf), lambda i, c: (0, c)),
            pl.BlockSpec((d, tf), lambda i, c: (0, c + nch)),
            pl.BlockSpec((CONV_W, tf), lambda i, c: (0, c)),
            pl.BlockSpec((1, tf), lambda i, c: (0, c)),
            pl.BlockSpec((tf, d), lambda i, c: (c, 0)),
        ],
        out_specs=pl.BlockSpec((tm, d), lambda i, c: (i, 0)),
        out_shape=jax.ShapeDtypeStruct((n, d), f32),
        scratch_shapes=[pltpu.VMEM((tm + HALO, d), bf16), pltpu.VMEM((tm + HALO, tf), f32),
                        pltpu.VMEM((tm, d), f32)],
        compiler_params=pltpu.CompilerParams(
            dimension_semantics=("arbitrary", "arbitrary"), vmem_limit_bytes=VMEM_LIMIT),
        name="conv_mlp",
    )(h2, h2, x1, w_up_bf, w_up_bf, conv_w, conv_b, w_down_bf)


def _rope_tables(seq):
    inv = 1.0 / (ROPE_THETA ** (jnp.arange(0, HEAD_DIM, 2, dtype=f32) / HEAD_DIM))
    ang = jnp.arange(seq, dtype=f32)[:, None] * inv[None, :]
    cos, sin = jnp.cos(ang), jnp.sin(ang)
    reps = LANES // HEAD_DIM
    return (jnp.tile(cos, (1, 2 * reps)), jnp.tile(jnp.concatenate([-sin, sin], axis=1), (1, reps)))


def kernel(x, norm1_g, w_in, b_gate, qn_a, kn_a, lam_q1, lam_k1, lam_q2, lam_k2, subln_g, qn_b, kn_b,
           w_a_proj, w_b_proj, w_out, norm2_g, w_up, conv_w, conv_b, w_down):
    bsz, seq, d = x.shape
    depth = w_in.shape[0]
    assert seq % 512 == 0 and d % LANES == 0
    cos_t, sin_t = _rope_tables(seq)
    seg = jnp.arange(LANES) // HEAD_DIM
    bd = (seg[:, None] == seg[None, :]).astype(bf16)
    reps = LANES // HEAD_DIM
    qscale = HEAD_DIM ** -0.5 * LOG2E
    for l in range(depth):
        lam_init = 0.8 - 0.6 * math.exp(-0.3 * l)
        gains = jnp.stack([jnp.tile(qn_a[l], reps) * qscale, jnp.tile(kn_a[l], reps),
                           jnp.tile(qn_b[l], reps) * qscale, jnp.tile(kn_b[l], reps)])
        q_all, k_all, vt_all, kmean, sig = _inproj(
            x, norm1_g[l][None], w_in[l].astype(bf16), b_gate[l][None], gains, cos_t, sin_t, bd, tm=512)
        kmean = kmean.reshape(bsz, seq // BLK, B_WIDTH)
        lamvec = jnp.stack([lam_q1[l], lam_k1[l], lam_q2[l], lam_k2[l]])
        gmax = jnp.max(jnp.abs(gains), axis=1) * (HEAD_DIM ** 0.5)
        oa = _attention(q_all, k_all, vt_all, (lamvec, subln_g[l][None]),
                        gmax[0] * gmax[1] * BOUND_SLACK, moba=False, lam_init=lam_init)
        ob = _attention(q_all, k_all, vt_all, (kmean,),
                        gmax[2] * gmax[3] * BOUND_SLACK, moba=True, lam_init=lam_init)
        x1, h2 = _merge(oa, ob, sig, x, w_a_proj[l].astype(bf16), w_b_proj[l].astype(bf16),
                        w_out[l].astype(bf16), norm2_g[l][None], tm=512)
        dff = w_down.shape[1]
        x = _mlp(h2.reshape(bsz * seq, d), x1.reshape(bsz * seq, d), w_up[l].astype(bf16), conv_w[l],
                 conv_b[l][None], w_down[l].astype(bf16), seq=seq, tm=512, tf=dff // 2).reshape(bsz, seq, d)
    return x
```

```python
import functools
import math

import jax
import jax.numpy as jnp
from jax import lax
from jax.experimental import pallas as pl
from jax.experimental.pallas import tpu as pltpu

HEAD_DIM = 64
A_HEADS = 4
A_VDIM = 2 * HEAD_DIM
A_WIDTH = A_HEADS * A_VDIM
B_HEADS = 8
B_WIDTH = B_HEADS * HEAD_DIM
B_BLOCK = 256
B_TOPK = 3
CONV_W = 3
ROPE_THETA = 10000.0
EPS = 1e-6

LANES = 128
BLK = B_BLOCK
PAIRS = A_WIDTH // LANES
LOG2E = 1.4426950408889634
NEG_BIG = -1e30
POS_BIG = 1e30
VMEM_LIMIT = 56 * 1024 * 1024

f32 = jnp.float32
bf16 = jnp.bfloat16


def _nt_dot(a, b, **kw):
    return lax.dot_general(a, b, (((1,), (1,)), ((), ())), preferred_element_type=f32, **kw)


def _inproj_kernel(x_ref, g1_ref, w_ref, bg_ref, gain_ref, cos_ref, sin_ref, bd_ref,
                   q_ref, k_ref, vt_ref, kmean_ref, sig_ref, h_scr, *, tm):
    nblk = tm // BLK
    x = x_ref[0]
    ms = jnp.mean(x * x, axis=-1, keepdims=True)
    h_scr[...] = (x * lax.rsqrt(ms + EPS) * g1_ref[...]).astype(bf16)
    cos = cos_ref[...]
    sin = sin_ref[...]
    bd = bd_ref[...]
    lane = lax.broadcasted_iota(jnp.int32, (tm, LANES), 1)
    first_half = (lane & (HEAD_DIM - 1)) < (HEAD_DIM // 2)

    def col_tile(j):
        return jnp.dot(h_scr[...], w_ref[:, j * A_WIDTH:(j + 1) * A_WIDTH],
                       preferred_element_type=f32)

    def norm_rope(y, gain):
        y2 = y * y
        hi = y2.astype(bf16)
        lo = (y2 - hi.astype(f32)).astype(bf16)
        ss = (jnp.dot(hi, bd, preferred_element_type=f32)
              + jnp.dot(lo, bd, preferred_element_type=f32))
        yn = y * lax.rsqrt(ss * (1.0 / HEAD_DIM) + EPS) * gain
        rot = jnp.where(first_half, pltpu.roll(yn, LANES - HEAD_DIM // 2, 1),
                        pltpu.roll(yn, HEAD_DIM // 2, 1))
        return yn * cos + rot * sin

    for branch in range(2):
        acc_q = col_tile(3 * branch)
        for c in range(PAIRS):
            out = norm_rope(acc_q[:, c * LANES:(c + 1) * LANES], gain_ref[2 * branch:2 * branch + 1, :])
            for bi in range(nblk):
                q_ref[0, branch * PAIRS + c, bi] = out[bi * BLK:(bi + 1) * BLK].astype(bf16)
        acc_k = col_tile(3 * branch + 1)
        for c in range(PAIRS):
            out = norm_rope(acc_k[:, c * LANES:(c + 1) * LANES],
                            gain_ref[2 * branch + 1:2 * branch + 2, :])
            for bi in range(nblk):
                blk = out[bi * BLK:(bi + 1) * BLK]
                k_ref[0, branch * PAIRS + c, bi] = blk.astype(bf16)
                if branch == 1:
                    kmean_ref[0, bi:bi + 1, c * LANES:(c + 1) * LANES] = jnp.mean(
                        blk, axis=0, keepdims=True)
        acc_v = col_tile(3 * branch + 2)
        for c in range(PAIRS):
            for bi in range(nblk):
                blk = acc_v[bi * BLK:(bi + 1) * BLK, c * LANES:(c + 1) * LANES]
                vt_ref[0, branch * PAIRS + c, bi] = blk.T.astype(bf16)

    for jj in range(4):
        acc = col_tile(6 + jj) + bg_ref[:, jj * A_WIDTH:(jj + 1) * A_WIDTH]
        sig_ref[0, :, jj * A_WIDTH:(jj + 1) * A_WIDTH] = jax.nn.sigmoid(acc).astype(bf16)


def _inproj(x, g1, w_in_bf, b_gate, gains, cos_t, sin_t, bd, *, tm):
    bsz, seq, d = x.shape
    n_cols = w_in_bf.shape[1]
    nblk = tm // BLK
    spt = seq // tm
    nb = seq // BLK
    qk_shape = jax.ShapeDtypeStruct((bsz, 2 * PAIRS, nb, BLK, LANES), bf16)
    vt_shape = jax.ShapeDtypeStruct((bsz, 2 * PAIRS, nb, LANES, BLK), bf16)
    qk_spec = pl.BlockSpec((1, 2 * PAIRS, nblk, BLK, LANES), lambda b, i: (b, 0, i, 0, 0))
    vt_spec = pl.BlockSpec((1, 2 * PAIRS, nblk, LANES, BLK), lambda b, i: (b, 0, i, 0, 0))
    return pl.pallas_call(
        functools.partial(_inproj_kernel, tm=tm),
        grid=(bsz, spt),
        in_specs=[
            pl.BlockSpec((1, tm, d), lambda b, i: (b, i, 0)),
            pl.BlockSpec((1, d), lambda b, i: (0, 0)),
            pl.BlockSpec((d, n_cols), lambda b, i: (0, 0)),
            pl.BlockSpec((1, b_gate.shape[1]), lambda b, i: (0, 0)),
            pl.BlockSpec((4, LANES), lambda b, i: (0, 0)),
            pl.BlockSpec((tm, LANES), lambda b, i: (i, 0)),
            pl.BlockSpec((tm, LANES), lambda b, i: (i, 0)),
            pl.BlockSpec((LANES, LANES), lambda b, i: (0, 0)),
        ],
        out_specs=[
            qk_spec, qk_spec, vt_spec,
            pl.BlockSpec((1, nblk, B_WIDTH), lambda b, i: (b * spt + i, 0, 0)),
            pl.BlockSpec((1, tm, b_gate.shape[1]), lambda b, i: (b, i, 0)),
        ],
        out_shape=[
            qk_shape, qk_shape, vt_shape,
            jax.ShapeDtypeStruct((bsz * spt, nblk, B_WIDTH), f32),
            jax.ShapeDtypeStruct((bsz, seq, b_gate.shape[1]), bf16),
        ],
        scratch_shapes=[pltpu.VMEM((tm, d), bf16)],
        compiler_params=pltpu.CompilerParams(
            dimension_semantics=("arbitrary", "arbitrary"), vmem_limit_bytes=VMEM_LIMIT),
        name="inproj",
    )(x, g1, w_in_bf, b_gate, gains, cos_t, sin_t, bd)


TRIP_STEPS = 34
TILE_UNROLL = 4
BOUND_SLACK = 1.0 + 2.0 ** -6
MAX_FIXED_BOUND = 60.0


def _attn_kernel(itab_ref, jtab_ref, otab_ref, *refs, moba, online, lam_init, n_steps, n_masked):
    if moba:
        (q_ref, k_ref, vt_ref, mref_ref, kmean_ref, o_ref,
         qs_scr, m_scr, l_scr, acc_scr, s_scr, sel_scr) = refs
    else:
        (q_ref, k_ref, vt_ref, mref_ref, lam_ref, subg_ref, o_ref,
         qs_scr, m_scr, l_scr, acc_scr, s_scr) = refs
    nb = k_ref.shape[2]
    acc_rows = acc_scr.shape[1]
    lane = lax.broadcasted_iota(jnp.int32, (BLK, LANES), 1)
    key_minus_row = (lax.broadcasted_iota(jnp.int32, (BLK, 2 * BLK), 0)
                     - (lax.broadcasted_iota(jnp.int32, (BLK, 2 * BLK), 1) & (BLK - 1)))

    if moba:
        km = kmean_ref[0]
        km_parts = []
        for _ in range(3):
            part = km.astype(bf16)
            km_parts.append(part)
            km = km - part.astype(f32)

    def init_tiles(t, carry):
        for r in range(TILE_UNROLL):
            i = t * TILE_UNROLL + r
            q = q_ref[0, 0, i].astype(f32)
            qs = jnp.concatenate([jnp.where(lane < HEAD_DIM, q, 0.0),
                                  jnp.where(lane >= HEAD_DIM, q, 0.0)], axis=0).astype(bf16)
            qs_scr[i] = qs
            if online:
                m_scr[i] = jnp.full((1, 2 * BLK), NEG_BIG, f32)
            l_scr[i] = jnp.zeros((1, 2 * BLK), f32)
            acc_scr[i] = jnp.zeros((acc_rows, 2 * BLK), f32)
            if moba:
                g = _nt_dot(km_parts[0], qs) + _nt_dot(km_parts[1], qs) + _nt_dot(km_parts[2], qs)
                nidx = lax.broadcasted_iota(jnp.int32, (nb, 2 * BLK), 0)
                past = nidx < i
                g = jnp.where(past, g, NEG_BIG)
                cnt = jnp.zeros((nb, 2 * BLK), f32)
                for mm in range(nb):
                    row = g[mm:mm + 1, :]
                    beats = (row > g) | ((row == g) & (nidx > mm))
                    cnt = cnt + jnp.where(beats, 1.0, 0.0)
                keep = ((cnt < float(B_TOPK)) & past) | (nidx == i)
                sel_scr[i] = jnp.where(keep, 1.0, 0.0)
        return carry

    lax.fori_loop(0, nb // TILE_UNROLL, init_tiles, 0)

    def qk(step, slot, u):
        step = jnp.minimum(step, n_steps - 1)
        s = _nt_dot(k_ref[0, 0, jtab_ref[step]], qs_scr[itab_ref[step]])
        if slot == 0 and u < n_masked:
            s = jnp.where(key_minus_row <= otab_ref[step], s, NEG_BIG)
        s_scr[slot, u] = s

    def update(step, slot, u):
        i = itab_ref[step]
        j = jtab_ref[step]
        s = s_scr[slot, u]
        if moba:
            chosen = sel_scr[i, pl.ds(j, 1), :] > 0.5
        if online:
            m_old = m_scr[i]
            m_new = jnp.maximum(m_old, jnp.max(s, axis=0, keepdims=True))
            if moba:
                m_new = jnp.where(chosen, m_new, m_old)
            m_sub = m_new
            alpha = jnp.exp2(m_old - m_new)
            m_scr[i] = m_new
        else:
            m_sub = mref_ref[...]
        if moba:
            m_sub = jnp.where(chosen, m_sub, POS_BIG)
        p = jnp.exp2(s - m_sub)
        psum = jnp.sum(p, axis=0, keepdims=True)
        pb = p.astype(bf16)
        vt = vt_ref[0, 0, j]
        if moba:
            pv = jnp.concatenate(
                [jnp.dot(vt[:HEAD_DIM], pb[:, :BLK], preferred_element_type=f32),
                 jnp.dot(vt[HEAD_DIM:], pb[:, BLK:], preferred_element_type=f32)], axis=1)
        else:
            pv = jnp.dot(vt, pb, preferred_element_type=f32)
        if online:
            l_scr[i] = alpha * l_scr[i] + psum
            acc_scr[i] = alpha * acc_scr[i] + pv
        else:
            l_scr[i] = l_scr[i] + psum
            acc_scr[i] = acc_scr[i] + pv

    def trip(it, carry):
        base = it * TRIP_STEPS
        for u in range(TRIP_STEPS):
            qk(base + u, 0, u)
        for u in range(TRIP_STEPS):
            update(base + u, 0, u)
        return carry

    lax.fori_loop(0, n_steps // TRIP_STEPS, trip, 0)

    if not moba:
        lv = lam_ref[...]
        lam = (jnp.exp(jnp.sum(lv[0:1] * lv[1:2], axis=-1, keepdims=True))
               - jnp.exp(jnp.sum(lv[2:3] * lv[3:4], axis=-1, keepdims=True)) + lam_init)

    def finish_tiles(t, carry):
        for r in range(TILE_UNROLL):
            i = t * TILE_UNROLL + r
            o = acc_scr[i] * (1.0 / l_scr[i])
            rows = pl.ds(pl.multiple_of(i * BLK, BLK), BLK)
            if moba:
                ot = jnp.concatenate([o[:, :BLK], o[:, BLK:]], axis=0).T
                o_ref[0, rows, :] = ot.astype(bf16)
            else:
                ot = (o[:, :BLK] - lam * o[:, BLK:]).T
                ms = jnp.mean(ot * ot, axis=-1, keepdims=True)
                o_ref[0, rows, :] = (ot * lax.rsqrt(ms + EPS) * subg_ref[...]
                                     * (1.0 - lam_init)).astype(bf16)
        return carry

    lax.fori_loop(0, nb // TILE_UNROLL, finish_tiles, 0)


def _step_tables(nb):
    per_trip = TRIP_STEPS
    diag = [(i, i, 0) for i in range(nb)]
    rest = [(i, j, BLK) for j in range(nb) for i in range(j + 1, nb)]
    n_steps = len(diag) + len(rest)
    assert n_steps % per_trip == 0
    n_trips = n_steps // per_trip
    n_masked = -(-len(diag) // n_trips)
    assert n_masked <= per_trip
    steps = []
    for _ in range(n_trips):
        for _ in range(n_masked):
            steps.append(diag.pop(0) if diag else rest.pop(0))
        for _ in range(per_trip - n_masked):
            steps.append(rest.pop(0))
    assert not diag and not rest
    return [jnp.asarray([s[c] for s in steps], jnp.int32) for c in range(3)], n_steps, n_masked


def _attention(q_all, k_all, vt_all, aux, score_bound, *, moba, lam_init):
    bsz, _, nb, _, _ = q_all.shape
    off = PAIRS if moba else 0
    assert nb % TILE_UNROLL == 0
    (itab, jtab, otab), n_steps, n_masked = _step_tables(nb)
    const = lambda b, h, it, jt, ot: (0, 0)
    in_specs = [
        pl.BlockSpec((1, 1, nb, BLK, LANES), lambda b, h, it, jt, ot: (b, h + off, 0, 0, 0)),
        pl.BlockSpec((1, 1, nb, BLK, LANES), lambda b, h, it, jt, ot: (b, h + off, 0, 0, 0)),
        pl.BlockSpec((1, 1, nb, LANES, BLK), lambda b, h, it, jt, ot: (b, h + off, 0, 0, 0)),
        pl.BlockSpec((1, 2 * BLK), const),
    ]
    scratch = [pltpu.VMEM((nb, 2 * BLK, LANES), bf16),
               pltpu.VMEM((nb, 1, 2 * BLK), f32), pltpu.VMEM((nb, 1, 2 * BLK), f32),
               pltpu.VMEM((nb, HEAD_DIM if moba else LANES, 2 * BLK), f32),
               pltpu.VMEM((1, TRIP_STEPS, BLK, 2 * BLK), f32)]
    if moba:
        (kmean,) = aux
        in_specs.append(pl.BlockSpec((1, nb, LANES), lambda b, h, it, jt, ot: (b, 0, h)))
        scratch.append(pltpu.VMEM((nb, nb, 2 * BLK), f32))
    else:
        lamvec, subg = aux
        in_specs.append(pl.BlockSpec(lamvec.shape, const))
        in_specs.append(pl.BlockSpec(subg.shape, const))
    mref = jnp.full((1, 2 * BLK), score_bound, f32)

    def run(online):
        return pl.pallas_call(
            functools.partial(_attn_kernel, moba=moba, online=online, lam_init=lam_init,
                              n_steps=n_steps, n_masked=n_masked),
            grid_spec=pltpu.PrefetchScalarGridSpec(
                num_scalar_prefetch=3,
                grid=(bsz, PAIRS),
                in_specs=in_specs,
                out_specs=pl.BlockSpec((1, nb * BLK, LANES), lambda b, h, it, jt, ot: (b, 0, h)),
                scratch_shapes=scratch),
            out_shape=jax.ShapeDtypeStruct((bsz, nb * BLK, PAIRS * LANES), bf16),
            compiler_params=pltpu.CompilerParams(
                dimension_semantics=("arbitrary", "arbitrary"), vmem_limit_bytes=VMEM_LIMIT),
            name=("moba_attn" if moba else "diff_attn") + ("_online" if online else ""),
        )(itab, jtab, otab, q_all, k_all, vt_all, mref, *aux)

    return lax.cond(score_bound <= MAX_FIXED_BOUND, lambda: run(False), lambda: run(True))


def _merge_kernel(oa_ref, ob_ref, sig_ref, x_ref, wa_ref, wb_ref, wo_ref, g2_ref, x1_ref, h2_ref):
    d = x_ref.shape[-1]
    pa = jnp.dot(oa_ref[0], wa_ref[...], preferred_element_type=f32)
    pb = jnp.dot(ob_ref[0], wb_ref[...], preferred_element_type=f32)
    merged = sig_ref[0, :, :d].astype(f32) * pa + sig_ref[0, :, d:].astype(f32) * pb
    x1 = x_ref[0] + jnp.dot(merged.astype(bf16), wo_ref[...], preferred_element_type=f32)
    x1_ref[0] = x1
    ms = jnp.mean(x1 * x1, axis=-1, keepdims=True)
    h2_ref[0] = (x1 * lax.rsqrt(ms + EPS) * g2_ref[...]).astype(bf16)


def _merge(oa, ob, sig, x, wa, wb, wo, g2, *, tm):
    bsz, seq, d = x.shape
    const = lambda b, i: (0, 0)
    tile = lambda b, i: (b, i, 0)
    return pl.pallas_call(
        _merge_kernel,
        grid=(bsz, seq // tm),
        in_specs=[
            pl.BlockSpec((1, tm, oa.shape[-1]), tile),
            pl.BlockSpec((1, tm, ob.shape[-1]), tile),
            pl.BlockSpec((1, tm, sig.shape[-1]), tile),
            pl.BlockSpec((1, tm, d), tile),
            pl.BlockSpec(wa.shape, const),
            pl.BlockSpec(wb.shape, const),
            pl.BlockSpec(wo.shape, const),
            pl.BlockSpec((1, d), const),
        ],
        out_specs=[pl.BlockSpec((1, tm, d), tile), pl.BlockSpec((1, tm, d), tile)],
        out_shape=[jax.ShapeDtypeStruct((bsz, seq, d), f32),
                   jax.ShapeDtypeStruct((bsz, seq, d), bf16)],
        compiler_params=pltpu.CompilerParams(
            dimension_semantics=("arbitrary", "arbitrary"), vmem_limit_bytes=VMEM_LIMIT),
        name="merge_outproj",
    )(oa, ob, sig, x, wa, wb, wo, g2)


HALO = 16


def _mlp_kernel(h2_ref, halo_ref, x1_ref, wu_ref, wg_ref, cw_ref, cb_ref, wd_ref, o_ref,
                ext_scr, u_scr, acc_scr, *, tm, tiles_per_seq):
    i = pl.program_id(0)
    c = pl.program_id(1)

    @pl.when(c == 0)
    def _():
        ext_scr[HALO:, :] = h2_ref[...]

    @pl.when((c == 0) & (i % tiles_per_seq == 0))
    def _():
        ext_scr[:HALO, :] = jnp.zeros((HALO, ext_scr.shape[1]), bf16)

    @pl.when((c == 0) & (i % tiles_per_seq != 0))
    def _():
        ext_scr[:HALO, :] = halo_ref[...]

    u_scr[...] = jnp.dot(ext_scr[...], wu_ref[...], preferred_element_type=f32)
    g = jnp.dot(h2_ref[...], wg_ref[...], preferred_element_type=f32)
    y = cb_ref[...]
    for j in range(CONV_W):
        y = y + u_scr[pl.ds(HALO - (CONV_W - 1) + j, tm), :] * cw_ref[j:j + 1, :]
    act = (jax.nn.gelu(y) * g).astype(bf16)
    part = jnp.dot(act, wd_ref[...], preferred_element_type=f32)

    @pl.when(c == 0)
    def _():
        acc_scr[...] = part

    @pl.when(c != 0)
    def _():
        acc_scr[...] = acc_scr[...] + part

    @pl.when(c == pl.num_programs(1) - 1)
    def _():
        o_ref[...] = x1_ref[...] + acc_scr[...]


def _mlp(h2, x1, w_up_bf, conv_w, conv_b, w_down_bf, *, seq, tm, tf):
    n, d = h2.shape
    dff = w_down_bf.shape[0]
    nch = dff // tf
    return pl.pallas_call(
        functools.partial(_mlp_kernel, tm=tm, tiles_per_seq=seq // tm),
        grid=(n // tm, nch),
        in_specs=[
            pl.BlockSpec((tm, d), lambda i, c: (i, 0)),
            pl.BlockSpec((HALO, d), lambda i, c: (jnp.maximum(i * (tm // HALO) - 1, 0), 0)),
            pl.BlockSpec((tm, d), lambda i, c: (i, 0)),
            pl.BlockSpec((d, tf), lambda i, c: (0, c)),
            pl.BlockSpec((d, tf), lambda i, c: (0, c + nch)),
            pl.BlockSpec((CONV_W, tf), lambda i, c: (0, c)),
            pl.BlockSpec((1, tf), lambda i, c: (0, c)),
            pl.BlockSpec((tf, d), lambda i, c: (c, 0)),
        ],
        out_specs=pl.BlockSpec((tm, d), lambda i, c: (i, 0)),
        out_shape=jax.ShapeDtypeStruct((n, d), f32),
        scratch_shapes=[pltpu.VMEM((tm + HALO, d), bf16), pltpu.VMEM((tm + HALO, tf), f32),
                        pltpu.VMEM((tm, d), f32)],
        compiler_params=pltpu.CompilerParams(
            dimension_semantics=("arbitrary", "arbitrary"), vmem_limit_bytes=VMEM_LIMIT),
        name="conv_mlp",
    )(h2, h2, x1, w_up_bf, w_up_bf, conv_w, conv_b, w_down_bf)


def _rope_tables(seq):
    inv = 1.0 / (ROPE_THETA ** (jnp.arange(0, HEAD_DIM, 2, dtype=f32) / HEAD_DIM))
    ang = jnp.arange(seq, dtype=f32)[:, None] * inv[None, :]
    cos, sin = jnp.cos(ang), jnp.sin(ang)
    reps = LANES // HEAD_DIM
    return (jnp.tile(cos, (1, 2 * reps)), jnp.tile(jnp.concatenate([-sin, sin], axis=1), (1, reps)))


def kernel(x, norm1_g, w_in, b_gate, qn_a, kn_a, lam_q1, lam_k1, lam_q2, lam_k2, subln_g, qn_b, kn_b,
           w_a_proj, w_b_proj, w_out, norm2_g, w_up, conv_w, conv_b, w_down):
    bsz, seq, d = x.shape
    depth = w_in.shape[0]
    assert seq % 512 == 0 and d % LANES == 0
    cos_t, sin_t = _rope_tables(seq)
    seg = jnp.arange(LANES) // HEAD_DIM
    bd = (seg[:, None] == seg[None, :]).astype(bf16)
    reps = LANES // HEAD_DIM
    qscale = HEAD_DIM ** -0.5 * LOG2E
    for l in range(depth):
        lam_init = 0.8 - 0.6 * math.exp(-0.3 * l)
        gains = jnp.stack([jnp.tile(qn_a[l], reps) * qscale, jnp.tile(kn_a[l], reps),
                           jnp.tile(qn_b[l], reps) * qscale, jnp.tile(kn_b[l], reps)])
        q_all, k_all, vt_all, kmean, sig = _inproj(
            x, norm1_g[l][None], w_in[l].astype(bf16), b_gate[l][None], gains, cos_t, sin_t, bd, tm=512)
        kmean = kmean.reshape(bsz, seq // BLK, B_WIDTH)
        lamvec = jnp.stack([lam_q1[l], lam_k1[l], lam_q2[l], lam_k2[l]])
        gmax = jnp.max(jnp.abs(gains), axis=1) * (HEAD_DIM ** 0.5)
        oa = _attention(q_all, k_all, vt_all, (lamvec, subln_g[l][None]),
                        gmax[0] * gmax[1] * BOUND_SLACK, moba=False, lam_init=lam_init)
        ob = _attention(q_all, k_all, vt_all, (kmean,),
                        gmax[2] * gmax[3] * BOUND_SLACK, moba=True, lam_init=lam_init)
        x1, h2 = _merge(oa, ob, sig, x, w_a_proj[l].astype(bf16), w_b_proj[l].astype(bf16),
                        w_out[l].astype(bf16), norm2_g[l][None], tm=512)
        dff = w_down.shape[1]
        x = _mlp(h2.reshape(bsz * seq, d), x1.reshape(bsz * seq, d), w_up[l].astype(bf16), conv_w[l],
                 conv_b[l][None], w_down[l].astype(bf16), seq=seq, tm=512, tf=dff // 2).reshape(bsz, seq, d)
    return x
```

```python
import functools
import math

import jax
import jax.numpy as jnp
from jax import lax
from jax.experimental import pallas as pl
from jax.experimental.pallas import tpu as pltpu

HEAD_DIM = 64
A_HEADS = 4
A_VDIM = 2 * HEAD_DIM
A_WIDTH = A_HEADS * A_VDIM
B_HEADS = 8
B_WIDTH = B_HEADS * HEAD_DIM
B_BLOCK = 256
B_TOPK = 3
CONV_W = 3
ROPE_THETA = 10000.0
EPS = 1e-6

LANES = 128
BLK = B_BLOCK
PAIRS = A_WIDTH // LANES
LOG2E = 1.4426950408889634
NEG_BIG = -1e30
POS_BIG = 1e30
VMEM_LIMIT = 56 * 1024 * 1024

f32 = jnp.float32
bf16 = jnp.bfloat16


def _nt_dot(a, b, **kw):
    return lax.dot_general(a, b, (((1,), (1,)), ((), ())), preferred_element_type=f32, **kw)


def _inproj_kernel(x_ref, g1_ref, w_ref, bg_ref, gain_ref, cos_ref, sin_ref, bd_ref,
                   q_ref, k_ref, vt_ref, kmean_ref, sig_ref, h_scr, *, tm):
    nblk = tm // BLK
    x = x_ref[0]
    ms = jnp.mean(x * x, axis=-1, keepdims=True)
    h_scr[...] = (x * lax.rsqrt(ms + EPS) * g1_ref[...]).astype(bf16)
    cos = cos_ref[...]
    sin = sin_ref[...]
    bd = bd_ref[...]
    lane = lax.broadcasted_iota(jnp.int32, (tm, LANES), 1)
    first_half = (lane & (HEAD_DIM - 1)) < (HEAD_DIM // 2)

    def col_tile(j):
        return jnp.dot(h_scr[...], w_ref[:, j * A_WIDTH:(j + 1) * A_WIDTH],
                       preferred_element_type=f32)

    def norm_rope(y, gain):
        y2 = y * y
        hi = y2.astype(bf16)
        lo = (y2 - hi.astype(f32)).astype(bf16)
        ss = (jnp.dot(hi, bd, preferred_element_type=f32)
              + jnp.dot(lo, bd, preferred_element_type=f32))
        yn = y * lax.rsqrt(ss * (1.0 / HEAD_DIM) + EPS) * gain
        rot = jnp.where(first_half, pltpu.roll(yn, LANES - HEAD_DIM // 2, 1),
                        pltpu.roll(yn, HEAD_DIM // 2, 1))
        return yn * cos + rot * sin

    for branch in range(2):
        acc_q = col_tile(3 * branch)
        for c in range(PAIRS):
            out = norm_rope(acc_q[:, c * LANES:(c + 1) * LANES], gain_ref[2 * branch:2 * branch + 1, :])
            for bi in range(nblk):
                q_ref[0, branch * PAIRS + c, bi] = out[bi * BLK:(bi + 1) * BLK].astype(bf16)
        acc_k = col_tile(3 * branch + 1)
        for c in range(PAIRS):
            out = norm_rope(acc_k[:, c * LANES:(c + 1) * LANES],
                            gain_ref[2 * branch + 1:2 * branch + 2, :])
            for bi in range(nblk):
                blk = out[bi * BLK:(bi + 1) * BLK]
                k_ref[0, branch * PAIRS + c, bi] = blk.astype(bf16)
                if branch == 1:
                    kmean_ref[0, bi:bi + 1, c * LANES:(c + 1) * LANES] = jnp.mean(
                        blk, axis=0, keepdims=True)
        acc_v = col_tile(3 * branch + 2)
        for c in range(PAIRS):
            for bi in range(nblk):
                blk = acc_v[bi * BLK:(bi + 1) * BLK, c * LANES:(c + 1) * LANES]
                vt_ref[0, branch * PAIRS + c, bi] = blk.T.astype(bf16)

    for jj in range(4):
        acc = col_tile(6 + jj) + bg_ref[:, jj * A_WIDTH:(jj + 1) * A_WIDTH]
        sig_ref[0, :, jj * A_WIDTH:(jj + 1) * A_WIDTH] = jax.nn.sigmoid(acc).astype(bf16)


def _inproj(x, g1, w_in_bf, b_gate, gains, cos_t, sin_t, bd, *, tm):
    bsz, seq, d = x.shape
    n_cols = w_in_bf.shape[1]
    nblk = tm // BLK
    spt = seq // tm
    nb = seq // BLK
    qk_shape = jax.ShapeDtypeStruct((bsz, 2 * PAIRS, nb, BLK, LANES), bf16)
    vt_shape = jax.ShapeDtypeStruct((bsz, 2 * PAIRS, nb, LANES, BLK), bf16)
    qk_spec = pl.BlockSpec((1, 2 * PAIRS, nblk, BLK, LANES), lambda b, i: (b, 0, i, 0, 0))
    vt_spec = pl.BlockSpec((1, 2 * PAIRS, nblk, LANES, BLK), lambda b, i: (b, 0, i, 0, 0))
    return pl.pallas_call(
        functools.partial(_inproj_kernel, tm=tm),
        grid=(bsz, spt),
        in_specs=[
            pl.BlockSpec((1, tm, d), lambda b, i: (b, i, 0)),
            pl.BlockSpec((1, d), lambda b, i: (0, 0)),
            pl.BlockSpec((d, n_cols), lambda b, i: (0, 0)),
            pl.BlockSpec((1, b_gate.shape[1]), lambda b, i: (0, 0)),
            pl.BlockSpec((4, LANES), lambda b, i: (0, 0)),
            pl.BlockSpec((tm, LANES), lambda b, i: (i, 0)),
            pl.BlockSpec((tm, LANES), lambda b, i: (i, 0)),
            pl.BlockSpec((LANES, LANES), lambda b, i: (0, 0)),
        ],
        out_specs=[
            qk_spec, qk_spec, vt_spec,
            pl.BlockSpec((1, nblk, B_WIDTH), lambda b, i: (b * spt + i, 0, 0)),
            pl.BlockSpec((1, tm, b_gate.shape[1]), lambda b, i: (b, i, 0)),
        ],
        out_shape=[
            qk_shape, qk_shape, vt_shape,
            jax.ShapeDtypeStruct((bsz * spt, nblk, B_WIDTH), f32),
            jax.ShapeDtypeStruct((bsz, seq, b_gate.shape[1]), bf16),
        ],
        scratch_shapes=[pltpu.VMEM((tm, d), bf16)],
        compiler_params=pltpu.CompilerParams(
            dimension_semantics=("arbitrary", "arbitrary"), vmem_limit_bytes=VMEM_LIMIT),
        name="inproj",
    )(x, g1, w_in_bf, b_gate, gains, cos_t, sin_t, bd)


TRIP_STEPS = 34
TILE_UNROLL = 4
BOUND_SLACK = 1.0 + 2.0 ** -6
MAX_FIXED_BOUND = 60.0


def _attn_kernel(itab_ref, jtab_ref, otab_ref, *refs, moba, online, lam_init, n_steps, n_masked):
    if moba:
        (q_ref, k_ref, vt_ref, mref_ref, kmean_ref, o_ref,
         qs_scr, m_scr, l_scr, acc_scr, s_scr, sel_scr) = refs
    else:
        (q_ref, k_ref, vt_ref, mref_ref, lam_ref, subg_ref, o_ref,
         qs_scr, m_scr, l_scr, acc_scr, s_scr) = refs
    nb = k_ref.shape[2]
    acc_rows = acc_scr.shape[1]
    lane = lax.broadcasted_iota(jnp.int32, (BLK, LANES), 1)
    key_minus_row = (lax.broadcasted_iota(jnp.int32, (BLK, 2 * BLK), 0)
                     - (lax.broadcasted_iota(jnp.int32, (BLK, 2 * BLK), 1) & (BLK - 1)))

    if moba:
        km = kmean_ref[0]
        km_parts = []
        for _ in range(3):
            part = km.astype(bf16)
            km_parts.append(part)
            km = km - part.astype(f32)

    def init_tiles(t, carry):
        for r in range(TILE_UNROLL):
            i = t * TILE_UNROLL + r
            q = q_ref[0, 0, i].astype(f32)
            qs = jnp.concatenate([jnp.where(lane < HEAD_DIM, q, 0.0),
                                  jnp.where(lane >= HEAD_DIM, q, 0.0)], axis=0).astype(bf16)
            qs_scr[i] = qs
            if online:
                m_scr[i] = jnp.full((1, 2 * BLK), NEG_BIG, f32)
            l_scr[i] = jnp.zeros((1, 2 * BLK), f32)
            acc_scr[i] = jnp.zeros((acc_rows, 2 * BLK), f32)
            if moba:
                g = _nt_dot(km_parts[0], qs) + _nt_dot(km_parts[1], qs) + _nt_dot(km_parts[2], qs)
                nidx = lax.broadcasted_iota(jnp.int32, (nb, 2 * BLK), 0)
                past = nidx < i
                g = jnp.where(past, g, NEG_BIG)
                cnt = jnp.zeros((nb, 2 * BLK), f32)
                for mm in range(nb):
                    row = g[mm:mm + 1, :]
                    beats = (row > g) | ((row == g) & (nidx > mm))
                    cnt = cnt + jnp.where(beats, 1.0, 0.0)
                keep = ((cnt < float(B_TOPK)) & past) | (nidx == i)
                sel_scr[i] = jnp.where(keep, 1.0, 0.0)
        return carry

    lax.fori_loop(0, nb // TILE_UNROLL, init_tiles, 0)

    def qk(step, slot, u):
        step = jnp.minimum(step, n_steps - 1)
        s = _nt_dot(k_ref[0, 0, jtab_ref[step]], qs_scr[itab_ref[step]])
        if slot == 0 and u < n_masked:
            s = jnp.where(key_minus_row <= otab_ref[step], s, NEG_BIG)
        s_scr[slot, u] = s

    def update(step, slot, u):
        i = itab_ref[step]
        j = jtab_ref[step]
        s = s_scr[slot, u]
        if moba:
            chosen = sel_scr[i, pl.ds(j, 1), :] > 0.5
        if online:
            m_old = m_scr[i]
            m_new = jnp.maximum(m_old, jnp.max(s, axis=0, keepdims=True))
            if moba:
                m_new = jnp.where(chosen, m_new, m_old)
            m_sub = m_new
            alpha = jnp.exp2(m_old - m_new)
            m_scr[i] = m_new
        else:
            m_sub = mref_ref[...]
        if moba:
            m_sub = jnp.where(chosen, m_sub, POS_BIG)
        p = jnp.exp2(s - m_sub)
        psum = jnp.sum(p, axis=0, keepdims=True)
        pb = p.astype(bf16)
        vt = vt_ref[0, 0, j]
        if moba:
            pv = jnp.concatenate(
                [jnp.dot(vt[:HEAD_DIM], pb[:, :BLK], preferred_element_type=f32),
                 jnp.dot(vt[HEAD_DIM:], pb[:, BLK:], preferred_element_type=f32)], axis=1)
        else:
            pv = jnp.dot(vt, pb, preferred_element_type=f32)
        if online:
            l_scr[i] = alpha * l_scr[i] + psum
            acc_scr[i] = alpha * acc_scr[i] + pv
        else:
            l_scr[i] = l_scr[i] + psum
            acc_scr[i] = acc_scr[i] + pv

    def trip(it, carry):
        base = it * TRIP_STEPS
        for u in range(TRIP_STEPS):
            qk(base + u, 0, u)
        for u in range(TRIP_STEPS):
            update(base + u, 0, u)
        return carry

    lax.fori_loop(0, n_steps // TRIP_STEPS, trip, 0)

    if not moba:
        lv = lam_ref[...]
        lam = (jnp.exp(jnp.sum(lv[0:1] * lv[1:2], axis=-1, keepdims=True))
               - jnp.exp(jnp.sum(lv[2:3] * lv[3:4], axis=-1, keepdims=True)) + lam_init)

    def finish_tiles(t, carry):
        for r in range(TILE_UNROLL):
            i = t * TILE_UNROLL + r
            o = acc_scr[i] * (1.0 / l_scr[i])
            rows = pl.ds(pl.multiple_of(i * BLK, BLK), BLK)
            if moba:
                ot = jnp.concatenate([o[:, :BLK], o[:, BLK:]], axis=0).T
                o_ref[0, rows, :] = ot.astype(bf16)
            else:
                ot = (o[:, :BLK] - lam * o[:, BLK:]).T
                ms = jnp.mean(ot * ot, axis=-1, keepdims=True)
                o_ref[0, rows, :] = (ot * lax.rsqrt(ms + EPS) * subg_ref[...]
                                     * (1.0 - lam_init)).astype(bf16)
        return carry

    lax.fori_loop(0, nb // TILE_UNROLL, finish_tiles, 0)


def _step_tables(nb):
    per_trip = TRIP_STEPS
    diag = [(i, i, 0) for i in range(nb)]
    rest = [(i, j, BLK) for j in range(nb) for i in range(j + 1, nb)]
    n_steps = len(diag) + len(rest)
    assert n_steps % per_trip == 0
    n_trips = n_steps // per_trip
    n_masked = -(-len(diag) // n_trips)
    assert n_masked <= per_trip
    steps = []
    for _ in range(n_trips):
        for _ in range(n_masked):
            steps.append(diag.pop(0) if diag else rest.pop(0))
        for _ in range(per_trip - n_masked):
            steps.append(rest.pop(0))
    assert not diag and not rest
    return [jnp.asarray([s[c] for s in steps], jnp.int32) for c in range(3)], n_steps, n_masked


def _attention(q_all, k_all, vt_all, aux, score_bound, *, moba, lam_init):
    bsz, _, nb, _, _ = q_all.shape
    off = PAIRS if moba else 0
    assert nb % TILE_UNROLL == 0
    (itab, jtab, otab), n_steps, n_masked = _step_tables(nb)
    const = lambda b, h, it, jt, ot: (0, 0)
    in_specs = [
        pl.BlockSpec((1, 1, nb, BLK, LANES), lambda b, h, it, jt, ot: (b, h + off, 0, 0, 0)),
        pl.BlockSpec((1, 1, nb, BLK, LANES), lambda b, h, it, jt, ot: (b, h + off, 0, 0, 0)),
        pl.BlockSpec((1, 1, nb, LANES, BLK), lambda b, h, it, jt, ot: (b, h + off, 0, 0, 0)),
        pl.BlockSpec((1, 2 * BLK), const),
    ]
    scratch = [pltpu.VMEM((nb, 2 * BLK, LANES), bf16),
               pltpu.VMEM((nb, 1, 2 * BLK), f32), pltpu.VMEM((nb, 1, 2 * BLK), f32),
               pltpu.VMEM((nb, HEAD_DIM if moba else LANES, 2 * BLK), f32),
               pltpu.VMEM((1, TRIP_STEPS, BLK, 2 * BLK), f32)]
    if moba:
        (kmean,) = aux
        in_specs.append(pl.BlockSpec((1, nb, LANES), lambda b, h, it, jt, ot: (b, 0, h)))
        scratch.append(pltpu.VMEM((nb, nb, 2 * BLK), f32))
    else:
        lamvec, subg = aux
        in_specs.append(pl.BlockSpec(lamvec.shape, const))
        in_specs.append(pl.BlockSpec(subg.shape, const))
    mref = jnp.full((1, 2 * BLK), score_bound, f32)

    def run(online):
        return pl.pallas_call(
            functools.partial(_attn_kernel, moba=moba, online=online, lam_init=lam_init,
                              n_steps=n_steps, n_masked=n_masked),
            grid_spec=pltpu.PrefetchScalarGridSpec(
                num_scalar_prefetch=3,
                grid=(bsz, PAIRS),
                in_specs=in_specs,
                out_specs=pl.BlockSpec((1, nb * BLK, LANES), lambda b, h, it, jt, ot: (b, 0, h)),
                scratch_shapes=scratch),
            out_shape=jax.ShapeDtypeStruct((bsz, nb * BLK, PAIRS * LANES), bf16),
            compiler_params=pltpu.CompilerParams(
                dimension_semantics=("arbitrary", "arbitrary"), vmem_limit_bytes=VMEM_LIMIT),
            name=("moba_attn" if moba else "diff_attn") + ("_online" if online else ""),
        )(itab, jtab, otab, q_all, k_all, vt_all, mref, *aux)

    return lax.cond(score_bound <= MAX_FIXED_BOUND, lambda: run(False), lambda: run(True))


def _merge_kernel(oa_ref, ob_ref, sig_ref, x_ref, wa_ref, wb_ref, wo_ref, g2_ref, x1_ref, h2_ref):
    d = x_ref.shape[-1]
    pa = jnp.dot(oa_ref[0], wa_ref[...], preferred_element_type=f32)
    pb = jnp.dot(ob_ref[0], wb_ref[...], preferred_element_type=f32)
    merged = sig_ref[0, :, :d].astype(f32) * pa + sig_ref[0, :, d:].astype(f32) * pb
    x1 = x_ref[0] + jnp.dot(merged.astype(bf16), wo_ref[...], preferred_element_type=f32)
    x1_ref[0] = x1
    ms = jnp.mean(x1 * x1, axis=-1, keepdims=True)
    h2_ref[0] = (x1 * lax.rsqrt(ms + EPS) * g2_ref[...]).astype(bf16)


def _merge(oa, ob, sig, x, wa, wb, wo, g2, *, tm):
    bsz, seq, d = x.shape
    const = lambda b, i: (0, 0)
    tile = lambda b, i: (b, i, 0)
    return pl.pallas_call(
        _merge_kernel,
        grid=(bsz, seq // tm),
        in_specs=[
            pl.BlockSpec((1, tm, oa.shape[-1]), tile),
            pl.BlockSpec((1, tm, ob.shape[-1]), tile),
            pl.BlockSpec((1, tm, sig.shape[-1]), tile),
            pl.BlockSpec((1, tm, d), tile),
            pl.BlockSpec(wa.shape, const),
            pl.BlockSpec(wb.shape, const),
            pl.BlockSpec(wo.shape, const),
            pl.BlockSpec((1, d), const),
        ],
        out_specs=[pl.BlockSpec((1, tm, d), tile), pl.BlockSpec((1, tm, d), tile)],
        out_shape=[jax.ShapeDtypeStruct((bsz, seq, d), f32),
                   jax.ShapeDtypeStruct((bsz, seq, d), bf16)],
        compiler_params=pltpu.CompilerParams(
            dimension_semantics=("arbitrary", "arbitrary"), vmem_limit_bytes=VMEM_LIMIT),
        name="merge_outproj",
    )(oa, ob, sig, x, wa, wb, wo, g2)


HALO = 16


MXU_TILE = 256


def _mlp_kernel(h2_ref, halo_ref, x1_ref, wup_ref, cw_ref, cb_ref, wd_ref, o_ref,
                ext_scr, u_scr, *, tm, tiles_per_seq):
    i = pl.program_id(0)
    dff = wd_ref.shape[0]
    ext_scr[HALO:, :] = h2_ref[...]

    @pl.when(i % tiles_per_seq == 0)
    def _():
        ext_scr[:HALO, :] = jnp.zeros((HALO, ext_scr.shape[1]), bf16)

    @pl.when(i % tiles_per_seq != 0)
    def _():
        ext_scr[:HALO, :] = halo_ref[...]

    split = (dff // (2 * MXU_TILE) + 1) * MXU_TILE
    acc = x1_ref[...]
    for c, (lo, hi) in enumerate(((0, split), (split, dff))):
        wid = hi - lo
        u_scr[c, :, :wid] = jnp.dot(ext_scr[...], wup_ref[:, lo:hi], preferred_element_type=f32)
        g = jnp.dot(h2_ref[...], wup_ref[:, dff + lo:dff + hi], preferred_element_type=f32)
        y = cb_ref[:, lo:hi]
        for j in range(CONV_W):
            y = y + (u_scr[c, pl.ds(HALO - (CONV_W - 1) + j, tm), :wid] * cw_ref[j:j + 1, lo:hi])
        act = (jax.nn.gelu(y) * g).astype(bf16)
        acc = acc + jnp.dot(act, wd_ref[lo:hi, :], preferred_element_type=f32)
    o_ref[...] = acc


def _mlp(h2, x1, w_up_bf, conv_w, conv_b, w_down_bf, *, seq, tm):
    n, d = h2.shape
    dff = w_down_bf.shape[0]
    assert dff % MXU_TILE == 0
    split = (dff // (2 * MXU_TILE) + 1) * MXU_TILE
    const = lambda i: (0, 0)
    return pl.pallas_call(
        functools.partial(_mlp_kernel, tm=tm, tiles_per_seq=seq // tm),
        grid=(n // tm,),
        in_specs=[
            pl.BlockSpec((tm, d), lambda i: (i, 0)),
            pl.BlockSpec((HALO, d), lambda i: (jnp.maximum(i * (tm // HALO) - 1, 0), 0)),
            pl.BlockSpec((tm, d), lambda i: (i, 0)),
            pl.BlockSpec(w_up_bf.shape, const, pipeline_mode=pl.Buffered(1)),
            pl.BlockSpec(conv_w.shape, const),
            pl.BlockSpec(conv_b.shape, const),
            pl.BlockSpec(w_down_bf.shape, const, pipeline_mode=pl.Buffered(1)),
        ],
        out_specs=pl.BlockSpec((tm, d), lambda i: (i, 0)),
        out_shape=jax.ShapeDtypeStruct((n, d), f32),
        scratch_shapes=[pltpu.VMEM((tm + HALO, d), bf16),
                        pltpu.VMEM((2, tm + HALO, split), f32)],
        compiler_params=pltpu.CompilerParams(
            dimension_semantics=("arbitrary",), vmem_limit_bytes=VMEM_LIMIT),
        name="conv_mlp",
    )(h2, h2, x1, w_up_bf, conv_w, conv_b, w_down_bf)


def _rope_tables(seq):
    inv = 1.0 / (ROPE_THETA ** (jnp.arange(0, HEAD_DIM, 2, dtype=f32) / HEAD_DIM))
    ang = jnp.arange(seq, dtype=f32)[:, None] * inv[None, :]
    cos, sin = jnp.cos(ang), jnp.sin(ang)
    reps = LANES // HEAD_DIM
    return (jnp.tile(cos, (1, 2 * reps)), jnp.tile(jnp.concatenate([-sin, sin], axis=1), (1, reps)))


def kernel(x, norm1_g, w_in, b_gate, qn_a, kn_a, lam_q1, lam_k1, lam_q2, lam_k2, subln_g, qn_b, kn_b,
           w_a_proj, w_b_proj, w_out, norm2_g, w_up, conv_w, conv_b, w_down):
    bsz, seq, d = x.shape
    depth = w_in.shape[0]
    assert seq % 512 == 0 and d % LANES == 0
    cos_t, sin_t = _rope_tables(seq)
    seg = jnp.arange(LANES) // HEAD_DIM
    bd = (seg[:, None] == seg[None, :]).astype(bf16)
    reps = LANES // HEAD_DIM
    qscale = HEAD_DIM ** -0.5 * LOG2E
    for l in range(depth):
        lam_init = 0.8 - 0.6 * math.exp(-0.3 * l)
        gains = jnp.stack([jnp.tile(qn_a[l], reps) * qscale, jnp.tile(kn_a[l], reps),
                           jnp.tile(qn_b[l], reps) * qscale, jnp.tile(kn_b[l], reps)])
        q_all, k_all, vt_all, kmean, sig = _inproj(
            x, norm1_g[l][None], w_in[l].astype(bf16), b_gate[l][None], gains, cos_t, sin_t, bd, tm=512)
        kmean = kmean.reshape(bsz, seq // BLK, B_WIDTH)
        lamvec = jnp.stack([lam_q1[l], lam_k1[l], lam_q2[l], lam_k2[l]])
        gmax = jnp.max(jnp.abs(gains), axis=1) * (HEAD_DIM ** 0.5)
        oa = _attention(q_all, k_all, vt_all, (lamvec, subln_g[l][None]),
                        gmax[0] * gmax[1] * BOUND_SLACK, moba=False, lam_init=lam_init)
        ob = _attention(q_all, k_all, vt_all, (kmean,),
                        gmax[2] * gmax[3] * BOUND_SLACK, moba=True, lam_init=lam_init)
        x1, h2 = _merge(oa, ob, sig, x, w_a_proj[l].astype(bf16), w_b_proj[l].astype(bf16),
                        w_out[l].astype(bf16), norm2_g[l][None], tm=512)
        dff = w_down.shape[1]
        x = _mlp(h2.reshape(bsz * seq, d), x1.reshape(bsz * seq, d), w_up[l].astype(bf16), conv_w[l],
                 conv_b[l][None], w_down[l].astype(bf16), seq=seq, tm=512).reshape(bsz, seq, d)
    return x
```

```python
import functools
import math

import jax
import jax.numpy as jnp
from jax import lax
from jax.experimental import pallas as pl
from jax.experimental.pallas import tpu as pltpu

HEAD_DIM = 64
A_HEADS = 4
A_VDIM = 2 * HEAD_DIM
A_WIDTH = A_HEADS * A_VDIM
B_HEADS = 8
B_WIDTH = B_HEADS * HEAD_DIM
B_BLOCK = 256
B_TOPK = 3
CONV_W = 3
ROPE_THETA = 10000.0
EPS = 1e-6

LANES = 128
BLK = B_BLOCK
PAIRS = A_WIDTH // LANES
LOG2E = 1.4426950408889634
NEG_BIG = -1e30
POS_BIG = 1e30
VMEM_LIMIT = 56 * 1024 * 1024

f32 = jnp.float32
bf16 = jnp.bfloat16


def _nt_dot(a, b, **kw):
    return lax.dot_general(a, b, (((1,), (1,)), ((), ())), preferred_element_type=f32, **kw)


def _inproj_kernel(x_ref, g1_ref, w_ref, bg_ref, gain_ref, cos_ref, sin_ref, bd_ref,
                   q_ref, k_ref, vt_ref, kmean_ref, sig_ref, h_scr, *, tm):
    nblk = tm // BLK
    x = x_ref[0]
    ms = jnp.mean(x * x, axis=-1, keepdims=True)
    h_scr[...] = (x * lax.rsqrt(ms + EPS) * g1_ref[...]).astype(bf16)
    cos = cos_ref[...]
    sin = sin_ref[...]
    bd = bd_ref[...]
    lane = lax.broadcasted_iota(jnp.int32, (tm, LANES), 1)
    first_half = (lane & (HEAD_DIM - 1)) < (HEAD_DIM // 2)

    def col_tile(j):
        return jnp.dot(h_scr[...], w_ref[:, j * A_WIDTH:(j + 1) * A_WIDTH],
                       preferred_element_type=f32)

    def norm_rope(y, gain):
        y2 = y * y
        hi = y2.astype(bf16)
        lo = (y2 - hi.astype(f32)).astype(bf16)
        ss = (jnp.dot(hi, bd, preferred_element_type=f32)
              + jnp.dot(lo, bd, preferred_element_type=f32))
        yn = y * lax.rsqrt(ss * (1.0 / HEAD_DIM) + EPS) * gain
        rot = jnp.where(first_half, pltpu.roll(yn, LANES - HEAD_DIM // 2, 1),
                        pltpu.roll(yn, HEAD_DIM // 2, 1))
        return yn * cos + rot * sin

    for branch in range(2):
        acc_q = col_tile(3 * branch)
        for c in range(PAIRS):
            out = norm_rope(acc_q[:, c * LANES:(c + 1) * LANES], gain_ref[2 * branch:2 * branch + 1, :])
            for bi in range(nblk):
                q_ref[0, branch * PAIRS + c, bi] = out[bi * BLK:(bi + 1) * BLK].astype(bf16)
        acc_k = col_tile(3 * branch + 1)
        for c in range(PAIRS):
            out = norm_rope(acc_k[:, c * LANES:(c + 1) * LANES],
                            gain_ref[2 * branch + 1:2 * branch + 2, :])
            for bi in range(nblk):
                blk = out[bi * BLK:(bi + 1) * BLK]
                k_ref[0, branch * PAIRS + c, bi] = blk.astype(bf16)
                if branch == 1:
                    kmean_ref[0, bi:bi + 1, c * LANES:(c + 1) * LANES] = jnp.mean(
                        blk, axis=0, keepdims=True)
        acc_v = col_tile(3 * branch + 2)
        for c in range(PAIRS):
            for bi in range(nblk):
                blk = acc_v[bi * BLK:(bi + 1) * BLK, c * LANES:(c + 1) * LANES]
                vt_ref[0, branch * PAIRS + c, bi] = blk.T.astype(bf16)

    for jj in range(4):
        acc = col_tile(6 + jj) + bg_ref[:, jj * A_WIDTH:(jj + 1) * A_WIDTH]
        sig_ref[0, :, jj * A_WIDTH:(jj + 1) * A_WIDTH] = jax.nn.sigmoid(acc).astype(bf16)


def _inproj(x, g1, w_in_bf, b_gate, gains, cos_t, sin_t, bd, *, tm):
    bsz, seq, d = x.shape
    n_cols = w_in_bf.shape[1]
    nblk = tm // BLK
    spt = seq // tm
    nb = seq // BLK
    qk_shape = jax.ShapeDtypeStruct((bsz, 2 * PAIRS, nb, BLK, LANES), bf16)
    vt_shape = jax.ShapeDtypeStruct((bsz, 2 * PAIRS, nb, LANES, BLK), bf16)
    qk_spec = pl.BlockSpec((1, 2 * PAIRS, nblk, BLK, LANES), lambda b, i: (b, 0, i, 0, 0))
    vt_spec = pl.BlockSpec((1, 2 * PAIRS, nblk, LANES, BLK), lambda b, i: (b, 0, i, 0, 0))
    return pl.pallas_call(
        functools.partial(_inproj_kernel, tm=tm),
        grid=(bsz, spt),
        in_specs=[
            pl.BlockSpec((1, tm, d), lambda b, i: (b, i, 0)),
            pl.BlockSpec((1, d), lambda b, i: (0, 0)),
            pl.BlockSpec((d, n_cols), lambda b, i: (0, 0)),
            pl.BlockSpec((1, b_gate.shape[1]), lambda b, i: (0, 0)),
            pl.BlockSpec((4, LANES), lambda b, i: (0, 0)),
            pl.BlockSpec((tm, LANES), lambda b, i: (i, 0)),
            pl.BlockSpec((tm, LANES), lambda b, i: (i, 0)),
            pl.BlockSpec((LANES, LANES), lambda b, i: (0, 0)),
        ],
        out_specs=[
            qk_spec, qk_spec, vt_spec,
            pl.BlockSpec((1, nblk, B_WIDTH), lambda b, i: (b * spt + i, 0, 0)),
            pl.BlockSpec((1, tm, b_gate.shape[1]), lambda b, i: (b, i, 0)),
        ],
        out_shape=[
            qk_shape, qk_shape, vt_shape,
            jax.ShapeDtypeStruct((bsz * spt, nblk, B_WIDTH), f32),
            jax.ShapeDtypeStruct((bsz, seq, b_gate.shape[1]), bf16),
        ],
        scratch_shapes=[pltpu.VMEM((tm, d), bf16)],
        compiler_params=pltpu.CompilerParams(
            dimension_semantics=("arbitrary", "arbitrary"), vmem_limit_bytes=VMEM_LIMIT),
        name="inproj",
    )(x, g1, w_in_bf, b_gate, gains, cos_t, sin_t, bd)


TRIP_STEPS = 34
TILE_UNROLL = 4
BOUND_SLACK = 1.0 + 2.0 ** -6
MAX_FIXED_BOUND = 60.0


def _attn_kernel(itab_ref, jtab_ref, otab_ref, *refs, moba, online, lam_init, n_steps, n_masked):
    if moba:
        (q_ref, k_ref, vt_ref, mref_ref, kmean_ref, o_ref,
         qs_scr, m_scr, l_scr, acc_scr, s_scr, sel_scr) = refs
    else:
        (q_ref, k_ref, vt_ref, mref_ref, lam_ref, subg_ref, o_ref,
         qs_scr, m_scr, l_scr, acc_scr, s_scr) = refs
    nb = k_ref.shape[2]
    acc_rows = acc_scr.shape[1]
    lane = lax.broadcasted_iota(jnp.int32, (BLK, LANES), 1)
    key_minus_row = (lax.broadcasted_iota(jnp.int32, (BLK, 2 * BLK), 0)
                     - (lax.broadcasted_iota(jnp.int32, (BLK, 2 * BLK), 1) & (BLK - 1)))

    if moba:
        km = kmean_ref[0]
        km_parts = []
        for _ in range(3):
            part = km.astype(bf16)
            km_parts.append(part)
            km = km - part.astype(f32)

    nidx = lax.broadcasted_iota(jnp.int32, (nb, 2 * BLK), 0)
    for i in range(nb):
        q = q_ref[0, 0, i].astype(f32)
        qs = jnp.concatenate([jnp.where(lane < HEAD_DIM, q, 0.0),
                              jnp.where(lane >= HEAD_DIM, q, 0.0)], axis=0).astype(bf16)
        qs_scr[i] = qs
        if online:
            m_scr[i] = jnp.full((1, 2 * BLK), NEG_BIG, f32)
        l_scr[i] = jnp.zeros((1, 2 * BLK), f32)
        acc_scr[i] = jnp.zeros((acc_rows, 2 * BLK), f32)
        if moba:
            keep = nidx <= i
            if i > B_TOPK:
                g = _nt_dot(km_parts[0], qs) + _nt_dot(km_parts[1], qs) + _nt_dot(km_parts[2], qs)
                cnt = jnp.zeros((nb, 2 * BLK), f32)
                for mm in range(i):
                    row = g[mm:mm + 1, :]
                    beats = (row > g) | ((row == g) & (nidx > mm))
                    cnt = cnt + jnp.where(beats, 1.0, 0.0)
                keep = ((cnt < float(B_TOPK)) & (nidx < i)) | (nidx == i)
            sel_scr[i] = jnp.where(keep, 1.0, 0.0)

    def qk(step, slot, u):
        s = _nt_dot(k_ref[0, 0, jtab_ref[step]], qs_scr[itab_ref[step]])
        if slot == 0 and u < n_masked:
            s = jnp.where(key_minus_row <= otab_ref[step], s, NEG_BIG)
        s_scr[slot, u] = s

    def update(step, slot, u):
        i = itab_ref[step]
        j = jtab_ref[step]
        s = s_scr[slot, u]
        if moba:
            chosen = sel_scr[i, pl.ds(j, 1), :] > 0.5
        if online:
            m_old = m_scr[i]
            m_new = jnp.maximum(m_old, jnp.max(s, axis=0, keepdims=True))
            if moba:
                m_new = jnp.where(chosen, m_new, m_old)
            m_sub = m_new
            alpha = jnp.exp2(m_old - m_new)
            m_scr[i] = m_new
        else:
            m_sub = mref_ref[...]
        if moba:
            m_sub = jnp.where(chosen, m_sub, POS_BIG)
        p = jnp.exp2(s - m_sub)
        psum = jnp.sum(p, axis=0, keepdims=True)
        pb = p.astype(bf16)
        vt = vt_ref[0, 0, j]
        if moba:
            pv = jnp.concatenate(
                [jnp.dot(vt[:HEAD_DIM], pb[:, :BLK], preferred_element_type=f32),
                 jnp.dot(vt[HEAD_DIM:], pb[:, BLK:], preferred_element_type=f32)], axis=1)
        else:
            pv = jnp.dot(vt, pb, preferred_element_type=f32)
        if online:
            l_scr[i] = alpha * l_scr[i] + psum
            acc_scr[i] = alpha * acc_scr[i] + pv
        else:
            l_scr[i] = l_scr[i] + psum
            acc_scr[i] = acc_scr[i] + pv

    def trip(it, carry):
        base = it * TRIP_STEPS
        for u in range(TRIP_STEPS):
            qk(base + u, 0, u)
        for u in range(TRIP_STEPS):
            update(base + u, 0, u)
        return carry

    lax.fori_loop(0, n_steps // TRIP_STEPS, trip, 0)

    if not moba:
        lv = lam_ref[...]
        lam = (jnp.exp(jnp.sum(lv[0:1] * lv[1:2], axis=-1, keepdims=True))
               - jnp.exp(jnp.sum(lv[2:3] * lv[3:4], axis=-1, keepdims=True)) + lam_init)

    def finish_tiles(t, carry):
        for r in range(TILE_UNROLL):
            i = t * TILE_UNROLL + r
            o = acc_scr[i] * (1.0 / l_scr[i])
            rows = pl.ds(pl.multiple_of(i * BLK, BLK), BLK)
            if moba:
                ot = jnp.concatenate([o[:, :BLK], o[:, BLK:]], axis=0).T
                o_ref[0, rows, :] = ot.astype(bf16)
            else:
                od = o[:, :BLK] - lam * o[:, BLK:]
                ms = jnp.mean(od * od, axis=0, keepdims=True)
                od = od * lax.rsqrt(ms + EPS) * subg_ref[...] * (1.0 - lam_init)
                o_ref[0, rows, :] = od.T.astype(bf16)
        return carry

    lax.fori_loop(0, nb // TILE_UNROLL, finish_tiles, 0)


def _step_tables(nb):
    per_trip = TRIP_STEPS
    diag = [(i, i, 0) for i in range(nb)]
    rest = [(i, j, BLK) for j in range(nb) for i in range(j + 1, nb)]
    n_steps = len(diag) + len(rest)
    assert n_steps % per_trip == 0
    n_trips = n_steps // per_trip
    n_masked = -(-len(diag) // n_trips)
    assert n_masked <= per_trip
    steps = []
    for _ in range(n_trips):
        for _ in range(n_masked):
            steps.append(diag.pop(0) if diag else rest.pop(0))
        for _ in range(per_trip - n_masked):
            steps.append(rest.pop(0))
    assert not diag and not rest
    return [jnp.asarray([s[c] for s in steps], jnp.int32) for c in range(3)], n_steps, n_masked


def _attention(q_all, k_all, vt_all, aux, score_bound, *, moba, lam_init):
    bsz, _, nb, _, _ = q_all.shape
    off = PAIRS if moba else 0
    assert nb % TILE_UNROLL == 0
    (itab, jtab, otab), n_steps, n_masked = _step_tables(nb)
    const = lambda b, h, it, jt, ot: (0, 0)
    in_specs = [
        pl.BlockSpec((1, 1, nb, BLK, LANES), lambda b, h, it, jt, ot: (b, h + off, 0, 0, 0)),
        pl.BlockSpec((1, 1, nb, BLK, LANES), lambda b, h, it, jt, ot: (b, h + off, 0, 0, 0)),
        pl.BlockSpec((1, 1, nb, LANES, BLK), lambda b, h, it, jt, ot: (b, h + off, 0, 0, 0)),
        pl.BlockSpec((1, 2 * BLK), const),
    ]
    scratch = [pltpu.VMEM((nb, 2 * BLK, LANES), bf16),
               pltpu.VMEM((nb, 1, 2 * BLK), f32), pltpu.VMEM((nb, 1, 2 * BLK), f32),
               pltpu.VMEM((nb, HEAD_DIM if moba else LANES, 2 * BLK), f32),
               pltpu.VMEM((1, TRIP_STEPS, BLK, 2 * BLK), f32)]
    if moba:
        (kmean,) = aux
        in_specs.append(pl.BlockSpec((1, nb, LANES), lambda b, h, it, jt, ot: (b, 0, h)))
        scratch.append(pltpu.VMEM((nb, nb, 2 * BLK), f32))
    else:
        lamvec, subg = aux
        in_specs.append(pl.BlockSpec(lamvec.shape, const))
        in_specs.append(pl.BlockSpec(subg.shape, const))
    mref = jnp.full((1, 2 * BLK), score_bound, f32)

    def run(online):
        return pl.pallas_call(
            functools.partial(_attn_kernel, moba=moba, online=online, lam_init=lam_init,
                              n_steps=n_steps, n_masked=n_masked),
            grid_spec=pltpu.PrefetchScalarGridSpec(
                num_scalar_prefetch=3,
                grid=(bsz, PAIRS),
                in_specs=in_specs,
                out_specs=pl.BlockSpec((1, nb * BLK, LANES), lambda b, h, it, jt, ot: (b, 0, h)),
                scratch_shapes=scratch),
            out_shape=jax.ShapeDtypeStruct((bsz, nb * BLK, PAIRS * LANES), bf16),
            compiler_params=pltpu.CompilerParams(
                dimension_semantics=("arbitrary", "arbitrary"), vmem_limit_bytes=VMEM_LIMIT),
            name=("moba_attn" if moba else "diff_attn") + ("_online" if online else ""),
        )(itab, jtab, otab, q_all, k_all, vt_all, mref, *aux)

    return lax.cond(score_bound <= MAX_FIXED_BOUND, lambda: run(False), lambda: run(True))


def _merge_kernel(oa_ref, ob_ref, sig_ref, x_ref, wa_ref, wb_ref, wo_ref, g2_ref, x1_ref, h2_ref):
    d = x_ref.shape[-1]
    pa = jnp.dot(oa_ref[0], wa_ref[...], preferred_element_type=f32)
    pb = jnp.dot(ob_ref[0], wb_ref[...], preferred_element_type=f32)
    merged = sig_ref[0, :, :d].astype(f32) * pa + sig_ref[0, :, d:].astype(f32) * pb
    x1 = x_ref[0] + jnp.dot(merged.astype(bf16), wo_ref[...], preferred_element_type=f32)
    x1_ref[0] = x1
    ms = jnp.mean(x1 * x1, axis=-1, keepdims=True)
    h2_ref[0] = (x1 * lax.rsqrt(ms + EPS) * g2_ref[...]).astype(bf16)


def _merge(oa, ob, sig, x, wa, wb, wo, g2, *, tm):
    bsz, seq, d = x.shape
    const = lambda b, i: (0, 0)
    tile = lambda b, i: (b, i, 0)
    return pl.pallas_call(
        _merge_kernel,
        grid=(bsz, seq // tm),
        in_specs=[
            pl.BlockSpec((1, tm, oa.shape[-1]), tile),
            pl.BlockSpec((1, tm, ob.shape[-1]), tile),
            pl.BlockSpec((1, tm, sig.shape[-1]), tile),
            pl.BlockSpec((1, tm, d), tile),
            pl.BlockSpec(wa.shape, const),
            pl.BlockSpec(wb.shape, const),
            pl.BlockSpec(wo.shape, const),
            pl.BlockSpec((1, d), const),
        ],
        out_specs=[pl.BlockSpec((1, tm, d), tile), pl.BlockSpec((1, tm, d), tile)],
        out_shape=[jax.ShapeDtypeStruct((bsz, seq, d), f32),
                   jax.ShapeDtypeStruct((bsz, seq, d), bf16)],
        compiler_params=pltpu.CompilerParams(
            dimension_semantics=("arbitrary", "arbitrary"), vmem_limit_bytes=VMEM_LIMIT),
        name="merge_outproj",
    )(oa, ob, sig, x, wa, wb, wo, g2)


HALO = 16


MXU_TILE = 256


def _mlp_kernel(h2_ref, halo_ref, x1_ref, wup_ref, cw_ref, cb_ref, wd_ref, o_ref,
                ext_scr, u_scr, *, tm, tiles_per_seq):
    i = pl.program_id(0)
    dff = wd_ref.shape[0]
    ext_scr[HALO:, :] = h2_ref[...]

    @pl.when(i % tiles_per_seq == 0)
    def _():
        ext_scr[:HALO, :] = jnp.zeros((HALO, ext_scr.shape[1]), bf16)

    @pl.when(i % tiles_per_seq != 0)
    def _():
        ext_scr[:HALO, :] = halo_ref[...]

    split = (dff // (2 * MXU_TILE) + 1) * MXU_TILE
    acc = x1_ref[...]
    for c, (lo, hi) in enumerate(((0, split), (split, dff))):
        wid = hi - lo
        u_scr[c, :, :wid] = jnp.dot(ext_scr[...], wup_ref[:, lo:hi], preferred_element_type=f32)
        g = jnp.dot(h2_ref[...], wup_ref[:, dff + lo:dff + hi], preferred_element_type=f32)
        y = cb_ref[:, lo:hi]
        for j in range(CONV_W):
            y = y + (u_scr[c, pl.ds(HALO - (CONV_W - 1) + j, tm), :wid] * cw_ref[j:j + 1, lo:hi])
        act = (jax.nn.gelu(y) * g).astype(bf16)
        acc = acc + jnp.dot(act, wd_ref[lo:hi, :], preferred_element_type=f32)
    o_ref[...] = acc


def _mlp(h2, x1, w_up_bf, conv_w, conv_b, w_down_bf, *, seq, tm):
    n, d = h2.shape
    dff = w_down_bf.shape[0]
    assert dff % MXU_TILE == 0
    split = (dff // (2 * MXU_TILE) + 1) * MXU_TILE
    const = lambda i: (0, 0)
    return pl.pallas_call(
        functools.partial(_mlp_kernel, tm=tm, tiles_per_seq=seq // tm),
        grid=(n // tm,),
        in_specs=[
            pl.BlockSpec((tm, d), lambda i: (i, 0)),
            pl.BlockSpec((HALO, d), lambda i: (jnp.maximum(i * (tm // HALO) - 1, 0), 0)),
            pl.BlockSpec((tm, d), lambda i: (i, 0)),
            pl.BlockSpec(w_up_bf.shape, const, pipeline_mode=pl.Buffered(1)),
            pl.BlockSpec(conv_w.shape, const),
            pl.BlockSpec(conv_b.shape, const),
            pl.BlockSpec(w_down_bf.shape, const, pipeline_mode=pl.Buffered(1)),
        ],
        out_specs=pl.BlockSpec((tm, d), lambda i: (i, 0)),
        out_shape=jax.ShapeDtypeStruct((n, d), f32),
        scratch_shapes=[pltpu.VMEM((tm + HALO, d), bf16),
                        pltpu.VMEM((2, tm + HALO, split), f32)],
        compiler_params=pltpu.CompilerParams(
            dimension_semantics=("arbitrary",), vmem_limit_bytes=VMEM_LIMIT),
        name="conv_mlp",
    )(h2, h2, x1, w_up_bf, conv_w, conv_b, w_down_bf)


def _rope_tables(seq):
    inv = 1.0 / (ROPE_THETA ** (jnp.arange(0, HEAD_DIM, 2, dtype=f32) / HEAD_DIM))
    ang = jnp.arange(seq, dtype=f32)[:, None] * inv[None, :]
    cos, sin = jnp.cos(ang), jnp.sin(ang)
    reps = LANES // HEAD_DIM
    return (jnp.tile(cos, (1, 2 * reps)), jnp.tile(jnp.concatenate([-sin, sin], axis=1), (1, reps)))


def kernel(x, norm1_g, w_in, b_gate, qn_a, kn_a, lam_q1, lam_k1, lam_q2, lam_k2, subln_g, qn_b, kn_b,
           w_a_proj, w_b_proj, w_out, norm2_g, w_up, conv_w, conv_b, w_down):
    bsz, seq, d = x.shape
    depth = w_in.shape[0]
    assert seq % 512 == 0 and d % LANES == 0
    cos_t, sin_t = _rope_tables(seq)
    seg = jnp.arange(LANES) // HEAD_DIM
    bd = (seg[:, None] == seg[None, :]).astype(bf16)
    reps = LANES // HEAD_DIM
    qscale = HEAD_DIM ** -0.5 * LOG2E
    for l in range(depth):
        lam_init = 0.8 - 0.6 * math.exp(-0.3 * l)
        gains = jnp.stack([jnp.tile(qn_a[l], reps) * qscale, jnp.tile(kn_a[l], reps),
                           jnp.tile(qn_b[l], reps) * qscale, jnp.tile(kn_b[l], reps)])
        q_all, k_all, vt_all, kmean, sig = _inproj(
            x, norm1_g[l][None], w_in[l].astype(bf16), b_gate[l][None], gains, cos_t, sin_t, bd, tm=512)
        kmean = kmean.reshape(bsz, seq // BLK, B_WIDTH)
        lamvec = jnp.stack([lam_q1[l], lam_k1[l], lam_q2[l], lam_k2[l]])
        gmax = jnp.max(jnp.abs(gains), axis=1) * (HEAD_DIM ** 0.5)
        subg_cols = jnp.tile(subln_g[l][:, None], (1, BLK))
        oa = _attention(q_all, k_all, vt_all, (lamvec, subg_cols),
                        gmax[0] * gmax[1] * BOUND_SLACK, moba=False, lam_init=lam_init)
        ob = _attention(q_all, k_all, vt_all, (kmean,),
                        gmax[2] * gmax[3] * BOUND_SLACK, moba=True, lam_init=lam_init)
        x1, h2 = _merge(oa, ob, sig, x, w_a_proj[l].astype(bf16), w_b_proj[l].astype(bf16),
                        w_out[l].astype(bf16), norm2_g[l][None], tm=512)
        dff = w_down.shape[1]
        x = _mlp(h2.reshape(bsz * seq, d), x1.reshape(bsz * seq, d), w_up[l].astype(bf16), conv_w[l],
                 conv_b[l][None], w_down[l].astype(bf16), seq=seq, tm=512).reshape(bsz, seq, d)
    return x
```

```python
import functools
import math

import jax
import jax.numpy as jnp
from jax import lax
from jax.experimental import pallas as pl
from jax.experimental.pallas import tpu as pltpu

HEAD_DIM = 64
A_HEADS = 4
A_VDIM = 2 * HEAD_DIM
A_WIDTH = A_HEADS * A_VDIM
B_HEADS = 8
B_WIDTH = B_HEADS * HEAD_DIM
B_BLOCK = 256
B_TOPK = 3
CONV_W = 3
ROPE_THETA = 10000.0
EPS = 1e-6

LANES = 128
BLK = B_BLOCK
PAIRS = A_WIDTH // LANES
LOG2E = 1.4426950408889634
NEG_BIG = -1e30
POS_BIG = 1e30
VMEM_LIMIT = 56 * 1024 * 1024

f32 = jnp.float32
bf16 = jnp.bfloat16


def _nt_dot(a, b, **kw):
    return lax.dot_general(a, b, (((1,), (1,)), ((), ())), preferred_element_type=f32, **kw)


def _inproj_kernel(x_ref, g1_ref, w_ref, gain_ref, cos_ref, sin_ref, bd_ref,
                   q_ref, k_ref, vt_ref, kmean_ref, h_scr, *, tm):
    nblk = tm // BLK
    x = x_ref[0]
    ms = jnp.mean(x * x, axis=-1, keepdims=True)
    h_scr[...] = (x * lax.rsqrt(ms + EPS) * g1_ref[...]).astype(bf16)
    cos = cos_ref[...]
    sin = sin_ref[...]
    bd = bd_ref[...]
    lane = lax.broadcasted_iota(jnp.int32, (tm, LANES), 1)
    first_half = (lane & (HEAD_DIM - 1)) < (HEAD_DIM // 2)

    def col_tile(j):
        return jnp.dot(h_scr[...], w_ref[:, j * A_WIDTH:(j + 1) * A_WIDTH],
                       preferred_element_type=f32)

    def norm_rope(y, gain):
        ss = jnp.dot((y * y).astype(bf16), bd, preferred_element_type=f32)
        yn = y * lax.rsqrt(ss * (1.0 / HEAD_DIM) + EPS) * gain
        rot = jnp.where(first_half, pltpu.roll(yn, LANES - HEAD_DIM // 2, 1),
                        pltpu.roll(yn, HEAD_DIM // 2, 1))
        return yn * cos + rot * sin

    for branch in range(2):
        acc_q = col_tile(3 * branch)
        for c in range(PAIRS):
            out = norm_rope(acc_q[:, c * LANES:(c + 1) * LANES], gain_ref[2 * branch:2 * branch + 1, :])
            for bi in range(nblk):
                q_ref[0, branch * PAIRS + c, bi] = out[bi * BLK:(bi + 1) * BLK].astype(bf16)
        acc_k = col_tile(3 * branch + 1)
        for c in range(PAIRS):
            out = norm_rope(acc_k[:, c * LANES:(c + 1) * LANES],
                            gain_ref[2 * branch + 1:2 * branch + 2, :])
            for bi in range(nblk):
                blk = out[bi * BLK:(bi + 1) * BLK]
                k_ref[0, branch * PAIRS + c, bi] = blk.astype(bf16)
                if branch == 1:
                    kmean_ref[0, bi:bi + 1, c * LANES:(c + 1) * LANES] = jnp.mean(
                        blk, axis=0, keepdims=True)
        acc_v = col_tile(3 * branch + 2)
        for c in range(PAIRS):
            for bi in range(nblk):
                blk = acc_v[bi * BLK:(bi + 1) * BLK, c * LANES:(c + 1) * LANES]
                vt_ref[0, branch * PAIRS + c, bi] = blk.T.astype(bf16)


def _inproj(x, g1, w_in_bf, gains, cos_t, sin_t, bd, *, tm):
    bsz, seq, d = x.shape
    n_cols = w_in_bf.shape[1]
    nblk = tm // BLK
    spt = seq // tm
    nb = seq // BLK
    qk_shape = jax.ShapeDtypeStruct((bsz, 2 * PAIRS, nb, BLK, LANES), bf16)
    vt_shape = jax.ShapeDtypeStruct((bsz, 2 * PAIRS, nb, LANES, BLK), bf16)
    qk_spec = pl.BlockSpec((1, 2 * PAIRS, nblk, BLK, LANES), lambda b, i: (b, 0, i, 0, 0))
    vt_spec = pl.BlockSpec((1, 2 * PAIRS, nblk, LANES, BLK), lambda b, i: (b, 0, i, 0, 0))
    return pl.pallas_call(
        functools.partial(_inproj_kernel, tm=tm),
        grid=(bsz, spt),
        in_specs=[
            pl.BlockSpec((1, tm, d), lambda b, i: (b, i, 0)),
            pl.BlockSpec((1, d), lambda b, i: (0, 0)),
            pl.BlockSpec((d, n_cols), lambda b, i: (0, 0), pipeline_mode=pl.Buffered(1)),
            pl.BlockSpec((4, LANES), lambda b, i: (0, 0)),
            pl.BlockSpec((tm, LANES), lambda b, i: (i, 0)),
            pl.BlockSpec((tm, LANES), lambda b, i: (i, 0)),
            pl.BlockSpec((LANES, LANES), lambda b, i: (0, 0)),
        ],
        out_specs=[
            qk_spec, qk_spec, vt_spec,
            pl.BlockSpec((1, nblk, B_WIDTH), lambda b, i: (b * spt + i, 0, 0)),
        ],
        out_shape=[
            qk_shape, qk_shape, vt_shape,
            jax.ShapeDtypeStruct((bsz * spt, nblk, B_WIDTH), f32),
        ],
        scratch_shapes=[pltpu.VMEM((tm, d), bf16)],
        compiler_params=pltpu.CompilerParams(
            dimension_semantics=("arbitrary", "arbitrary"), vmem_limit_bytes=VMEM_LIMIT),
        name="inproj",
    )(x, g1, w_in_bf, gains, cos_t, sin_t, bd)


TRIP_STEPS = 34
TILE_UNROLL = 4
BOUND_SLACK = 1.0 + 2.0 ** -6
MAX_FIXED_BOUND = 60.0


def _attn_kernel(itab_ref, jtab_ref, otab_ref, *refs, moba, online, lam_init, n_steps, n_masked):
    if moba:
        (q_ref, k_ref, vt_ref, mref_ref, kmean_ref, o_ref,
         qs_scr, m_scr, l_scr, acc_scr, s_scr, sel_scr) = refs
    else:
        (q_ref, k_ref, vt_ref, mref_ref, lam_ref, subg_ref, o_ref,
         qs_scr, m_scr, l_scr, acc_scr, s_scr) = refs
    nb = k_ref.shape[2]
    acc_rows = acc_scr.shape[1]
    lane = lax.broadcasted_iota(jnp.int32, (BLK, LANES), 1)
    key_minus_row = (lax.broadcasted_iota(jnp.int32, (BLK, 2 * BLK), 0)
                     - (lax.broadcasted_iota(jnp.int32, (BLK, 2 * BLK), 1) & (BLK - 1)))

    if moba:
        km = kmean_ref[0]
        km_parts = []
        for _ in range(3):
            part = km.astype(bf16)
            km_parts.append(part)
            km = km - part.astype(f32)

    nidx = lax.broadcasted_iota(jnp.int32, (nb, 2 * BLK), 0)
    for i in range(nb):
        q = q_ref[0, 0, i].astype(f32)
        qs = jnp.concatenate([jnp.where(lane < HEAD_DIM, q, 0.0),
                              jnp.where(lane >= HEAD_DIM, q, 0.0)], axis=0).astype(bf16)
        qs_scr[i] = qs
        if online:
            m_scr[i] = jnp.full((1, 2 * BLK), NEG_BIG, f32)
        l_scr[i] = jnp.zeros((1, 2 * BLK), f32)
        acc_scr[i] = jnp.zeros((acc_rows, 2 * BLK), f32)
        if moba:
            keep = nidx <= i
            if i > B_TOPK:
                g = _nt_dot(km_parts[0], qs) + _nt_dot(km_parts[1], qs) + _nt_dot(km_parts[2], qs)
                cnt = jnp.zeros((nb, 2 * BLK), f32)
                for mm in range(i):
                    row = g[mm:mm + 1, :]
                    beats = (row > g) | ((row == g) & (nidx > mm))
                    cnt = cnt + jnp.where(beats, 1.0, 0.0)
                keep = ((cnt < float(B_TOPK)) & (nidx < i)) | (nidx == i)
            sel_scr[i] = jnp.where(keep, 1.0, 0.0)

    def qk(step, slot, u):
        s = _nt_dot(k_ref[0, 0, jtab_ref[step]], qs_scr[itab_ref[step]])
        if slot == 0 and u < n_masked:
            s = jnp.where(key_minus_row <= otab_ref[step], s, NEG_BIG)
        s_scr[slot, u] = s

    def update(step, slot, u):
        i = itab_ref[step]
        j = jtab_ref[step]
        s = s_scr[slot, u]
        if moba:
            chosen = sel_scr[i, pl.ds(j, 1), :] > 0.5
        if online:
            m_old = m_scr[i]
            m_new = jnp.maximum(m_old, jnp.max(s, axis=0, keepdims=True))
            if moba:
                m_new = jnp.where(chosen, m_new, m_old)
            m_sub = m_new
            alpha = jnp.exp2(m_old - m_new)
            m_scr[i] = m_new
        else:
            m_sub = mref_ref[...]
        if moba:
            m_sub = jnp.where(chosen, m_sub, POS_BIG)
        p = jnp.exp2(s - m_sub)
        psum = jnp.sum(p, axis=0, keepdims=True)
        pb = p.astype(bf16)
        vt = vt_ref[0, 0, j]
        if moba:
            pv = jnp.concatenate(
                [jnp.dot(vt[:HEAD_DIM], pb[:, :BLK], preferred_element_type=f32),
                 jnp.dot(vt[HEAD_DIM:], pb[:, BLK:], preferred_element_type=f32)], axis=1)
        else:
            pv = jnp.dot(vt, pb, preferred_element_type=f32)
        if online:
            l_scr[i] = alpha * l_scr[i] + psum
            acc_scr[i] = alpha * acc_scr[i] + pv
        else:
            l_scr[i] = l_scr[i] + psum
            acc_scr[i] = acc_scr[i] + pv

    def trip(it, carry):
        base = it * TRIP_STEPS
        for u in range(TRIP_STEPS):
            qk(base + u, 0, u)
        for u in range(TRIP_STEPS):
            update(base + u, 0, u)
        return carry

    lax.fori_loop(0, n_steps // TRIP_STEPS, trip, 0)

    if not moba:
        lv = lam_ref[...]
        lam = (jnp.exp(jnp.sum(lv[0:1] * lv[1:2], axis=-1, keepdims=True))
               - jnp.exp(jnp.sum(lv[2:3] * lv[3:4], axis=-1, keepdims=True)) + lam_init)

    def finish_tiles(t, carry):
        for r in range(TILE_UNROLL):
            i = t * TILE_UNROLL + r
            o = acc_scr[i] * (1.0 / l_scr[i])
            rows = pl.ds(pl.multiple_of(i * BLK, BLK), BLK)
            if moba:
                ot = jnp.concatenate([o[:, :BLK], o[:, BLK:]], axis=0).T
                o_ref[0, rows, :] = ot.astype(bf16)
            else:
                od = o[:, :BLK] - lam * o[:, BLK:]
                ms = jnp.mean(od * od, axis=0, keepdims=True)
                od = od * lax.rsqrt(ms + EPS) * subg_ref[...] * (1.0 - lam_init)
                o_ref[0, rows, :] = od.T.astype(bf16)
        return carry

    lax.fori_loop(0, nb // TILE_UNROLL, finish_tiles, 0)


def _step_tables(nb):
    per_trip = TRIP_STEPS
    diag = [(i, i, 0) for i in range(nb)]
    rest = [(i, j, BLK) for j in range(nb) for i in range(j + 1, nb)]
    n_steps = len(diag) + len(rest)
    assert n_steps % per_trip == 0
    n_trips = n_steps // per_trip
    n_masked = -(-len(diag) // n_trips)
    assert n_masked <= per_trip
    steps = []
    for _ in range(n_trips):
        for _ in range(n_masked):
            steps.append(diag.pop(0) if diag else rest.pop(0))
        for _ in range(per_trip - n_masked):
            steps.append(rest.pop(0))
    assert not diag and not rest
    return [jnp.asarray([s[c] for s in steps], jnp.int32) for c in range(3)], n_steps, n_masked


def _attention(q_all, k_all, vt_all, aux, score_bound, *, moba, lam_init):
    bsz, _, nb, _, _ = q_all.shape
    off = PAIRS if moba else 0
    assert nb % TILE_UNROLL == 0
    (itab, jtab, otab), n_steps, n_masked = _step_tables(nb)
    const = lambda b, h, it, jt, ot: (0, 0)
    in_specs = [
        pl.BlockSpec((1, 1, nb, BLK, LANES), lambda b, h, it, jt, ot: (b, h + off, 0, 0, 0)),
        pl.BlockSpec((1, 1, nb, BLK, LANES), lambda b, h, it, jt, ot: (b, h + off, 0, 0, 0)),
        pl.BlockSpec((1, 1, nb, LANES, BLK), lambda b, h, it, jt, ot: (b, h + off, 0, 0, 0)),
        pl.BlockSpec((1, 2 * BLK), const),
    ]
    scratch = [pltpu.VMEM((nb, 2 * BLK, LANES), bf16),
               pltpu.VMEM((nb, 1, 2 * BLK), f32), pltpu.VMEM((nb, 1, 2 * BLK), f32),
               pltpu.VMEM((nb, HEAD_DIM if moba else LANES, 2 * BLK), f32),
               pltpu.VMEM((1, TRIP_STEPS, BLK, 2 * BLK), f32)]
    if moba:
        (kmean,) = aux
        in_specs.append(pl.BlockSpec((1, nb, LANES), lambda b, h, it, jt, ot: (b, 0, h)))
        scratch.append(pltpu.VMEM((nb, nb, 2 * BLK), f32))
    else:
        lamvec, subg = aux
        in_specs.append(pl.BlockSpec(lamvec.shape, const))
        in_specs.append(pl.BlockSpec(subg.shape, const))
    mref = jnp.full((1, 2 * BLK), score_bound, f32)

    def run(online):
        return pl.pallas_call(
            functools.partial(_attn_kernel, moba=moba, online=online, lam_init=lam_init,
                              n_steps=n_steps, n_masked=n_masked),
            grid_spec=pltpu.PrefetchScalarGridSpec(
                num_scalar_prefetch=3,
                grid=(bsz, PAIRS),
                in_specs=in_specs,
                out_specs=pl.BlockSpec((1, nb * BLK, LANES), lambda b, h, it, jt, ot: (b, 0, h)),
                scratch_shapes=scratch),
            out_shape=jax.ShapeDtypeStruct((bsz, nb * BLK, PAIRS * LANES), bf16),
            compiler_params=pltpu.CompilerParams(
                dimension_semantics=("arbitrary", "arbitrary"), vmem_limit_bytes=VMEM_LIMIT),
            name=("moba_attn" if moba else "diff_attn") + ("_online" if online else ""),
        )(itab, jtab, otab, q_all, k_all, vt_all, mref, *aux)

    return lax.cond(score_bound <= MAX_FIXED_BOUND, lambda: run(False), lambda: run(True))


def _merge_kernel(oa_ref, ob_ref, x_ref, g1_ref, wg_ref, bg_ref, wa_ref, wb_ref, wo_ref, g2_ref,
                  x1_ref, h2_ref):
    d = x_ref.shape[-1]
    x = x_ref[0]
    ms = jnp.mean(x * x, axis=-1, keepdims=True)
    h = (x * lax.rsqrt(ms + EPS) * g1_ref[...]).astype(bf16)
    merged = None
    for br, (o_ref, w_ref) in enumerate(((oa_ref, wa_ref), (ob_ref, wb_ref))):
        gate = jax.nn.sigmoid(jnp.dot(h, wg_ref[:, br * d:(br + 1) * d], preferred_element_type=f32)
                              + bg_ref[:, br * d:(br + 1) * d])
        term = gate * jnp.dot(o_ref[0], w_ref[...], preferred_element_type=f32)
        merged = term if merged is None else merged + term
    x1 = x + jnp.dot(merged.astype(bf16), wo_ref[...], preferred_element_type=f32)
    x1_ref[0] = x1
    ms = jnp.mean(x1 * x1, axis=-1, keepdims=True)
    h2_ref[0] = (x1 * lax.rsqrt(ms + EPS) * g2_ref[...]).astype(bf16)


def _merge(oa, ob, x, g1, wg, bg, wa, wb, wo, g2, *, tm):
    bsz, seq, d = x.shape
    const = lambda b, i: (0, 0)
    tile = lambda b, i: (b, i, 0)
    return pl.pallas_call(
        _merge_kernel,
        grid=(bsz, seq // tm),
        in_specs=[
            pl.BlockSpec((1, tm, oa.shape[-1]), tile),
            pl.BlockSpec((1, tm, ob.shape[-1]), tile),
            pl.BlockSpec((1, tm, d), tile),
            pl.BlockSpec((1, d), const),
            pl.BlockSpec(wg.shape, const, pipeline_mode=pl.Buffered(1)),
            pl.BlockSpec(bg.shape, const),
            pl.BlockSpec(wa.shape, const),
            pl.BlockSpec(wb.shape, const),
            pl.BlockSpec(wo.shape, const),
            pl.BlockSpec((1, d), const),
        ],
        out_specs=[pl.BlockSpec((1, tm, d), tile), pl.BlockSpec((1, tm, d), tile)],
        out_shape=[jax.ShapeDtypeStruct((bsz, seq, d), f32),
                   jax.ShapeDtypeStruct((bsz, seq, d), bf16)],
        compiler_params=pltpu.CompilerParams(
            dimension_semantics=("arbitrary", "arbitrary"), vmem_limit_bytes=VMEM_LIMIT),
        name="merge_outproj",
    )(oa, ob, x, g1, wg, bg, wa, wb, wo, g2)


HALO = 16


MXU_TILE = 256


def _mlp_kernel(h2_ref, halo_ref, x1_ref, wup_ref, cw_ref, cb_ref, wd_ref, o_ref,
                ext_scr, u_scr, *, tm, tiles_per_seq):
    i = pl.program_id(0)
    dff = wd_ref.shape[0]
    ext_scr[HALO:, :] = h2_ref[...]

    @pl.when(i % tiles_per_seq == 0)
    def _():
        ext_scr[:HALO, :] = jnp.zeros((HALO, ext_scr.shape[1]), bf16)

    @pl.when(i % tiles_per_seq != 0)
    def _():
        ext_scr[:HALO, :] = halo_ref[...]

    split = (dff // (2 * MXU_TILE) + 1) * MXU_TILE
    acc = x1_ref[...]
    for c, (lo, hi) in enumerate(((0, split), (split, dff))):
        wid = hi - lo
        u_scr[c, :, :wid] = jnp.dot(ext_scr[...], wup_ref[:, lo:hi], preferred_element_type=f32)
        g = jnp.dot(h2_ref[...], wup_ref[:, dff + lo:dff + hi], preferred_element_type=f32)
        y = cb_ref[:, lo:hi]
        for j in range(CONV_W):
            y = y + (u_scr[c, pl.ds(HALO - (CONV_W - 1) + j, tm), :wid] * cw_ref[j:j + 1, lo:hi])
        act = (jax.nn.gelu(y) * g).astype(bf16)
        acc = acc + jnp.dot(act, wd_ref[lo:hi, :], preferred_element_type=f32)
    o_ref[...] = acc


def _mlp(h2, x1, w_up_bf, conv_w, conv_b, w_down_bf, *, seq, tm):
    n, d = h2.shape
    dff = w_down_bf.shape[0]
    assert dff % MXU_TILE == 0
    split = (dff // (2 * MXU_TILE) + 1) * MXU_TILE
    const = lambda i: (0, 0)
    return pl.pallas_call(
        functools.partial(_mlp_kernel, tm=tm, tiles_per_seq=seq // tm),
        grid=(n // tm,),
        in_specs=[
            pl.BlockSpec((tm, d), lambda i: (i, 0)),
            pl.BlockSpec((HALO, d), lambda i: (jnp.maximum(i * (tm // HALO) - 1, 0), 0)),
            pl.BlockSpec((tm, d), lambda i: (i, 0)),
            pl.BlockSpec(w_up_bf.shape, const, pipeline_mode=pl.Buffered(1)),
            pl.BlockSpec(conv_w.shape, const),
            pl.BlockSpec(conv_b.shape, const),
            pl.BlockSpec(w_down_bf.shape, const, pipeline_mode=pl.Buffered(1)),
        ],
        out_specs=pl.BlockSpec((tm, d), lambda i: (i, 0)),
        out_shape=jax.ShapeDtypeStruct((n, d), f32),
        scratch_shapes=[pltpu.VMEM((tm + HALO, d), bf16),
                        pltpu.VMEM((2, tm + HALO, split), f32)],
        compiler_params=pltpu.CompilerParams(
            dimension_semantics=("arbitrary",), vmem_limit_bytes=VMEM_LIMIT),
        name="conv_mlp",
    )(h2, h2, x1, w_up_bf, conv_w, conv_b, w_down_bf)


def _rope_tables(seq):
    inv = 1.0 / (ROPE_THETA ** (jnp.arange(0, HEAD_DIM, 2, dtype=f32) / HEAD_DIM))
    ang = jnp.arange(seq, dtype=f32)[:, None] * inv[None, :]
    cos, sin = jnp.cos(ang), jnp.sin(ang)
    reps = LANES // HEAD_DIM
    return (jnp.tile(cos, (1, 2 * reps)), jnp.tile(jnp.concatenate([-sin, sin], axis=1), (1, reps)))


def kernel(x, norm1_g, w_in, b_gate, qn_a, kn_a, lam_q1, lam_k1, lam_q2, lam_k2, subln_g, qn_b, kn_b,
           w_a_proj, w_b_proj, w_out, norm2_g, w_up, conv_w, conv_b, w_down):
    bsz, seq, d = x.shape
    depth = w_in.shape[0]
    assert seq % 512 == 0 and d % LANES == 0
    cos_t, sin_t = _rope_tables(seq)
    seg = jnp.arange(LANES) // HEAD_DIM
    bd = (seg[:, None] == seg[None, :]).astype(bf16)
    reps = LANES // HEAD_DIM
    qscale = HEAD_DIM ** -0.5 * LOG2E
    for l in range(depth):
        lam_init = 0.8 - 0.6 * math.exp(-0.3 * l)
        gains = jnp.stack([jnp.tile(qn_a[l], reps) * qscale, jnp.tile(kn_a[l], reps),
                           jnp.tile(qn_b[l], reps) * qscale, jnp.tile(kn_b[l], reps)])
        n_qkv = 3 * A_WIDTH + 3 * B_WIDTH
        q_all, k_all, vt_all, kmean = _inproj(
            x, norm1_g[l][None], w_in[l][:, :n_qkv].astype(bf16), gains, cos_t, sin_t, bd, tm=1024)
        kmean = kmean.reshape(bsz, seq // BLK, B_WIDTH)
        lamvec = jnp.stack([lam_q1[l], lam_k1[l], lam_q2[l], lam_k2[l]])
        gmax = jnp.max(jnp.abs(gains), axis=1) * (HEAD_DIM ** 0.5)
        subg_cols = jnp.tile(subln_g[l][:, None], (1, BLK))
        oa = _attention(q_all, k_all, vt_all, (lamvec, subg_cols),
                        gmax[0] * gmax[1] * BOUND_SLACK, moba=False, lam_init=lam_init)
        ob = _attention(q_all, k_all, vt_all, (kmean,),
                        gmax[2] * gmax[3] * BOUND_SLACK, moba=True, lam_init=lam_init)
        x1, h2 = _merge(oa, ob, x, norm1_g[l][None], w_in[l][:, n_qkv:].astype(bf16), b_gate[l][None],
                        w_a_proj[l].astype(bf16), w_b_proj[l].astype(bf16),
                        w_out[l].astype(bf16), norm2_g[l][None], tm=512)
        dff = w_down.shape[1]
        x = _mlp(h2.reshape(bsz * seq, d), x1.reshape(bsz * seq, d), w_up[l].astype(bf16), conv_w[l],
                 conv_b[l][None], w_down[l].astype(bf16), seq=seq, tm=512).reshape(bsz, seq, d)
    return x
```

```python
import functools
import math

import jax
import jax.numpy as jnp
from jax import lax
from jax.experimental import pallas as pl
from jax.experimental.pallas import tpu as pltpu

HEAD_DIM = 64
A_HEADS = 4
A_VDIM = 2 * HEAD_DIM
A_WIDTH = A_HEADS * A_VDIM
B_HEADS = 8
B_WIDTH = B_HEADS * HEAD_DIM
B_BLOCK = 256
B_TOPK = 3
CONV_W = 3
ROPE_THETA = 10000.0
EPS = 1e-6

LANES = 128
BLK = B_BLOCK
PAIRS = A_WIDTH // LANES
LOG2E = 1.4426950408889634
NEG_BIG = -1e30
POS_BIG = 1e30
VMEM_LIMIT = 56 * 1024 * 1024

f32 = jnp.float32
bf16 = jnp.bfloat16


def _nt_dot(a, b, **kw):
    return lax.dot_general(a, b, (((1,), (1,)), ((), ())), preferred_element_type=f32, **kw)


def _inproj_kernel(x_ref, g1_ref, w_ref, gain_ref, cos_ref, sin_ref, bd_ref,
                   q_ref, k_ref, vt_ref, kmean_ref, h_scr, *, tm):
    nblk = tm // BLK
    x = x_ref[0]
    ms = jnp.mean(x * x, axis=-1, keepdims=True)
    h_scr[...] = (x * lax.rsqrt(ms + EPS) * g1_ref[...]).astype(bf16)
    cos = cos_ref[...]
    sin = sin_ref[...]
    bd = bd_ref[...]
    lane = lax.broadcasted_iota(jnp.int32, (tm, LANES), 1)
    first_half = (lane & (HEAD_DIM - 1)) < (HEAD_DIM // 2)

    def col_tile(j):
        return jnp.dot(h_scr[...], w_ref[:, j * A_WIDTH:(j + 1) * A_WIDTH],
                       preferred_element_type=f32)

    def norm_rope(y, gain):
        ss = jnp.dot((y * y).astype(bf16), bd, preferred_element_type=f32)
        yn = y * lax.rsqrt(ss * (1.0 / HEAD_DIM) + EPS) * gain
        rot = jnp.where(first_half, pltpu.roll(yn, LANES - HEAD_DIM // 2, 1),
                        pltpu.roll(yn, HEAD_DIM // 2, 1))
        return yn * cos + rot * sin

    for branch in range(2):
        acc_q = col_tile(3 * branch)
        for c in range(PAIRS):
            out = norm_rope(acc_q[:, c * LANES:(c + 1) * LANES], gain_ref[2 * branch:2 * branch + 1, :])
            for bi in range(nblk):
                q_ref[0, branch * PAIRS + c, bi] = out[bi * BLK:(bi + 1) * BLK].astype(bf16)
        acc_k = col_tile(3 * branch + 1)
        for c in range(PAIRS):
            out = norm_rope(acc_k[:, c * LANES:(c + 1) * LANES],
                            gain_ref[2 * branch + 1:2 * branch + 2, :])
            for bi in range(nblk):
                blk = out[bi * BLK:(bi + 1) * BLK]
                k_ref[0, branch * PAIRS + c, bi] = blk.astype(bf16)
                if branch == 1:
                    kmean_ref[0, bi:bi + 1, c * LANES:(c + 1) * LANES] = jnp.mean(
                        blk, axis=0, keepdims=True)
        acc_v = col_tile(3 * branch + 2)
        for c in range(PAIRS):
            for bi in range(nblk):
                blk = acc_v[bi * BLK:(bi + 1) * BLK, c * LANES:(c + 1) * LANES]
                vt_ref[0, branch * PAIRS + c, bi] = blk.T.astype(bf16)


def _inproj(x, g1, w_in_bf, gains, cos_t, sin_t, bd, *, tm):
    bsz, seq, d = x.shape
    n_cols = w_in_bf.shape[1]
    nblk = tm // BLK
    spt = seq // tm
    nb = seq // BLK
    qk_shape = jax.ShapeDtypeStruct((bsz, 2 * PAIRS, nb, BLK, LANES), bf16)
    vt_shape = jax.ShapeDtypeStruct((bsz, 2 * PAIRS, nb, LANES, BLK), bf16)
    qk_spec = pl.BlockSpec((1, 2 * PAIRS, nblk, BLK, LANES), lambda b, i: (b, 0, i, 0, 0))
    vt_spec = pl.BlockSpec((1, 2 * PAIRS, nblk, LANES, BLK), lambda b, i: (b, 0, i, 0, 0))
    return pl.pallas_call(
        functools.partial(_inproj_kernel, tm=tm),
        grid=(bsz, spt),
        in_specs=[
            pl.BlockSpec((1, tm, d), lambda b, i: (b, i, 0)),
            pl.BlockSpec((1, d), lambda b, i: (0, 0)),
            pl.BlockSpec((d, n_cols), lambda b, i: (0, 0), pipeline_mode=pl.Buffered(1)),
            pl.BlockSpec((4, LANES), lambda b, i: (0, 0)),
            pl.BlockSpec((tm, LANES), lambda b, i: (i, 0)),
            pl.BlockSpec((tm, LANES), lambda b, i: (i, 0)),
            pl.BlockSpec((LANES, LANES), lambda b, i: (0, 0)),
        ],
        out_specs=[
            qk_spec, qk_spec, vt_spec,
            pl.BlockSpec((1, nblk, B_WIDTH), lambda b, i: (b * spt + i, 0, 0)),
        ],
        out_shape=[
            qk_shape, qk_shape, vt_shape,
            jax.ShapeDtypeStruct((bsz * spt, nblk, B_WIDTH), f32),
        ],
        scratch_shapes=[pltpu.VMEM((tm, d), bf16)],
        compiler_params=pltpu.CompilerParams(
            dimension_semantics=("arbitrary", "arbitrary"), vmem_limit_bytes=VMEM_LIMIT),
        name="inproj",
    )(x, g1, w_in_bf, gains, cos_t, sin_t, bd)


TRIP_STEPS = 34
TILE_UNROLL = 4
BOUND_SLACK = 1.0 + 2.0 ** -6
MAX_FIXED_BOUND = 60.0


def _attn_kernel(itab_ref, jtab_ref, otab_ref, *refs, moba, online, lam_init, n_steps, n_masked):
    if moba:
        (q_ref, k_ref, vt_ref, mref_ref, kmean_ref, o_ref,
         qs_scr, m_scr, l_scr, acc_scr, s_scr, sel_scr) = refs
    else:
        (q_ref, k_ref, vt_ref, mref_ref, lam_ref, subg_ref, o_ref,
         qs_scr, m_scr, l_scr, acc_scr, s_scr) = refs
    nb = k_ref.shape[2]
    acc_rows = acc_scr.shape[1]
    lane = lax.broadcasted_iota(jnp.int32, (BLK, LANES), 1)
    key_minus_row = (lax.broadcasted_iota(jnp.int32, (BLK, 2 * BLK), 0)
                     - (lax.broadcasted_iota(jnp.int32, (BLK, 2 * BLK), 1) & (BLK - 1)))

    if moba:
        km = kmean_ref[0]
        km_parts = []
        for _ in range(3):
            part = km.astype(bf16)
            km_parts.append(part)
            km = km - part.astype(f32)

    nidx = lax.broadcasted_iota(jnp.int32, (nb, 2 * BLK), 0)
    for i in range(nb):
        q = q_ref[0, 0, i].astype(f32)
        qs = jnp.concatenate([jnp.where(lane < HEAD_DIM, q, 0.0),
                              jnp.where(lane >= HEAD_DIM, q, 0.0)], axis=0).astype(bf16)
        qs_scr[i] = qs
        if online:
            m_scr[i] = jnp.full((1, 2 * BLK), NEG_BIG, f32)
        l_scr[i] = jnp.zeros((1, 2 * BLK), f32)
        acc_scr[i] = jnp.zeros((acc_rows, 2 * BLK), f32)
        if moba:
            keep = nidx <= i
            if i > B_TOPK:
                g = _nt_dot(km_parts[0], qs) + _nt_dot(km_parts[1], qs) + _nt_dot(km_parts[2], qs)
                cnt = jnp.zeros((nb, 2 * BLK), f32)
                for mm in range(i):
                    row = g[mm:mm + 1, :]
                    beats = (row > g) | ((row == g) & (nidx > mm))
                    cnt = cnt + jnp.where(beats, 1.0, 0.0)
                keep = ((cnt < float(B_TOPK)) & (nidx < i)) | (nidx == i)
            sel_scr[i] = jnp.where(keep, 1.0, 0.0)

    def qk(step, slot, u):
        s = _nt_dot(k_ref[0, 0, jtab_ref[step]], qs_scr[itab_ref[step]])
        if slot == 0 and u < n_masked:
            s = jnp.where(key_minus_row <= otab_ref[step], s, NEG_BIG)
        s_scr[slot, u] = s

    def update(step, slot, u):
        i = itab_ref[step]
        j = jtab_ref[step]
        s = s_scr[slot, u]
        if moba:
            chosen = sel_scr[i, pl.ds(j, 1), :] > 0.5
        if online:
            m_old = m_scr[i]
            m_new = jnp.maximum(m_old, jnp.max(s, axis=0, keepdims=True))
            if moba:
                m_new = jnp.where(chosen, m_new, m_old)
            m_sub = m_new
            alpha = jnp.exp2(m_old - m_new)
            m_scr[i] = m_new
        else:
            m_sub = mref_ref[...]
        if moba:
            m_sub = jnp.where(chosen, m_sub, POS_BIG)
        p = jnp.exp2(s - m_sub)
        psum = jnp.sum(p, axis=0, keepdims=True)
        pb = p.astype(bf16)
        vt = vt_ref[0, 0, j]
        if moba:
            pv = jnp.concatenate(
                [jnp.dot(vt[:HEAD_DIM], pb[:, :BLK], preferred_element_type=f32),
                 jnp.dot(vt[HEAD_DIM:], pb[:, BLK:], preferred_element_type=f32)], axis=1)
        else:
            pv = jnp.dot(vt, pb, preferred_element_type=f32)
        if online:
            l_scr[i] = alpha * l_scr[i] + psum
            acc_scr[i] = alpha * acc_scr[i] + pv
        else:
            l_scr[i] = l_scr[i] + psum
            acc_scr[i] = acc_scr[i] + pv

    def trip(it, carry):
        base = it * TRIP_STEPS
        for u in range(TRIP_STEPS):
            qk(base + u, 0, u)
        for u in range(TRIP_STEPS):
            update(base + u, 0, u)
        return carry

    lax.fori_loop(0, n_steps // TRIP_STEPS, trip, 0)

    if not moba:
        lv = lam_ref[...]
        lam = (jnp.exp(jnp.sum(lv[0:1] * lv[1:2], axis=-1, keepdims=True))
               - jnp.exp(jnp.sum(lv[2:3] * lv[3:4], axis=-1, keepdims=True)) + lam_init)

    def finish_tiles(t, carry):
        for r in range(TILE_UNROLL):
            i = t * TILE_UNROLL + r
            o = acc_scr[i] * (1.0 / l_scr[i])
            rows = pl.ds(pl.multiple_of(i * BLK, BLK), BLK)
            if moba:
                ot = jnp.concatenate([o[:, :BLK], o[:, BLK:]], axis=0).T
                o_ref[0, rows, :] = ot.astype(bf16)
            else:
                od = o[:, :BLK] - lam * o[:, BLK:]
                ms = jnp.mean(od * od, axis=0, keepdims=True)
                od = od * lax.rsqrt(ms + EPS) * subg_ref[...] * (1.0 - lam_init)
                o_ref[0, rows, :] = od.T.astype(bf16)
        return carry

    lax.fori_loop(0, nb // TILE_UNROLL, finish_tiles, 0)


def _step_tables(nb):
    per_trip = TRIP_STEPS
    diag = [(i, i, 0) for i in range(nb)]
    rest = [(i, j, BLK) for j in range(nb) for i in range(j + 1, nb)]
    n_steps = len(diag) + len(rest)
    assert n_steps % per_trip == 0
    n_trips = n_steps // per_trip
    n_masked = -(-len(diag) // n_trips)
    assert n_masked <= per_trip
    steps = []
    for _ in range(n_trips):
        for _ in range(n_masked):
            steps.append(diag.pop(0) if diag else rest.pop(0))
        for _ in range(per_trip - n_masked):
            steps.append(rest.pop(0))
    assert not diag and not rest
    return [jnp.asarray([s[c] for s in steps], jnp.int32) for c in range(3)], n_steps, n_masked


def _attention(q_all, k_all, vt_all, aux, score_bound, *, moba, lam_init):
    bsz, _, nb, _, _ = q_all.shape
    off = PAIRS if moba else 0
    assert nb % TILE_UNROLL == 0
    (itab, jtab, otab), n_steps, n_masked = _step_tables(nb)
    const = lambda b, h, it, jt, ot: (0, 0)
    in_specs = [
        pl.BlockSpec((1, 1, nb, BLK, LANES), lambda b, h, it, jt, ot: (b, h + off, 0, 0, 0)),
        pl.BlockSpec((1, 1, nb, BLK, LANES), lambda b, h, it, jt, ot: (b, h + off, 0, 0, 0)),
        pl.BlockSpec((1, 1, nb, LANES, BLK), lambda b, h, it, jt, ot: (b, h + off, 0, 0, 0)),
        pl.BlockSpec((1, 2 * BLK), const),
    ]
    scratch = [pltpu.VMEM((nb, 2 * BLK, LANES), bf16),
               pltpu.VMEM((nb, 1, 2 * BLK), f32), pltpu.VMEM((nb, 1, 2 * BLK), f32),
               pltpu.VMEM((nb, HEAD_DIM if moba else LANES, 2 * BLK), f32),
               pltpu.VMEM((1, TRIP_STEPS, BLK, 2 * BLK), f32)]
    if moba:
        (kmean,) = aux
        in_specs.append(pl.BlockSpec((1, nb, LANES), lambda b, h, it, jt, ot: (b, 0, h)))
        scratch.append(pltpu.VMEM((nb, nb, 2 * BLK), f32))
    else:
        lamvec, subg = aux
        in_specs.append(pl.BlockSpec(lamvec.shape, const))
        in_specs.append(pl.BlockSpec(subg.shape, const))
    mref = jnp.full((1, 2 * BLK), score_bound, f32)

    def run(online):
        return pl.pallas_call(
            functools.partial(_attn_kernel, moba=moba, online=online, lam_init=lam_init,
                              n_steps=n_steps, n_masked=n_masked),
            grid_spec=pltpu.PrefetchScalarGridSpec(
                num_scalar_prefetch=3,
                grid=(bsz, PAIRS),
                in_specs=in_specs,
                out_specs=pl.BlockSpec((1, nb * BLK, LANES), lambda b, h, it, jt, ot: (b, 0, h)),
                scratch_shapes=scratch),
            out_shape=jax.ShapeDtypeStruct((bsz, nb * BLK, PAIRS * LANES), bf16),
            compiler_params=pltpu.CompilerParams(
                dimension_semantics=("arbitrary", "arbitrary"), vmem_limit_bytes=VMEM_LIMIT),
            name=("moba_attn" if moba else "diff_attn") + ("_online" if online else ""),
        )(itab, jtab, otab, q_all, k_all, vt_all, mref, *aux)

    return lax.cond(score_bound <= MAX_FIXED_BOUND, lambda: run(False), lambda: run(True))


def _merge_kernel(oa_ref, ob_ref, x_ref, g1_ref, wg_ref, bg_ref, wa_ref, wb_ref, wo_ref, g2_ref,
                  x1_ref, h2_ref):
    d = x_ref.shape[-1]
    x = x_ref[0]
    ms = jnp.mean(x * x, axis=-1, keepdims=True)
    h = (x * lax.rsqrt(ms + EPS) * g1_ref[...]).astype(bf16)
    gates = jax.nn.sigmoid(jnp.dot(h, wg_ref[...], preferred_element_type=f32) + bg_ref[...])
    merged = (gates[:, :d] * jnp.dot(oa_ref[0], wa_ref[...], preferred_element_type=f32)
              + gates[:, d:] * jnp.dot(ob_ref[0], wb_ref[...], preferred_element_type=f32))
    x1 = x + jnp.dot(merged.astype(bf16), wo_ref[...], preferred_element_type=f32)
    x1_ref[0] = x1
    ms = jnp.mean(x1 * x1, axis=-1, keepdims=True)
    h2_ref[0] = (x1 * lax.rsqrt(ms + EPS) * g2_ref[...]).astype(bf16)


def _merge(oa, ob, x, g1, wg, bg, wa, wb, wo, g2, *, tm):
    bsz, seq, d = x.shape
    const = lambda b, i: (0, 0)
    tile = lambda b, i: (b, i, 0)
    return pl.pallas_call(
        _merge_kernel,
        grid=(bsz, seq // tm),
        in_specs=[
            pl.BlockSpec((1, tm, oa.shape[-1]), tile),
            pl.BlockSpec((1, tm, ob.shape[-1]), tile),
            pl.BlockSpec((1, tm, d), tile),
            pl.BlockSpec((1, d), const),
            pl.BlockSpec(wg.shape, const, pipeline_mode=pl.Buffered(1)),
            pl.BlockSpec(bg.shape, const),
            pl.BlockSpec(wa.shape, const),
            pl.BlockSpec(wb.shape, const),
            pl.BlockSpec(wo.shape, const),
            pl.BlockSpec((1, d), const),
        ],
        out_specs=[pl.BlockSpec((1, tm, d), tile), pl.BlockSpec((1, tm, d), tile)],
        out_shape=[jax.ShapeDtypeStruct((bsz, seq, d), f32),
                   jax.ShapeDtypeStruct((bsz, seq, d), bf16)],
        compiler_params=pltpu.CompilerParams(
            dimension_semantics=("arbitrary", "arbitrary"), vmem_limit_bytes=VMEM_LIMIT),
        name="merge_outproj",
    )(oa, ob, x, g1, wg, bg, wa, wb, wo, g2)


HALO = 16


MXU_TILE = 256
FF_CHUNKS = 1


def _mlp_kernel(h2_ref, halo_ref, x1_ref, wup_ref, cw_ref, cb_ref, wd_ref, o_ref,
                ext_scr, u_scr, *, tm, tiles_per_seq):
    i = pl.program_id(0)
    dff = wd_ref.shape[0]
    ext_scr[HALO:, :] = h2_ref[...]

    @pl.when(i % tiles_per_seq == 0)
    def _():
        ext_scr[:HALO, :] = jnp.zeros((HALO, ext_scr.shape[1]), bf16)

    @pl.when(i % tiles_per_seq != 0)
    def _():
        ext_scr[:HALO, :] = halo_ref[...]

    width = u_scr.shape[2]
    acc = x1_ref[...]
    for c in range(u_scr.shape[0]):
        lo, hi = c * width, min((c + 1) * width, dff)
        wid = hi - lo
        u_scr[c, :, :wid] = jnp.dot(ext_scr[...], wup_ref[:, lo:hi], preferred_element_type=f32)
        g = jnp.dot(h2_ref[...], wup_ref[:, dff + lo:dff + hi], preferred_element_type=f32)
        y = cb_ref[:, lo:hi]
        for j in range(CONV_W):
            y = y + (u_scr[c, pl.ds(HALO - (CONV_W - 1) + j, tm), :wid] * cw_ref[j:j + 1, lo:hi])
        act = (jax.nn.gelu(y) * g).astype(bf16)
        acc = acc + jnp.dot(act, wd_ref[lo:hi, :], preferred_element_type=f32)
    o_ref[...] = acc


def _mlp(h2, x1, w_up_bf, conv_w, conv_b, w_down_bf, *, seq, tm):
    n, d = h2.shape
    dff = w_down_bf.shape[0]
    assert dff % MXU_TILE == 0
    width = -(-dff // (FF_CHUNKS * MXU_TILE)) * MXU_TILE
    const = lambda i: (0, 0)
    return pl.pallas_call(
        functools.partial(_mlp_kernel, tm=tm, tiles_per_seq=seq // tm),
        grid=(n // tm,),
        in_specs=[
            pl.BlockSpec((tm, d), lambda i: (i, 0)),
            pl.BlockSpec((HALO, d), lambda i: (jnp.maximum(i * (tm // HALO) - 1, 0), 0)),
            pl.BlockSpec((tm, d), lambda i: (i, 0)),
            pl.BlockSpec(w_up_bf.shape, const, pipeline_mode=pl.Buffered(1)),
            pl.BlockSpec(conv_w.shape, const),
            pl.BlockSpec(conv_b.shape, const),
            pl.BlockSpec(w_down_bf.shape, const, pipeline_mode=pl.Buffered(1)),
        ],
        out_specs=pl.BlockSpec((tm, d), lambda i: (i, 0)),
        out_shape=jax.ShapeDtypeStruct((n, d), f32),
        scratch_shapes=[pltpu.VMEM((tm + HALO, d), bf16),
                        pltpu.VMEM((-(-dff // width), tm + HALO, width), f32)],
        compiler_params=pltpu.CompilerParams(
            dimension_semantics=("arbitrary",), vmem_limit_bytes=VMEM_LIMIT),
        name="conv_mlp",
    )(h2, h2, x1, w_up_bf, conv_w, conv_b, w_down_bf)


def _rope_tables(seq):
    inv = 1.0 / (ROPE_THETA ** (jnp.arange(0, HEAD_DIM, 2, dtype=f32) / HEAD_DIM))
    ang = jnp.arange(seq, dtype=f32)[:, None] * inv[None, :]
    cos, sin = jnp.cos(ang), jnp.sin(ang)
    reps = LANES // HEAD_DIM
    return (jnp.tile(cos, (1, 2 * reps)), jnp.tile(jnp.concatenate([-sin, sin], axis=1), (1, reps)))


def kernel(x, norm1_g, w_in, b_gate, qn_a, kn_a, lam_q1, lam_k1, lam_q2, lam_k2, subln_g, qn_b, kn_b,
           w_a_proj, w_b_proj, w_out, norm2_g, w_up, conv_w, conv_b, w_down):
    bsz, seq, d = x.shape
    depth = w_in.shape[0]
    assert seq % 512 == 0 and d % LANES == 0
    cos_t, sin_t = _rope_tables(seq)
    seg = jnp.arange(LANES) // HEAD_DIM
    bd = (seg[:, None] == seg[None, :]).astype(bf16)
    reps = LANES // HEAD_DIM
    qscale = HEAD_DIM ** -0.5 * LOG2E
    for l in range(depth):
        lam_init = 0.8 - 0.6 * math.exp(-0.3 * l)
        gains = jnp.stack([jnp.tile(qn_a[l], reps) * qscale, jnp.tile(kn_a[l], reps),
                           jnp.tile(qn_b[l], reps) * qscale, jnp.tile(kn_b[l], reps)])
        n_qkv = 3 * A_WIDTH + 3 * B_WIDTH
        q_all, k_all, vt_all, kmean = _inproj(
            x, norm1_g[l][None], w_in[l][:, :n_qkv].astype(bf16), gains, cos_t, sin_t, bd, tm=1024)
        kmean = kmean.reshape(bsz, seq // BLK, B_WIDTH)
        lamvec = jnp.stack([lam_q1[l], lam_k1[l], lam_q2[l], lam_k2[l]])
        gmax = jnp.max(jnp.abs(gains), axis=1) * (HEAD_DIM ** 0.5)
        subg_cols = jnp.tile(subln_g[l][:, None], (1, BLK))
        oa = _attention(q_all, k_all, vt_all, (lamvec, subg_cols),
                        gmax[0] * gmax[1] * BOUND_SLACK, moba=False, lam_init=lam_init)
        ob = _attention(q_all, k_all, vt_all, (kmean,),
                        gmax[2] * gmax[3] * BOUND_SLACK, moba=True, lam_init=lam_init)
        x1, h2 = _merge(oa, ob, x, norm1_g[l][None], w_in[l][:, n_qkv:].astype(bf16), b_gate[l][None],
                        w_a_proj[l].astype(bf16), w_b_proj[l].astype(bf16),
                        w_out[l].astype(bf16), norm2_g[l][None], tm=1024)
        dff = w_down.shape[1]
        x = _mlp(h2.reshape(bsz * seq, d), x1.reshape(bsz * seq, d), w_up[l].astype(bf16), conv_w[l],
                 conv_b[l][None], w_down[l].astype(bf16), seq=seq, tm=512).reshape(bsz, seq, d)
    return x
```

```python
import functools
import math

import jax
import jax.numpy as jnp
from jax import lax
from jax.experimental import pallas as pl
from jax.experimental.pallas import tpu as pltpu

HEAD_DIM = 64
A_HEADS = 4
A_VDIM = 2 * HEAD_DIM
A_WIDTH = A_HEADS * A_VDIM
B_HEADS = 8
B_WIDTH = B_HEADS * HEAD_DIM
B_BLOCK = 256
B_TOPK = 3
CONV_W = 3
ROPE_THETA = 10000.0
EPS = 1e-6

LANES = 128
BLK = B_BLOCK
PAIRS = A_WIDTH // LANES
LOG2E = 1.4426950408889634
NEG_BIG = -1e30
POS_BIG = 1e30
VMEM_LIMIT = 56 * 1024 * 1024

f32 = jnp.float32
bf16 = jnp.bfloat16


def _nt_dot(a, b, **kw):
    return lax.dot_general(a, b, (((1,), (1,)), ((), ())), preferred_element_type=f32, **kw)


def _inproj_kernel(x_ref, g1_ref, w_ref, gain_ref, cos_ref, sin_ref, bd_ref,
                   q_ref, k_ref, vt_ref, kmean_ref, h_scr, *, tm):
    nblk = tm // BLK
    x = x_ref[0]
    ms = jnp.mean(x * x, axis=-1, keepdims=True)
    h_scr[...] = (x * lax.rsqrt(ms + EPS) * g1_ref[...]).astype(bf16)
    cos = cos_ref[...]
    sin = sin_ref[...]
    bd = bd_ref[...]
    lane = lax.broadcasted_iota(jnp.int32, (tm, LANES), 1)
    first_half = (lane & (HEAD_DIM - 1)) < (HEAD_DIM // 2)

    def col_tile(j):
        return jnp.dot(h_scr[...], w_ref[:, j * A_WIDTH:(j + 1) * A_WIDTH],
                       preferred_element_type=f32)

    def norm_rope(y, gain):
        ss = jnp.dot((y * y).astype(bf16), bd, preferred_element_type=f32)
        yn = y * lax.rsqrt(ss * (1.0 / HEAD_DIM) + EPS) * gain
        rot = jnp.where(first_half, pltpu.roll(yn, LANES - HEAD_DIM // 2, 1),
                        pltpu.roll(yn, HEAD_DIM // 2, 1))
        return yn * cos + rot * sin

    for branch in range(2):
        acc_q = col_tile(3 * branch)
        for c in range(PAIRS):
            out = norm_rope(acc_q[:, c * LANES:(c + 1) * LANES], gain_ref[2 * branch:2 * branch + 1, :])
            for bi in range(nblk):
                q_ref[0, branch * PAIRS + c, bi] = out[bi * BLK:(bi + 1) * BLK].astype(bf16)
        acc_k = col_tile(3 * branch + 1)
        for c in range(PAIRS):
            out = norm_rope(acc_k[:, c * LANES:(c + 1) * LANES],
                            gain_ref[2 * branch + 1:2 * branch + 2, :])
            for bi in range(nblk):
                blk = out[bi * BLK:(bi + 1) * BLK]
                k_ref[0, branch * PAIRS + c, bi] = blk.astype(bf16)
                if branch == 1:
                    kmean_ref[0, bi:bi + 1, c * LANES:(c + 1) * LANES] = jnp.mean(
                        blk, axis=0, keepdims=True)
        acc_v = col_tile(3 * branch + 2)
        for c in range(PAIRS):
            for bi in range(nblk):
                blk = acc_v[bi * BLK:(bi + 1) * BLK, c * LANES:(c + 1) * LANES]
                vt_ref[0, branch * PAIRS + c, bi] = blk.T.astype(bf16)


def _inproj(x, g1, w_in_bf, gains, cos_t, sin_t, bd, *, tm):
    bsz, seq, d = x.shape
    n_cols = w_in_bf.shape[1]
    nblk = tm // BLK
    spt = seq // tm
    nb = seq // BLK
    qk_shape = jax.ShapeDtypeStruct((bsz, 2 * PAIRS, nb, BLK, LANES), bf16)
    vt_shape = jax.ShapeDtypeStruct((bsz, 2 * PAIRS, nb, LANES, BLK), bf16)
    qk_spec = pl.BlockSpec((1, 2 * PAIRS, nblk, BLK, LANES), lambda b, i: (b, 0, i, 0, 0))
    vt_spec = pl.BlockSpec((1, 2 * PAIRS, nblk, LANES, BLK), lambda b, i: (b, 0, i, 0, 0))
    return pl.pallas_call(
        functools.partial(_inproj_kernel, tm=tm),
        grid=(bsz, spt),
        in_specs=[
            pl.BlockSpec((1, tm, d), lambda b, i: (b, i, 0)),
            pl.BlockSpec((1, d), lambda b, i: (0, 0)),
            pl.BlockSpec((d, n_cols), lambda b, i: (0, 0), pipeline_mode=pl.Buffered(1)),
            pl.BlockSpec((4, LANES), lambda b, i: (0, 0)),
            pl.BlockSpec((tm, LANES), lambda b, i: (i, 0)),
            pl.BlockSpec((tm, LANES), lambda b, i: (i, 0)),
            pl.BlockSpec((LANES, LANES), lambda b, i: (0, 0)),
        ],
        out_specs=[
            qk_spec, qk_spec, vt_spec,
            pl.BlockSpec((1, nblk, B_WIDTH), lambda b, i: (b * spt + i, 0, 0)),
        ],
        out_shape=[
            qk_shape, qk_shape, vt_shape,
            jax.ShapeDtypeStruct((bsz * spt, nblk, B_WIDTH), f32),
        ],
        scratch_shapes=[pltpu.VMEM((tm, d), bf16)],
        compiler_params=pltpu.CompilerParams(
            dimension_semantics=("arbitrary", "arbitrary"), vmem_limit_bytes=VMEM_LIMIT),
        name="inproj",
    )(x, g1, w_in_bf, gains, cos_t, sin_t, bd)


TRIP_STEPS = 34
QK_AHEAD = 6
TILE_UNROLL = 4
BOUND_SLACK = 1.0 + 2.0 ** -6
MAX_FIXED_BOUND = 60.0


def _attn_kernel(itab_ref, jtab_ref, otab_ref, *refs, moba, online, lam_init, n_steps, n_masked):
    if moba:
        (q_ref, k_ref, vt_ref, mref_ref, kmean_ref, o_ref,
         qs_scr, m_scr, l_scr, acc_scr, s_scr, sel_scr) = refs
    else:
        (q_ref, k_ref, vt_ref, mref_ref, lam_ref, subg_ref, o_ref,
         qs_scr, m_scr, l_scr, acc_scr, s_scr) = refs
    nb = k_ref.shape[2]
    acc_rows = acc_scr.shape[1]
    lane = lax.broadcasted_iota(jnp.int32, (BLK, LANES), 1)
    key_minus_row = (lax.broadcasted_iota(jnp.int32, (BLK, 2 * BLK), 0)
                     - (lax.broadcasted_iota(jnp.int32, (BLK, 2 * BLK), 1) & (BLK - 1)))

    if moba:
        km = kmean_ref[0]
        km_parts = []
        for _ in range(3):
            part = km.astype(bf16)
            km_parts.append(part)
            km = km - part.astype(f32)

    nidx = lax.broadcasted_iota(jnp.int32, (nb, 2 * BLK), 0)
    for i in range(nb):
        q = q_ref[0, 0, i].astype(f32)
        qs = jnp.concatenate([jnp.where(lane < HEAD_DIM, q, 0.0),
                              jnp.where(lane >= HEAD_DIM, q, 0.0)], axis=0).astype(bf16)
        qs_scr[i] = qs
        if online:
            m_scr[i] = jnp.full((1, 2 * BLK), NEG_BIG, f32)
        l_scr[i] = jnp.zeros((1, 2 * BLK), f32)
        acc_scr[i] = jnp.zeros((acc_rows, 2 * BLK), f32)
        if moba:
            keep = nidx <= i
            if i > B_TOPK:
                g = _nt_dot(km_parts[0], qs) + _nt_dot(km_parts[1], qs) + _nt_dot(km_parts[2], qs)
                cnt = jnp.zeros((nb, 2 * BLK), f32)
                for mm in range(i):
                    row = g[mm:mm + 1, :]
                    beats = (row > g) | ((row == g) & (nidx > mm))
                    cnt = cnt + jnp.where(beats, 1.0, 0.0)
                keep = ((cnt < float(B_TOPK)) & (nidx < i)) | (nidx == i)
            sel_scr[i] = jnp.where(keep, 1.0, 0.0)

    def qk(step, slot, u):
        s = _nt_dot(k_ref[0, 0, jtab_ref[step]], qs_scr[itab_ref[step]])
        if slot == 0 and u < n_masked:
            s = jnp.where(key_minus_row <= otab_ref[step], s, NEG_BIG)
        s_scr[slot, u] = s

    def update(step, slot, u):
        i = itab_ref[step]
        j = jtab_ref[step]
        s = s_scr[slot, u]
        if moba:
            chosen = sel_scr[i, pl.ds(j, 1), :] > 0.5
        if online:
            m_old = m_scr[i]
            m_new = jnp.maximum(m_old, jnp.max(s, axis=0, keepdims=True))
            if moba:
                m_new = jnp.where(chosen, m_new, m_old)
            m_sub = m_new
            alpha = jnp.exp2(m_old - m_new)
            m_scr[i] = m_new
        else:
            m_sub = mref_ref[...]
        if moba:
            m_sub = jnp.where(chosen, m_sub, POS_BIG)
        p = jnp.exp2(s - m_sub)
        psum = jnp.sum(p, axis=0, keepdims=True)
        pb = p.astype(bf16)
        vt = vt_ref[0, 0, j]
        if moba:
            pv = jnp.concatenate(
                [jnp.dot(vt[:HEAD_DIM], pb[:, :BLK], preferred_element_type=f32),
                 jnp.dot(vt[HEAD_DIM:], pb[:, BLK:], preferred_element_type=f32)], axis=1)
        else:
            pv = jnp.dot(vt, pb, preferred_element_type=f32)
        if online:
            l_scr[i] = alpha * l_scr[i] + psum
            acc_scr[i] = alpha * acc_scr[i] + pv
        else:
            l_scr[i] = l_scr[i] + psum
            acc_scr[i] = acc_scr[i] + pv

    def trip(it, carry):
        base = it * TRIP_STEPS
        for u in range(QK_AHEAD):
            qk(base + u, 0, u)
        for u in range(TRIP_STEPS):
            if u + QK_AHEAD < TRIP_STEPS:
                qk(base + u + QK_AHEAD, 0, u + QK_AHEAD)
            update(base + u, 0, u)
        return carry

    lax.fori_loop(0, n_steps // TRIP_STEPS, trip, 0)

    if not moba:
        lv = lam_ref[...]
        lam = (jnp.exp(jnp.sum(lv[0:1] * lv[1:2], axis=-1, keepdims=True))
               - jnp.exp(jnp.sum(lv[2:3] * lv[3:4], axis=-1, keepdims=True)) + lam_init)

    def finish_tiles(t, carry):
        for r in range(TILE_UNROLL):
            i = t * TILE_UNROLL + r
            o = acc_scr[i] * (1.0 / l_scr[i])
            rows = pl.ds(pl.multiple_of(i * BLK, BLK), BLK)
            if moba:
                ot = jnp.concatenate([o[:, :BLK], o[:, BLK:]], axis=0).T
                o_ref[0, rows, :] = ot.astype(bf16)
            else:
                od = o[:, :BLK] - lam * o[:, BLK:]
                ms = jnp.mean(od * od, axis=0, keepdims=True)
                od = od * lax.rsqrt(ms + EPS) * subg_ref[...] * (1.0 - lam_init)
                o_ref[0, rows, :] = od.T.astype(bf16)
        return carry

    lax.fori_loop(0, nb // TILE_UNROLL, finish_tiles, 0)


def _step_tables(nb):
    per_trip = TRIP_STEPS
    diag = [(i, i, 0) for i in range(nb)]
    rest = [(i, j, BLK) for j in range(nb) for i in range(j + 1, nb)]
    n_steps = len(diag) + len(rest)
    assert n_steps % per_trip == 0
    n_trips = n_steps // per_trip
    n_masked = -(-len(diag) // n_trips)
    assert n_masked <= per_trip
    steps = []
    for _ in range(n_trips):
        for _ in range(n_masked):
            steps.append(diag.pop(0) if diag else rest.pop(0))
        for _ in range(per_trip - n_masked):
            steps.append(rest.pop(0))
    assert not diag and not rest
    return [jnp.asarray([s[c] for s in steps], jnp.int32) for c in range(3)], n_steps, n_masked


def _attention(q_all, k_all, vt_all, aux, score_bound, *, moba, lam_init):
    bsz, _, nb, _, _ = q_all.shape
    off = PAIRS if moba else 0
    assert nb % TILE_UNROLL == 0
    (itab, jtab, otab), n_steps, n_masked = _step_tables(nb)
    const = lambda b, h, it, jt, ot: (0, 0)
    in_specs = [
        pl.BlockSpec((1, 1, nb, BLK, LANES), lambda b, h, it, jt, ot: (b, h + off, 0, 0, 0)),
        pl.BlockSpec((1, 1, nb, BLK, LANES), lambda b, h, it, jt, ot: (b, h + off, 0, 0, 0)),
        pl.BlockSpec((1, 1, nb, LANES, BLK), lambda b, h, it, jt, ot: (b, h + off, 0, 0, 0)),
        pl.BlockSpec((1, 2 * BLK), const),
    ]
    scratch = [pltpu.VMEM((nb, 2 * BLK, LANES), bf16),
               pltpu.VMEM((nb, 1, 2 * BLK), f32), pltpu.VMEM((nb, 1, 2 * BLK), f32),
               pltpu.VMEM((nb, HEAD_DIM if moba else LANES, 2 * BLK), f32),
               pltpu.VMEM((1, TRIP_STEPS, BLK, 2 * BLK), f32)]
    if moba:
        (kmean,) = aux
        in_specs.append(pl.BlockSpec((1, nb, LANES), lambda b, h, it, jt, ot: (b, 0, h)))
        scratch.append(pltpu.VMEM((nb, nb, 2 * BLK), f32))
    else:
        lamvec, subg = aux
        in_specs.append(pl.BlockSpec(lamvec.shape, const))
        in_specs.append(pl.BlockSpec(subg.shape, const))
    mref = jnp.full((1, 2 * BLK), score_bound, f32)

    def run(online):
        return pl.pallas_call(
            functools.partial(_attn_kernel, moba=moba, online=online, lam_init=lam_init,
                              n_steps=n_steps, n_masked=n_masked),
            grid_spec=pltpu.PrefetchScalarGridSpec(
                num_scalar_prefetch=3,
                grid=(bsz, PAIRS),
                in_specs=in_specs,
                out_specs=pl.BlockSpec((1, nb * BLK, LANES), lambda b, h, it, jt, ot: (b, 0, h)),
                scratch_shapes=scratch),
            out_shape=jax.ShapeDtypeStruct((bsz, nb * BLK, PAIRS * LANES), bf16),
            compiler_params=pltpu.CompilerParams(
                dimension_semantics=("arbitrary", "arbitrary"), vmem_limit_bytes=VMEM_LIMIT),
            name=("moba_attn" if moba else "diff_attn") + ("_online" if online else ""),
        )(itab, jtab, otab, q_all, k_all, vt_all, mref, *aux)

    return lax.cond(score_bound <= MAX_FIXED_BOUND, lambda: run(False), lambda: run(True))


def _merge_kernel(oa_ref, ob_ref, x_ref, g1_ref, wg_ref, bg_ref, wa_ref, wb_ref, wo_ref, g2_ref,
                  x1_ref, h2_ref):
    d = x_ref.shape[-1]
    x = x_ref[0]
    ms = jnp.mean(x * x, axis=-1, keepdims=True)
    h = (x * lax.rsqrt(ms + EPS) * g1_ref[...]).astype(bf16)
    n_qkv = wg_ref.shape[1] - 2 * d
    gates = jax.nn.sigmoid(jnp.dot(h, wg_ref[:, n_qkv:], preferred_element_type=f32) + bg_ref[...])
    merged = (gates[:, :d] * jnp.dot(oa_ref[0], wa_ref[...], preferred_element_type=f32)
              + gates[:, d:] * jnp.dot(ob_ref[0], wb_ref[...], preferred_element_type=f32))
    x1 = x + jnp.dot(merged.astype(bf16), wo_ref[...], preferred_element_type=f32)
    x1_ref[0] = x1
    ms = jnp.mean(x1 * x1, axis=-1, keepdims=True)
    h2_ref[0] = (x1 * lax.rsqrt(ms + EPS) * g2_ref[...]).astype(bf16)


def _merge(oa, ob, x, g1, wg, bg, wa, wb, wo, g2, *, tm):
    bsz, seq, d = x.shape
    const = lambda b, i: (0, 0)
    tile = lambda b, i: (b, i, 0)
    return pl.pallas_call(
        _merge_kernel,
        grid=(bsz, seq // tm),
        in_specs=[
            pl.BlockSpec((1, tm, oa.shape[-1]), tile),
            pl.BlockSpec((1, tm, ob.shape[-1]), tile),
            pl.BlockSpec((1, tm, d), tile),
            pl.BlockSpec((1, d), const),
            pl.BlockSpec(wg.shape, const, pipeline_mode=pl.Buffered(1)),
            pl.BlockSpec(bg.shape, const),
            pl.BlockSpec(wa.shape, const),
            pl.BlockSpec(wb.shape, const),
            pl.BlockSpec(wo.shape, const),
            pl.BlockSpec((1, d), const),
        ],
        out_specs=[pl.BlockSpec((1, tm, d), tile), pl.BlockSpec((1, tm, d), tile)],
        out_shape=[jax.ShapeDtypeStruct((bsz, seq, d), f32),
                   jax.ShapeDtypeStruct((bsz, seq, d), bf16)],
        compiler_params=pltpu.CompilerParams(
            dimension_semantics=("arbitrary", "arbitrary"), vmem_limit_bytes=VMEM_LIMIT),
        name="merge_outproj",
    )(oa, ob, x, g1, wg, bg, wa, wb, wo, g2)


HALO = 8
assert HALO >= CONV_W - 1


def _mlp_kernel(h2_ref, x1_ref, wup_ref, cw_ref, cb_ref, wd_ref, o_ref, u_scr, *, tm, tiles_per_seq):
    i = pl.program_id(0)
    dff = wd_ref.shape[0]

    @pl.when(i % tiles_per_seq == 0)
    def _():
        u_scr[:HALO, :] = jnp.zeros((HALO, dff), f32)

    h2 = h2_ref[...]
    u = jnp.dot(h2, wup_ref[:, :dff], preferred_element_type=f32)
    g = jnp.dot(h2, wup_ref[:, dff:], preferred_element_type=f32)
    u_scr[HALO:, :] = u
    y = cb_ref[...]
    for j in range(CONV_W):
        y = y + u_scr[pl.ds(HALO - (CONV_W - 1) + j, tm), :] * cw_ref[j:j + 1, :]
    u_scr[:HALO, :] = u[tm - HALO:, :]
    act = (jax.nn.gelu(y) * g).astype(bf16)
    o_ref[...] = x1_ref[...] + jnp.dot(act, wd_ref[...], preferred_element_type=f32)


def _mlp(h2, x1, w_up_bf, conv_w, conv_b, w_down_bf, *, seq, tm):
    n, d = h2.shape
    dff = w_down_bf.shape[0]
    const = lambda i: (0, 0)
    return pl.pallas_call(
        functools.partial(_mlp_kernel, tm=tm, tiles_per_seq=seq // tm),
        grid=(n // tm,),
        in_specs=[
            pl.BlockSpec((tm, d), lambda i: (i, 0)),
            pl.BlockSpec((tm, d), lambda i: (i, 0)),
            pl.BlockSpec(w_up_bf.shape, const, pipeline_mode=pl.Buffered(1)),
            pl.BlockSpec(conv_w.shape, const),
            pl.BlockSpec(conv_b.shape, const),
            pl.BlockSpec(w_down_bf.shape, const, pipeline_mode=pl.Buffered(1)),
        ],
        out_specs=pl.BlockSpec((tm, d), lambda i: (i, 0)),
        out_shape=jax.ShapeDtypeStruct((n, d), f32),
        scratch_shapes=[pltpu.VMEM((tm + HALO, dff), f32)],
        compiler_params=pltpu.CompilerParams(
            dimension_semantics=("arbitrary",), vmem_limit_bytes=VMEM_LIMIT),
        name="conv_mlp",
    )(h2, x1, w_up_bf, conv_w, conv_b, w_down_bf)


def _rope_tables(seq):
    inv = 1.0 / (ROPE_THETA ** (jnp.arange(0, HEAD_DIM, 2, dtype=f32) / HEAD_DIM))
    ang = jnp.arange(seq, dtype=f32)[:, None] * inv[None, :]
    cos, sin = jnp.cos(ang), jnp.sin(ang)
    reps = LANES // HEAD_DIM
    return (jnp.tile(cos, (1, 2 * reps)), jnp.tile(jnp.concatenate([-sin, sin], axis=1), (1, reps)))


def kernel(x, norm1_g, w_in, b_gate, qn_a, kn_a, lam_q1, lam_k1, lam_q2, lam_k2, subln_g, qn_b, kn_b,
           w_a_proj, w_b_proj, w_out, norm2_g, w_up, conv_w, conv_b, w_down):
    bsz, seq, d = x.shape
    depth = w_in.shape[0]
    assert seq % 512 == 0 and d % LANES == 0
    cos_t, sin_t = _rope_tables(seq)
    seg = jnp.arange(LANES) // HEAD_DIM
    bd = (seg[:, None] == seg[None, :]).astype(bf16)
    reps = LANES // HEAD_DIM
    qscale = HEAD_DIM ** -0.5 * LOG2E
    for l in range(depth):
        lam_init = 0.8 - 0.6 * math.exp(-0.3 * l)
        gains = jnp.stack([jnp.tile(qn_a[l], reps) * qscale, jnp.tile(kn_a[l], reps),
                           jnp.tile(qn_b[l], reps) * qscale, jnp.tile(kn_b[l], reps)])
        w_in_bf = w_in[l].astype(bf16)
        q_all, k_all, vt_all, kmean = _inproj(
            x, norm1_g[l][None], w_in_bf, gains, cos_t, sin_t, bd, tm=1024)
        kmean = kmean.reshape(bsz, seq // BLK, B_WIDTH)
        lamvec = jnp.stack([lam_q1[l], lam_k1[l], lam_q2[l], lam_k2[l]])
        gmax = jnp.max(jnp.abs(gains), axis=1) * (HEAD_DIM ** 0.5)
        subg_cols = jnp.tile(subln_g[l][:, None], (1, BLK))
        oa = _attention(q_all, k_all, vt_all, (lamvec, subg_cols),
                        gmax[0] * gmax[1] * BOUND_SLACK, moba=False, lam_init=lam_init)
        ob = _attention(q_all, k_all, vt_all, (kmean,),
                        gmax[2] * gmax[3] * BOUND_SLACK, moba=True, lam_init=lam_init)
        x1, h2 = _merge(oa, ob, x, norm1_g[l][None], w_in_bf, b_gate[l][None],
                        w_a_proj[l].astype(bf16), w_b_proj[l].astype(bf16),
                        w_out[l].astype(bf16), norm2_g[l][None], tm=1024)
        dff = w_down.shape[1]
        x = _mlp(h2.reshape(bsz * seq, d), x1.reshape(bsz * seq, d), w_up[l].astype(bf16), conv_w[l],
                 conv_b[l][None], w_down[l].astype(bf16), seq=seq, tm=512).reshape(bsz, seq, d)
    return x
```

```python
import functools
import math

import jax
import jax.numpy as jnp
from jax import lax
from jax.experimental import pallas as pl
from jax.experimental.pallas import tpu as pltpu

HEAD_DIM = 64
A_HEADS = 4
A_VDIM = 2 * HEAD_DIM
A_WIDTH = A_HEADS * A_VDIM
B_HEADS = 8
B_WIDTH = B_HEADS * HEAD_DIM
B_BLOCK = 256
B_TOPK = 3
CONV_W = 3
ROPE_THETA = 10000.0
EPS = 1e-6

LANES = 128
BLK = B_BLOCK
PAIRS = A_WIDTH // LANES
LOG2E = 1.4426950408889634
NEG_BIG = -1e30
POS_BIG = 1e30
VMEM_LIMIT = 56 * 1024 * 1024

f32 = jnp.float32
bf16 = jnp.bfloat16


def _nt_dot(a, b, **kw):
    return lax.dot_general(a, b, (((1,), (1,)), ((), ())), preferred_element_type=f32, **kw)


def _inproj_kernel(x_ref, g1_ref, w_ref, gain_ref, cos_ref, sin_ref, bd_ref,
                   q_ref, k_ref, vt_ref, kmean_ref, h_scr, *, tm):
    nblk = tm // BLK
    x = x_ref[0]
    ms = jnp.mean(x * x, axis=-1, keepdims=True)
    h_scr[...] = (x * lax.rsqrt(ms + EPS) * g1_ref[...]).astype(bf16)
    cos = cos_ref[...]
    sin = sin_ref[...]
    bd = bd_ref[...]
    lane = lax.broadcasted_iota(jnp.int32, (tm, LANES), 1)
    first_half = (lane & (HEAD_DIM - 1)) < (HEAD_DIM // 2)

    def col_tile(j):
        return jnp.dot(h_scr[...], w_ref[:, j * A_WIDTH:(j + 1) * A_WIDTH],
                       preferred_element_type=f32)

    def norm_rope(y, gain):
        ss = jnp.dot((y * y).astype(bf16), bd, preferred_element_type=f32)
        yn = y * lax.rsqrt(ss * (1.0 / HEAD_DIM) + EPS) * gain
        rot = jnp.where(first_half, pltpu.roll(yn, LANES - HEAD_DIM // 2, 1),
                        pltpu.roll(yn, HEAD_DIM // 2, 1))
        return yn * cos + rot * sin

    for branch in range(2):
        acc_q = col_tile(3 * branch)
        for c in range(PAIRS):
            out = norm_rope(acc_q[:, c * LANES:(c + 1) * LANES], gain_ref[2 * branch:2 * branch + 1, :])
            for bi in range(nblk):
                q_ref[0, branch * PAIRS + c, bi] = out[bi * BLK:(bi + 1) * BLK].astype(bf16)
        acc_k = col_tile(3 * branch + 1)
        for c in range(PAIRS):
            out = norm_rope(acc_k[:, c * LANES:(c + 1) * LANES],
                            gain_ref[2 * branch + 1:2 * branch + 2, :])
            for bi in range(nblk):
                blk = out[bi * BLK:(bi + 1) * BLK]
                k_ref[0, branch * PAIRS + c, bi] = blk.astype(bf16)
                if branch == 1:
                    kmean_ref[0, bi:bi + 1, c * LANES:(c + 1) * LANES] = jnp.mean(
                        blk, axis=0, keepdims=True)
        acc_v = col_tile(3 * branch + 2)
        for c in range(PAIRS):
            for bi in range(nblk):
                blk = acc_v[bi * BLK:(bi + 1) * BLK, c * LANES:(c + 1) * LANES]
                vt_ref[0, branch * PAIRS + c, bi] = blk.T.astype(bf16)


def _inproj(x, g1, w_in_bf, gains, cos_t, sin_t, bd, *, tm):
    bsz, seq, d = x.shape
    n_cols = w_in_bf.shape[1]
    nblk = tm // BLK
    spt = seq // tm
    nb = seq // BLK
    qk_shape = jax.ShapeDtypeStruct((bsz, 2 * PAIRS, nb, BLK, LANES), bf16)
    vt_shape = jax.ShapeDtypeStruct((bsz, 2 * PAIRS, nb, LANES, BLK), bf16)
    qk_spec = pl.BlockSpec((1, 2 * PAIRS, nblk, BLK, LANES), lambda b, i: (b, 0, i, 0, 0))
    vt_spec = pl.BlockSpec((1, 2 * PAIRS, nblk, LANES, BLK), lambda b, i: (b, 0, i, 0, 0))
    return pl.pallas_call(
        functools.partial(_inproj_kernel, tm=tm),
        grid=(bsz, spt),
        in_specs=[
            pl.BlockSpec((1, tm, d), lambda b, i: (b, i, 0)),
            pl.BlockSpec((1, d), lambda b, i: (0, 0)),
            pl.BlockSpec((d, n_cols), lambda b, i: (0, 0), pipeline_mode=pl.Buffered(1)),
            pl.BlockSpec((4, LANES), lambda b, i: (0, 0)),
            pl.BlockSpec((tm, LANES), lambda b, i: (i, 0)),
            pl.BlockSpec((tm, LANES), lambda b, i: (i, 0)),
            pl.BlockSpec((LANES, LANES), lambda b, i: (0, 0)),
        ],
        out_specs=[
            qk_spec, qk_spec, vt_spec,
            pl.BlockSpec((1, nblk, B_WIDTH), lambda b, i: (b * spt + i, 0, 0)),
        ],
        out_shape=[
            qk_shape, qk_shape, vt_shape,
            jax.ShapeDtypeStruct((bsz * spt, nblk, B_WIDTH), f32),
        ],
        scratch_shapes=[pltpu.VMEM((tm, d), bf16)],
        compiler_params=pltpu.CompilerParams(
            dimension_semantics=("arbitrary", "arbitrary"), vmem_limit_bytes=VMEM_LIMIT),
        name="inproj",
    )(x, g1, w_in_bf, gains, cos_t, sin_t, bd)


TRIP_STEPS = 34
QK_AHEAD = {False: 6, True: 34}
TILE_UNROLL = 4
BOUND_SLACK = 1.0 + 2.0 ** -6
MAX_FIXED_BOUND = 60.0


def _attn_kernel(itab_ref, jtab_ref, otab_ref, *refs, moba, online, lam_init, n_steps, n_masked):
    if moba:
        (q_ref, k_ref, vt_ref, mref_ref, kmean_ref, o_ref,
         qs_scr, m_scr, l_scr, acc_scr, s_scr, sel_scr) = refs
    else:
        (q_ref, k_ref, vt_ref, mref_ref, lam_ref, subg_ref, o_ref,
         qs_scr, m_scr, l_scr, acc_scr, s_scr) = refs
    nb = k_ref.shape[2]
    acc_rows = acc_scr.shape[1]
    lane = lax.broadcasted_iota(jnp.int32, (BLK, LANES), 1)
    key_minus_row = (lax.broadcasted_iota(jnp.int32, (BLK, 2 * BLK), 0)
                     - (lax.broadcasted_iota(jnp.int32, (BLK, 2 * BLK), 1) & (BLK - 1)))

    if moba:
        km = kmean_ref[0]
        km_parts = []
        for _ in range(3):
            part = km.astype(bf16)
            km_parts.append(part)
            km = km - part.astype(f32)

    nidx = lax.broadcasted_iota(jnp.int32, (nb, 2 * BLK), 0)
    for i in range(nb):
        q = q_ref[0, 0, i].astype(f32)
        qs = jnp.concatenate([jnp.where(lane < HEAD_DIM, q, 0.0),
                              jnp.where(lane >= HEAD_DIM, q, 0.0)], axis=0).astype(bf16)
        qs_scr[i] = qs
        if online:
            m_scr[i] = jnp.full((1, 2 * BLK), NEG_BIG, f32)
        l_scr[i] = jnp.zeros((1, 2 * BLK), f32)
        acc_scr[i] = jnp.zeros((acc_rows, 2 * BLK), f32)
        if moba:
            keep = nidx <= i
            if i > B_TOPK:
                g = _nt_dot(km_parts[0], qs) + _nt_dot(km_parts[1], qs) + _nt_dot(km_parts[2], qs)
                cnt = jnp.zeros((nb, 2 * BLK), f32)
                for mm in range(i):
                    row = g[mm:mm + 1, :]
                    beats = (row > g) | ((row == g) & (nidx > mm))
                    cnt = cnt + jnp.where(beats, 1.0, 0.0)
                keep = ((cnt < float(B_TOPK)) & (nidx < i)) | (nidx == i)
            sel_scr[i] = jnp.where(keep, 1.0, 0.0)

    def qk(step, slot, u):
        s = _nt_dot(k_ref[0, 0, jtab_ref[step]], qs_scr[itab_ref[step]])
        if slot == 0 and u < n_masked:
            s = jnp.where(key_minus_row <= otab_ref[step], s, NEG_BIG)
        s_scr[slot, u] = s

    def update(step, slot, u):
        i = itab_ref[step]
        j = jtab_ref[step]
        s = s_scr[slot, u]
        if moba:
            chosen = sel_scr[i, pl.ds(j, 1), :] > 0.5
        if online:
            m_old = m_scr[i]
            m_new = jnp.maximum(m_old, jnp.max(s, axis=0, keepdims=True))
            if moba:
                m_new = jnp.where(chosen, m_new, m_old)
            m_sub = m_new
            alpha = jnp.exp2(m_old - m_new)
            m_scr[i] = m_new
        else:
            m_sub = mref_ref[...]
        if moba:
            m_sub = jnp.where(chosen, m_sub, POS_BIG)
        p = jnp.exp2(s - m_sub)
        psum = jnp.sum(p, axis=0, keepdims=True)
        pb = p.astype(bf16)
        vt = vt_ref[0, 0, j]
        if moba:
            pv = jnp.concatenate(
                [jnp.dot(vt[:HEAD_DIM], pb[:, :BLK], preferred_element_type=f32),
                 jnp.dot(vt[HEAD_DIM:], pb[:, BLK:], preferred_element_type=f32)], axis=1)
        else:
            pv = jnp.dot(vt, pb, preferred_element_type=f32)
        if online:
            l_scr[i] = alpha * l_scr[i] + psum
            acc_scr[i] = alpha * acc_scr[i] + pv
        else:
            l_scr[i] = l_scr[i] + psum
            acc_scr[i] = acc_scr[i] + pv

    ahead = min(QK_AHEAD[moba], TRIP_STEPS)

    def trip(it, carry):
        base = it * TRIP_STEPS
        for u in range(ahead):
            qk(base + u, 0, u)
        for u in range(TRIP_STEPS):
            if u + ahead < TRIP_STEPS:
                qk(base + u + ahead, 0, u + ahead)
            update(base + u, 0, u)
        return carry

    lax.fori_loop(0, n_steps // TRIP_STEPS, trip, 0)

    if not moba:
        lv = lam_ref[...]
        lam = (jnp.exp(jnp.sum(lv[0:1] * lv[1:2], axis=-1, keepdims=True))
               - jnp.exp(jnp.sum(lv[2:3] * lv[3:4], axis=-1, keepdims=True)) + lam_init)

    def finish_tiles(t, carry):
        for r in range(TILE_UNROLL):
            i = t * TILE_UNROLL + r
            o = acc_scr[i] * (1.0 / l_scr[i])
            rows = pl.ds(pl.multiple_of(i * BLK, BLK), BLK)
            if moba:
                ot = jnp.concatenate([o[:, :BLK], o[:, BLK:]], axis=0).T
                o_ref[0, rows, :] = ot.astype(bf16)
            else:
                od = o[:, :BLK] - lam * o[:, BLK:]
                ms = jnp.mean(od * od, axis=0, keepdims=True)
                od = od * lax.rsqrt(ms + EPS) * subg_ref[...] * (1.0 - lam_init)
                o_ref[0, rows, :] = od.T.astype(bf16)
        return carry

    lax.fori_loop(0, nb // TILE_UNROLL, finish_tiles, 0)


def _step_tables(nb):
    per_trip = TRIP_STEPS
    diag = [(i, i, 0) for i in range(nb)]
    rest = [(i, j, BLK) for j in range(nb) for i in range(j + 1, nb)]
    n_steps = len(diag) + len(rest)
    assert n_steps % per_trip == 0
    n_trips = n_steps // per_trip
    n_masked = -(-len(diag) // n_trips)
    assert n_masked <= per_trip
    steps = []
    for _ in range(n_trips):
        for _ in range(n_masked):
            steps.append(diag.pop(0) if diag else rest.pop(0))
        for _ in range(per_trip - n_masked):
            steps.append(rest.pop(0))
    assert not diag and not rest
    return [jnp.asarray([s[c] for s in steps], jnp.int32) for c in range(3)], n_steps, n_masked


def _attention(q_all, k_all, vt_all, aux, score_bound, *, moba, lam_init):
    bsz, _, nb, _, _ = q_all.shape
    off = PAIRS if moba else 0
    assert nb % TILE_UNROLL == 0
    (itab, jtab, otab), n_steps, n_masked = _step_tables(nb)
    const = lambda b, h, it, jt, ot: (0, 0)
    in_specs = [
        pl.BlockSpec((1, 1, nb, BLK, LANES), lambda b, h, it, jt, ot: (b, h + off, 0, 0, 0)),
        pl.BlockSpec((1, 1, nb, BLK, LANES), lambda b, h, it, jt, ot: (b, h + off, 0, 0, 0)),
        pl.BlockSpec((1, 1, nb, LANES, BLK), lambda b, h, it, jt, ot: (b, h + off, 0, 0, 0)),
        pl.BlockSpec((1, 2 * BLK), const),
    ]
    scratch = [pltpu.VMEM((nb, 2 * BLK, LANES), bf16),
               pltpu.VMEM((nb, 1, 2 * BLK), f32), pltpu.VMEM((nb, 1, 2 * BLK), f32),
               pltpu.VMEM((nb, HEAD_DIM if moba else LANES, 2 * BLK), f32),
               pltpu.VMEM((1, TRIP_STEPS, BLK, 2 * BLK), f32)]
    if moba:
        (kmean,) = aux
        in_specs.append(pl.BlockSpec((1, nb, LANES), lambda b, h, it, jt, ot: (b, 0, h)))
        scratch.append(pltpu.VMEM((nb, nb, 2 * BLK), f32))
    else:
        lamvec, subg = aux
        in_specs.append(pl.BlockSpec(lamvec.shape, const))
        in_specs.append(pl.BlockSpec(subg.shape, const))
    mref = jnp.full((1, 2 * BLK), score_bound, f32)

    def run(online):
        return pl.pallas_call(
            functools.partial(_attn_kernel, moba=moba, online=online, lam_init=lam_init,
                              n_steps=n_steps, n_masked=n_masked),
            grid_spec=pltpu.PrefetchScalarGridSpec(
                num_scalar_prefetch=3,
                grid=(bsz, PAIRS),
                in_specs=in_specs,
                out_specs=pl.BlockSpec((1, nb * BLK, LANES), lambda b, h, it, jt, ot: (b, 0, h)),
                scratch_shapes=scratch),
            out_shape=jax.ShapeDtypeStruct((bsz, nb * BLK, PAIRS * LANES), bf16),
            compiler_params=pltpu.CompilerParams(
                dimension_semantics=("arbitrary", "arbitrary"), vmem_limit_bytes=VMEM_LIMIT),
            name=("moba_attn" if moba else "diff_attn") + ("_online" if online else ""),
        )(itab, jtab, otab, q_all, k_all, vt_all, mref, *aux)

    return lax.cond(score_bound <= MAX_FIXED_BOUND, lambda: run(False), lambda: run(True))


def _merge_kernel(oa_ref, ob_ref, x_ref, g1_ref, wg_ref, bg_ref, wa_ref, wb_ref, wo_ref, g2_ref,
                  x1_ref, h2_ref):
    d = x_ref.shape[-1]
    x = x_ref[0]
    ms = jnp.mean(x * x, axis=-1, keepdims=True)
    h = (x * lax.rsqrt(ms + EPS) * g1_ref[...]).astype(bf16)
    n_qkv = wg_ref.shape[1] - 2 * d
    gates = jax.nn.sigmoid(jnp.dot(h, wg_ref[:, n_qkv:], preferred_element_type=f32) + bg_ref[...])
    merged = (gates[:, :d] * jnp.dot(oa_ref[0], wa_ref[...], preferred_element_type=f32)
              + gates[:, d:] * jnp.dot(ob_ref[0], wb_ref[...], preferred_element_type=f32))
    x1 = x + jnp.dot(merged.astype(bf16), wo_ref[...], preferred_element_type=f32)
    x1_ref[0] = x1
    ms = jnp.mean(x1 * x1, axis=-1, keepdims=True)
    h2_ref[0] = (x1 * lax.rsqrt(ms + EPS) * g2_ref[...]).astype(bf16)


def _merge(oa, ob, x, g1, wg, bg, wa, wb, wo, g2, *, tm):
    bsz, seq, d = x.shape
    const = lambda b, i: (0, 0)
    tile = lambda b, i: (b, i, 0)
    return pl.pallas_call(
        _merge_kernel,
        grid=(bsz, seq // tm),
        in_specs=[
            pl.BlockSpec((1, tm, oa.shape[-1]), tile),
            pl.BlockSpec((1, tm, ob.shape[-1]), tile),
            pl.BlockSpec((1, tm, d), tile),
            pl.BlockSpec((1, d), const),
            pl.BlockSpec(wg.shape, const, pipeline_mode=pl.Buffered(1)),
            pl.BlockSpec(bg.shape, const),
            pl.BlockSpec(wa.shape, const),
            pl.BlockSpec(wb.shape, const),
            pl.BlockSpec(wo.shape, const),
            pl.BlockSpec((1, d), const),
        ],
        out_specs=[pl.BlockSpec((1, tm, d), tile), pl.BlockSpec((1, tm, d), tile)],
        out_shape=[jax.ShapeDtypeStruct((bsz, seq, d), f32),
                   jax.ShapeDtypeStruct((bsz, seq, d), bf16)],
        compiler_params=pltpu.CompilerParams(
            dimension_semantics=("arbitrary", "arbitrary"), vmem_limit_bytes=VMEM_LIMIT),
        name="merge_outproj",
    )(oa, ob, x, g1, wg, bg, wa, wb, wo, g2)


HALO = 8
assert HALO >= CONV_W - 1


def _mlp_kernel(h2_ref, x1_ref, wup_ref, cw_ref, cb_ref, wd_ref, o_ref, u_scr, *, tm, tiles_per_seq):
    i = pl.program_id(0)
    dff = wd_ref.shape[0]

    @pl.when(i % tiles_per_seq == 0)
    def _():
        u_scr[:HALO, :] = jnp.zeros((HALO, dff), f32)

    h2 = h2_ref[...]
    u = jnp.dot(h2, wup_ref[:, :dff], preferred_element_type=f32)
    g = jnp.dot(h2, wup_ref[:, dff:], preferred_element_type=f32)
    u_scr[HALO:, :] = u
    y = cb_ref[...]
    for j in range(CONV_W):
        y = y + u_scr[pl.ds(HALO - (CONV_W - 1) + j, tm), :] * cw_ref[j:j + 1, :]
    u_scr[:HALO, :] = u[tm - HALO:, :]
    act = (jax.nn.gelu(y) * g).astype(bf16)
    o_ref[...] = x1_ref[...] + jnp.dot(act, wd_ref[...], preferred_element_type=f32)


def _mlp(h2, x1, w_up_bf, conv_w, conv_b, w_down_bf, *, seq, tm):
    n, d = h2.shape
    dff = w_down_bf.shape[0]
    const = lambda i: (0, 0)
    return pl.pallas_call(
        functools.partial(_mlp_kernel, tm=tm, tiles_per_seq=seq // tm),
        grid=(n // tm,),
        in_specs=[
            pl.BlockSpec((tm, d), lambda i: (i, 0)),
            pl.BlockSpec((tm, d), lambda i: (i, 0)),
            pl.BlockSpec(w_up_bf.shape, const, pipeline_mode=pl.Buffered(1)),
            pl.BlockSpec(conv_w.shape, const),
            pl.BlockSpec(conv_b.shape, const),
            pl.BlockSpec(w_down_bf.shape, const, pipeline_mode=pl.Buffered(1)),
        ],
        out_specs=pl.BlockSpec((tm, d), lambda i: (i, 0)),
        out_shape=jax.ShapeDtypeStruct((n, d), f32),
        scratch_shapes=[pltpu.VMEM((tm + HALO, dff), f32)],
        compiler_params=pltpu.CompilerParams(
            dimension_semantics=("arbitrary",), vmem_limit_bytes=VMEM_LIMIT),
        name="conv_mlp",
    )(h2, x1, w_up_bf, conv_w, conv_b, w_down_bf)


def _rope_tables(seq):
    inv = 1.0 / (ROPE_THETA ** (jnp.arange(0, HEAD_DIM, 2, dtype=f32) / HEAD_DIM))
    ang = jnp.arange(seq, dtype=f32)[:, None] * inv[None, :]
    cos, sin = jnp.cos(ang), jnp.sin(ang)
    reps = LANES // HEAD_DIM
    return (jnp.tile(cos, (1, 2 * reps)), jnp.tile(jnp.concatenate([-sin, sin], axis=1), (1, reps)))


def kernel(x, norm1_g, w_in, b_gate, qn_a, kn_a, lam_q1, lam_k1, lam_q2, lam_k2, subln_g, qn_b, kn_b,
           w_a_proj, w_b_proj, w_out, norm2_g, w_up, conv_w, conv_b, w_down):
    bsz, seq, d = x.shape
    depth = w_in.shape[0]
    assert seq % 512 == 0 and d % LANES == 0
    cos_t, sin_t = _rope_tables(seq)
    seg = jnp.arange(LANES) // HEAD_DIM
    bd = (seg[:, None] == seg[None, :]).astype(bf16)
    reps = LANES // HEAD_DIM
    qscale = HEAD_DIM ** -0.5 * LOG2E
    for l in range(depth):
        lam_init = 0.8 - 0.6 * math.exp(-0.3 * l)
        gains = jnp.stack([jnp.tile(qn_a[l], reps) * qscale, jnp.tile(kn_a[l], reps),
                           jnp.tile(qn_b[l], reps) * qscale, jnp.tile(kn_b[l], reps)])
        w_in_bf = w_in[l].astype(bf16)
        q_all, k_all, vt_all, kmean = _inproj(
            x, norm1_g[l][None], w_in_bf, gains, cos_t, sin_t, bd, tm=1024)
        kmean = kmean.reshape(bsz, seq // BLK, B_WIDTH)
        lamvec = jnp.stack([lam_q1[l], lam_k1[l], lam_q2[l], lam_k2[l]])
        gmax = jnp.max(jnp.abs(gains), axis=1) * (HEAD_DIM ** 0.5)
        subg_cols = jnp.tile(subln_g[l][:, None], (1, BLK))
        oa = _attention(q_all, k_all, vt_all, (lamvec, subg_cols),
                        gmax[0] * gmax[1] * BOUND_SLACK, moba=False, lam_init=lam_init)
        ob = _attention(q_all, k_all, vt_all, (kmean,),
                        gmax[2] * gmax[3] * BOUND_SLACK, moba=True, lam_init=lam_init)
        x1, h2 = _merge(oa, ob, x, norm1_g[l][None], w_in_bf, b_gate[l][None],
                        w_a_proj[l].astype(bf16), w_b_proj[l].astype(bf16),
                        w_out[l].astype(bf16), norm2_g[l][None], tm=1024)
        dff = w_down.shape[1]
        x = _mlp(h2.reshape(bsz * seq, d), x1.reshape(bsz * seq, d), w_up[l].astype(bf16), conv_w[l],
                 conv_b[l][None], w_down[l].astype(bf16), seq=seq, tm=512).reshape(bsz, seq, d)
    return x
```

```python
import functools
import math

import jax
import jax.numpy as jnp
from jax import lax
from jax.experimental import pallas as pl
from jax.experimental.pallas import tpu as pltpu

HEAD_DIM = 64
A_HEADS = 4
A_VDIM = 2 * HEAD_DIM
A_WIDTH = A_HEADS * A_VDIM
B_HEADS = 8
B_WIDTH = B_HEADS * HEAD_DIM
B_BLOCK = 256
B_TOPK = 3
CONV_W = 3
ROPE_THETA = 10000.0
EPS = 1e-6

LANES = 128
BLK = B_BLOCK
PAIRS = A_WIDTH // LANES
LOG2E = 1.4426950408889634
NEG_BIG = -1e30
POS_BIG = 1e30
VMEM_LIMIT = 56 * 1024 * 1024

f32 = jnp.float32
bf16 = jnp.bfloat16


def _nt_dot(a, b, **kw):
    return lax.dot_general(a, b, (((1,), (1,)), ((), ())), preferred_element_type=f32, **kw)


def _inproj_kernel(x_ref, g1_ref, w_ref, gain_ref, cos_ref, sin_ref, bd_ref,
                   q_ref, k_ref, vt_ref, kmean_ref, h_scr, *, tm):
    nblk = tm // BLK
    x = x_ref[0]
    ms = jnp.mean(x * x, axis=-1, keepdims=True)
    h_scr[...] = (x * lax.rsqrt(ms + EPS) * g1_ref[...]).astype(bf16)
    cos = cos_ref[...]
    sin = sin_ref[...]
    bd = bd_ref[...]
    lane = lax.broadcasted_iota(jnp.int32, (tm, LANES), 1)
    first_half = (lane & (HEAD_DIM - 1)) < (HEAD_DIM // 2)

    def col_tile(j):
        return jnp.dot(h_scr[...], w_ref[:, j * A_WIDTH:(j + 1) * A_WIDTH],
                       preferred_element_type=f32)

    def norm_rope(y, gain):
        ss = jnp.dot((y * y).astype(bf16), bd, preferred_element_type=f32)
        yn = y * lax.rsqrt(ss * (1.0 / HEAD_DIM) + EPS) * gain
        rot = jnp.where(first_half, pltpu.roll(yn, LANES - HEAD_DIM // 2, 1),
                        pltpu.roll(yn, HEAD_DIM // 2, 1))
        return yn * cos + rot * sin

    for branch in range(2):
        acc_q = col_tile(3 * branch)
        for c in range(PAIRS):
            out = norm_rope(acc_q[:, c * LANES:(c + 1) * LANES], gain_ref[2 * branch:2 * branch + 1, :])
            for bi in range(nblk):
                q_ref[0, branch * PAIRS + c, bi] = out[bi * BLK:(bi + 1) * BLK].astype(bf16)
        acc_k = col_tile(3 * branch + 1)
        for c in range(PAIRS):
            out = norm_rope(acc_k[:, c * LANES:(c + 1) * LANES],
                            gain_ref[2 * branch + 1:2 * branch + 2, :])
            for bi in range(nblk):
                blk = out[bi * BLK:(bi + 1) * BLK]
                k_ref[0, branch * PAIRS + c, bi] = blk.astype(bf16)
                if branch == 1:
                    kmean_ref[0, bi:bi + 1, c * LANES:(c + 1) * LANES] = jnp.mean(
                        blk, axis=0, keepdims=True)
        acc_v = col_tile(3 * branch + 2)
        for c in range(PAIRS):
            for bi in range(nblk):
                blk = acc_v[bi * BLK:(bi + 1) * BLK, c * LANES:(c + 1) * LANES]
                vt_ref[0, branch * PAIRS + c, bi] = blk.T.astype(bf16)


def _inproj(x, g1, w_in_bf, gains, cos_t, sin_t, bd, *, tm):
    bsz, seq, d = x.shape
    n_cols = w_in_bf.shape[1]
    nblk = tm // BLK
    spt = seq // tm
    nb = seq // BLK
    qk_shape = jax.ShapeDtypeStruct((bsz, 2 * PAIRS, nb, BLK, LANES), bf16)
    vt_shape = jax.ShapeDtypeStruct((bsz, 2 * PAIRS, nb, LANES, BLK), bf16)
    qk_spec = pl.BlockSpec((1, 2 * PAIRS, nblk, BLK, LANES), lambda b, i: (b, 0, i, 0, 0))
    vt_spec = pl.BlockSpec((1, 2 * PAIRS, nblk, LANES, BLK), lambda b, i: (b, 0, i, 0, 0))
    return pl.pallas_call(
        functools.partial(_inproj_kernel, tm=tm),
        grid=(bsz, spt),
        in_specs=[
            pl.BlockSpec((1, tm, d), lambda b, i: (b, i, 0)),
            pl.BlockSpec((1, d), lambda b, i: (0, 0)),
            pl.BlockSpec((d, n_cols), lambda b, i: (0, 0), pipeline_mode=pl.Buffered(1)),
            pl.BlockSpec((4, LANES), lambda b, i: (0, 0)),
            pl.BlockSpec((tm, LANES), lambda b, i: (i, 0)),
            pl.BlockSpec((tm, LANES), lambda b, i: (i, 0)),
            pl.BlockSpec((LANES, LANES), lambda b, i: (0, 0)),
        ],
        out_specs=[
            qk_spec, qk_spec, vt_spec,
            pl.BlockSpec((1, nblk, B_WIDTH), lambda b, i: (b * spt + i, 0, 0)),
        ],
        out_shape=[
            qk_shape, qk_shape, vt_shape,
            jax.ShapeDtypeStruct((bsz * spt, nblk, B_WIDTH), f32),
        ],
        scratch_shapes=[pltpu.VMEM((tm, d), bf16)],
        compiler_params=pltpu.CompilerParams(
            dimension_semantics=("arbitrary", "arbitrary"), vmem_limit_bytes=VMEM_LIMIT),
        name="inproj",
    )(x, g1, w_in_bf, gains, cos_t, sin_t, bd)


TRIP_STEPS = 34
QK_AHEAD = {False: 6, True: 34}
TILE_UNROLL = 4
BOUND_SLACK = 1.0 + 2.0 ** -6
MAX_FIXED_BOUND = 60.0


def _attn_kernel(itab_ref, jtab_ref, otab_ref, *refs, moba, online, lam_init, n_steps, n_masked):
    if moba:
        (q_ref, k_ref, vt_ref, mref_ref, kmean_ref, o_ref,
         qs_scr, m_scr, l_scr, acc_scr, s_scr, sel_scr) = refs
    else:
        (q_ref, k_ref, vt_ref, mref_ref, lam_ref, subg_ref, o_ref,
         qs_scr, m_scr, l_scr, acc_scr, s_scr) = refs
    nb = k_ref.shape[2]
    acc_rows = acc_scr.shape[1]
    lane = lax.broadcasted_iota(jnp.int32, (BLK, LANES), 1)
    key_minus_row = (lax.broadcasted_iota(jnp.int32, (BLK, 2 * BLK), 0)
                     - (lax.broadcasted_iota(jnp.int32, (BLK, 2 * BLK), 1) & (BLK - 1)))

    if moba:
        km = kmean_ref[0]
        km_parts = []
        for _ in range(3):
            part = km.astype(bf16)
            km_parts.append(part)
            km = km - part.astype(f32)

    nidx = lax.broadcasted_iota(jnp.int32, (nb, 2 * BLK), 0)
    for i in range(nb):
        q = q_ref[0, 0, i].astype(f32)
        qs = jnp.concatenate([jnp.where(lane < HEAD_DIM, q, 0.0),
                              jnp.where(lane >= HEAD_DIM, q, 0.0)], axis=0).astype(bf16)
        qs_scr[i] = qs
        if online:
            m_scr[i] = jnp.full((1, 2 * BLK), NEG_BIG, f32)
        l_scr[i] = jnp.zeros((1, 2 * BLK), f32)
        acc_scr[i] = jnp.zeros((acc_rows, 2 * BLK), f32)
        if moba:
            keep = nidx <= i
            if i > B_TOPK:
                g = _nt_dot(km_parts[0], qs) + _nt_dot(km_parts[1], qs) + _nt_dot(km_parts[2], qs)
                cnt = jnp.zeros((nb, 2 * BLK), f32)
                for mm in range(i):
                    row = g[mm:mm + 1, :]
                    beats = (row > g) | ((row == g) & (nidx > mm))
                    cnt = cnt + jnp.where(beats, 1.0, 0.0)
                keep = ((cnt < float(B_TOPK)) & (nidx < i)) | (nidx == i)
            sel_scr[i] = jnp.where(keep, 1.0, 0.0)

    def qk(step, slot, u):
        s = _nt_dot(k_ref[0, 0, jtab_ref[step]], qs_scr[itab_ref[step]])
        if slot == 0 and u < n_masked:
            s = jnp.where(key_minus_row <= otab_ref[step], s, NEG_BIG)
        s_scr[slot, u] = s

    def update(step, slot, u):
        i = itab_ref[step]
        j = jtab_ref[step]
        s = s_scr[slot, u]
        if moba:
            chosen = sel_scr[i, pl.ds(j, 1), :] > 0.5
        if online:
            m_old = m_scr[i]
            m_new = jnp.maximum(m_old, jnp.max(s, axis=0, keepdims=True))
            if moba:
                m_new = jnp.where(chosen, m_new, m_old)
            m_sub = m_new
            alpha = jnp.exp2(m_old - m_new)
            m_scr[i] = m_new
        else:
            m_sub = mref_ref[...]
        if moba:
            m_sub = jnp.where(chosen, m_sub, POS_BIG)
        p = jnp.exp2(s - m_sub)
        psum = jnp.sum(p, axis=0, keepdims=True)
        pb = p.astype(bf16)
        vt = vt_ref[0, 0, j]
        pv = jnp.dot(vt, pb, preferred_element_type=f32)
        if online:
            l_scr[i] = alpha * l_scr[i] + psum
            acc_scr[i] = alpha * acc_scr[i] + pv
        else:
            l_scr[i] = l_scr[i] + psum
            acc_scr[i] = acc_scr[i] + pv

    ahead = min(QK_AHEAD[moba], TRIP_STEPS)

    def trip(it, carry):
        base = it * TRIP_STEPS
        for u in range(ahead):
            qk(base + u, 0, u)
        for u in range(TRIP_STEPS):
            if u + ahead < TRIP_STEPS:
                qk(base + u + ahead, 0, u + ahead)
            update(base + u, 0, u)
        return carry

    lax.fori_loop(0, n_steps // TRIP_STEPS, trip, 0)

    if not moba:
        lv = lam_ref[...]
        lam = (jnp.exp(jnp.sum(lv[0:1] * lv[1:2], axis=-1, keepdims=True))
               - jnp.exp(jnp.sum(lv[2:3] * lv[3:4], axis=-1, keepdims=True)) + lam_init)

    def finish_tiles(t, carry):
        for r in range(TILE_UNROLL):
            i = t * TILE_UNROLL + r
            o = acc_scr[i] * (1.0 / l_scr[i])
            rows = pl.ds(pl.multiple_of(i * BLK, BLK), BLK)
            if moba:
                ot = jnp.concatenate([o[:HEAD_DIM, :BLK], o[HEAD_DIM:, BLK:]], axis=0).T
                o_ref[0, rows, :] = ot.astype(bf16)
            else:
                od = o[:, :BLK] - lam * o[:, BLK:]
                ms = jnp.mean(od * od, axis=0, keepdims=True)
                od = od * lax.rsqrt(ms + EPS) * subg_ref[...] * (1.0 - lam_init)
                o_ref[0, rows, :] = od.T.astype(bf16)
        return carry

    lax.fori_loop(0, nb // TILE_UNROLL, finish_tiles, 0)


def _step_tables(nb):
    per_trip = TRIP_STEPS
    diag = [(i, i, 0) for i in range(nb)]
    rest = [(i, j, BLK) for j in range(nb) for i in range(j + 1, nb)]
    n_steps = len(diag) + len(rest)
    assert n_steps % per_trip == 0
    n_trips = n_steps // per_trip
    n_masked = -(-len(diag) // n_trips)
    assert n_masked <= per_trip
    steps = []
    for _ in range(n_trips):
        for _ in range(n_masked):
            steps.append(diag.pop(0) if diag else rest.pop(0))
        for _ in range(per_trip - n_masked):
            steps.append(rest.pop(0))
    assert not diag and not rest
    return [jnp.asarray([s[c] for s in steps], jnp.int32) for c in range(3)], n_steps, n_masked


def _attention(q_all, k_all, vt_all, aux, score_bound, *, moba, lam_init):
    bsz, _, nb, _, _ = q_all.shape
    off = PAIRS if moba else 0
    assert nb % TILE_UNROLL == 0
    (itab, jtab, otab), n_steps, n_masked = _step_tables(nb)
    const = lambda b, h, it, jt, ot: (0, 0)
    in_specs = [
        pl.BlockSpec((1, 1, nb, BLK, LANES), lambda b, h, it, jt, ot: (b, h + off, 0, 0, 0)),
        pl.BlockSpec((1, 1, nb, BLK, LANES), lambda b, h, it, jt, ot: (b, h + off, 0, 0, 0)),
        pl.BlockSpec((1, 1, nb, LANES, BLK), lambda b, h, it, jt, ot: (b, h + off, 0, 0, 0)),
        pl.BlockSpec((1, 2 * BLK), const),
    ]
    scratch = [pltpu.VMEM((nb, 2 * BLK, LANES), bf16),
               pltpu.VMEM((nb, 1, 2 * BLK), f32), pltpu.VMEM((nb, 1, 2 * BLK), f32),
               pltpu.VMEM((nb, LANES, 2 * BLK), f32),
               pltpu.VMEM((1, TRIP_STEPS, BLK, 2 * BLK), f32)]
    if moba:
        (kmean,) = aux
        in_specs.append(pl.BlockSpec((1, nb, LANES), lambda b, h, it, jt, ot: (b, 0, h)))
        scratch.append(pltpu.VMEM((nb, nb, 2 * BLK), f32))
    else:
        lamvec, subg = aux
        in_specs.append(pl.BlockSpec(lamvec.shape, const))
        in_specs.append(pl.BlockSpec(subg.shape, const))
    mref = jnp.full((1, 2 * BLK), score_bound, f32)

    def run(online):
        return pl.pallas_call(
            functools.partial(_attn_kernel, moba=moba, online=online, lam_init=lam_init,
                              n_steps=n_steps, n_masked=n_masked),
            grid_spec=pltpu.PrefetchScalarGridSpec(
                num_scalar_prefetch=3,
                grid=(bsz, PAIRS),
                in_specs=in_specs,
                out_specs=pl.BlockSpec((1, nb * BLK, LANES), lambda b, h, it, jt, ot: (b, 0, h)),
                scratch_shapes=scratch),
            out_shape=jax.ShapeDtypeStruct((bsz, nb * BLK, PAIRS * LANES), bf16),
            compiler_params=pltpu.CompilerParams(
                dimension_semantics=("arbitrary", "arbitrary"), vmem_limit_bytes=VMEM_LIMIT),
            name=("moba_attn" if moba else "diff_attn") + ("_online" if online else ""),
        )(itab, jtab, otab, q_all, k_all, vt_all, mref, *aux)

    return lax.cond(score_bound <= MAX_FIXED_BOUND, lambda: run(False), lambda: run(True))


def _merge_kernel(oa_ref, ob_ref, x_ref, g1_ref, wg_ref, bg_ref, wa_ref, wb_ref, wo_ref, g2_ref,
                  x1_ref, h2_ref):
    d = x_ref.shape[-1]
    x = x_ref[0]
    ms = jnp.mean(x * x, axis=-1, keepdims=True)
    h = (x * lax.rsqrt(ms + EPS) * g1_ref[...]).astype(bf16)
    n_qkv = wg_ref.shape[1] - 2 * d
    gates = jax.nn.sigmoid(jnp.dot(h, wg_ref[:, n_qkv:], preferred_element_type=f32) + bg_ref[...])
    merged = (gates[:, :d] * jnp.dot(oa_ref[0], wa_ref[...], preferred_element_type=f32)
              + gates[:, d:] * jnp.dot(ob_ref[0], wb_ref[...], preferred_element_type=f32))
    x1 = x + jnp.dot(merged.astype(bf16), wo_ref[...], preferred_element_type=f32)
    x1_ref[0] = x1
    ms = jnp.mean(x1 * x1, axis=-1, keepdims=True)
    h2_ref[0] = (x1 * lax.rsqrt(ms + EPS) * g2_ref[...]).astype(bf16)


def _merge(oa, ob, x, g1, wg, bg, wa, wb, wo, g2, *, tm):
    bsz, seq, d = x.shape
    const = lambda b, i: (0, 0)
    tile = lambda b, i: (b, i, 0)
    return pl.pallas_call(
        _merge_kernel,
        grid=(bsz, seq // tm),
        in_specs=[
            pl.BlockSpec((1, tm, oa.shape[-1]), tile),
            pl.BlockSpec((1, tm, ob.shape[-1]), tile),
            pl.BlockSpec((1, tm, d), tile),
            pl.BlockSpec((1, d), const),
            pl.BlockSpec(wg.shape, const, pipeline_mode=pl.Buffered(1)),
            pl.BlockSpec(bg.shape, const),
            pl.BlockSpec(wa.shape, const),
            pl.BlockSpec(wb.shape, const),
            pl.BlockSpec(wo.shape, const),
            pl.BlockSpec((1, d), const),
        ],
        out_specs=[pl.BlockSpec((1, tm, d), tile), pl.BlockSpec((1, tm, d), tile)],
        out_shape=[jax.ShapeDtypeStruct((bsz, seq, d), f32),
                   jax.ShapeDtypeStruct((bsz, seq, d), bf16)],
        compiler_params=pltpu.CompilerParams(
            dimension_semantics=("arbitrary", "arbitrary"), vmem_limit_bytes=VMEM_LIMIT),
        name="merge_outproj",
    )(oa, ob, x, g1, wg, bg, wa, wb, wo, g2)


HALO = 8
assert HALO >= CONV_W - 1


def _mlp_kernel(h2_ref, x1_ref, wup_ref, cw_ref, cb_ref, wd_ref, o_ref, u_scr, *, tm, tiles_per_seq):
    i = pl.program_id(0)
    dff = wd_ref.shape[0]

    @pl.when(i % tiles_per_seq == 0)
    def _():
        u_scr[:HALO, :] = jnp.zeros((HALO, dff), f32)

    h2 = h2_ref[...]
    u = jnp.dot(h2, wup_ref[:, :dff], preferred_element_type=f32)
    g = jnp.dot(h2, wup_ref[:, dff:], preferred_element_type=f32)
    u_scr[HALO:, :] = u
    y = cb_ref[...]
    for j in range(CONV_W):
        y = y + u_scr[pl.ds(HALO - (CONV_W - 1) + j, tm), :] * cw_ref[j:j + 1, :]
    u_scr[:HALO, :] = u[tm - HALO:, :]
    act = (jax.nn.gelu(y) * g).astype(bf16)
    o_ref[...] = x1_ref[...] + jnp.dot(act, wd_ref[...], preferred_element_type=f32)


def _mlp(h2, x1, w_up_bf, conv_w, conv_b, w_down_bf, *, seq, tm):
    n, d = h2.shape
    dff = w_down_bf.shape[0]
    const = lambda i: (0, 0)
    return pl.pallas_call(
        functools.partial(_mlp_kernel, tm=tm, tiles_per_seq=seq // tm),
        grid=(n // tm,),
        in_specs=[
            pl.BlockSpec((tm, d), lambda i: (i, 0)),
            pl.BlockSpec((tm, d), lambda i: (i, 0)),
            pl.BlockSpec(w_up_bf.shape, const, pipeline_mode=pl.Buffered(1)),
            pl.BlockSpec(conv_w.shape, const),
            pl.BlockSpec(conv_b.shape, const),
            pl.BlockSpec(w_down_bf.shape, const, pipeline_mode=pl.Buffered(1)),
        ],
        out_specs=pl.BlockSpec((tm, d), lambda i: (i, 0)),
        out_shape=jax.ShapeDtypeStruct((n, d), f32),
        scratch_shapes=[pltpu.VMEM((tm + HALO, dff), f32)],
        compiler_params=pltpu.CompilerParams(
            dimension_semantics=("arbitrary",), vmem_limit_bytes=VMEM_LIMIT),
        name="conv_mlp",
    )(h2, x1, w_up_bf, conv_w, conv_b, w_down_bf)


def _rope_tables(seq):
    inv = 1.0 / (ROPE_THETA ** (jnp.arange(0, HEAD_DIM, 2, dtype=f32) / HEAD_DIM))
    ang = jnp.arange(seq, dtype=f32)[:, None] * inv[None, :]
    cos, sin = jnp.cos(ang), jnp.sin(ang)
    reps = LANES // HEAD_DIM
    return (jnp.tile(cos, (1, 2 * reps)), jnp.tile(jnp.concatenate([-sin, sin], axis=1), (1, reps)))


def kernel(x, norm1_g, w_in, b_gate, qn_a, kn_a, lam_q1, lam_k1, lam_q2, lam_k2, subln_g, qn_b, kn_b,
           w_a_proj, w_b_proj, w_out, norm2_g, w_up, conv_w, conv_b, w_down):
    bsz, seq, d = x.shape
    depth = w_in.shape[0]
    assert seq % 512 == 0 and d % LANES == 0
    cos_t, sin_t = _rope_tables(seq)
    seg = jnp.arange(LANES) // HEAD_DIM
    bd = (seg[:, None] == seg[None, :]).astype(bf16)
    reps = LANES // HEAD_DIM
    qscale = HEAD_DIM ** -0.5 * LOG2E
    for l in range(depth):
        lam_init = 0.8 - 0.6 * math.exp(-0.3 * l)
        gains = jnp.stack([jnp.tile(qn_a[l], reps) * qscale, jnp.tile(kn_a[l], reps),
                           jnp.tile(qn_b[l], reps) * qscale, jnp.tile(kn_b[l], reps)])
        w_in_bf = w_in[l].astype(bf16)
        q_all, k_all, vt_all, kmean = _inproj(
            x, norm1_g[l][None], w_in_bf, gains, cos_t, sin_t, bd, tm=1024)
        kmean = kmean.reshape(bsz, seq // BLK, B_WIDTH)
        lamvec = jnp.stack([lam_q1[l], lam_k1[l], lam_q2[l], lam_k2[l]])
        gmax = jnp.max(jnp.abs(gains), axis=1) * (HEAD_DIM ** 0.5)
        subg_cols = jnp.tile(subln_g[l][:, None], (1, BLK))
        oa = _attention(q_all, k_all, vt_all, (lamvec, subg_cols),
                        gmax[0] * gmax[1] * BOUND_SLACK, moba=False, lam_init=lam_init)
        ob = _attention(q_all, k_all, vt_all, (kmean,),
                        gmax[2] * gmax[3] * BOUND_SLACK, moba=True, lam_init=lam_init)
        x1, h2 = _merge(oa, ob, x, norm1_g[l][None], w_in_bf, b_gate[l][None],
                        w_a_proj[l].astype(bf16), w_b_proj[l].astype(bf16),
                        w_out[l].astype(bf16), norm2_g[l][None], tm=1024)
        dff = w_down.shape[1]
        x = _mlp(h2.reshape(bsz * seq, d), x1.reshape(bsz * seq, d), w_up[l].astype(bf16), conv_w[l],
                 conv_b[l][None], w_down[l].astype(bf16), seq=seq, tm=512).reshape(bsz, seq, d)
    return x
```

```python
import functools
import math

import jax
import jax.numpy as jnp
from jax import lax
from jax.experimental import pallas as pl
from jax.experimental.pallas import tpu as pltpu

HEAD_DIM = 64
A_HEADS = 4
A_VDIM = 2 * HEAD_DIM
A_WIDTH = A_HEADS * A_VDIM
B_HEADS = 8
B_WIDTH = B_HEADS * HEAD_DIM
B_BLOCK = 256
B_TOPK = 3
CONV_W = 3
ROPE_THETA = 10000.0
EPS = 1e-6

LANES = 128
BLK = B_BLOCK
PAIRS = A_WIDTH // LANES
LOG2E = 1.4426950408889634
NEG_BIG = -1e30
POS_BIG = 1e30
VMEM_LIMIT = 56 * 1024 * 1024

f32 = jnp.float32
bf16 = jnp.bfloat16


def _nt_dot(a, b, **kw):
    return lax.dot_general(a, b, (((1,), (1,)), ((), ())), preferred_element_type=f32, **kw)


def _inproj_kernel(x_ref, g1_ref, w_ref, gain_ref, cos_ref, sin_ref, bd_ref,
                   q_ref, k_ref, vt_ref, kmean_ref, h_scr, *, tm):
    nblk = tm // BLK
    x = x_ref[0]
    ms = jnp.mean(x * x, axis=-1, keepdims=True)
    h_scr[...] = (x * lax.rsqrt(ms + EPS) * g1_ref[...]).astype(bf16)
    cos = cos_ref[...]
    sin = sin_ref[...]
    bd = bd_ref[...]
    lane = lax.broadcasted_iota(jnp.int32, (tm, LANES), 1)
    first_half = (lane & (HEAD_DIM - 1)) < (HEAD_DIM // 2)

    def col_tile(j):
        return jnp.dot(h_scr[...], w_ref[:, j * A_WIDTH:(j + 1) * A_WIDTH],
                       preferred_element_type=f32)

    def norm_rope(y, gain):
        ss = jnp.dot((y * y).astype(bf16), bd, preferred_element_type=f32)
        yn = y * lax.rsqrt(ss * (1.0 / HEAD_DIM) + EPS) * gain
        rot = jnp.where(first_half, pltpu.roll(yn, LANES - HEAD_DIM // 2, 1),
                        pltpu.roll(yn, HEAD_DIM // 2, 1))
        return yn * cos + rot * sin

    for branch in range(2):
        acc_q = col_tile(3 * branch)
        for c in range(PAIRS):
            out = norm_rope(acc_q[:, c * LANES:(c + 1) * LANES], gain_ref[2 * branch:2 * branch + 1, :])
            for bi in range(nblk):
                q_ref[0, branch * PAIRS + c, bi] = out[bi * BLK:(bi + 1) * BLK].astype(bf16)
        acc_k = col_tile(3 * branch + 1)
        for c in range(PAIRS):
            out = norm_rope(acc_k[:, c * LANES:(c + 1) * LANES],
                            gain_ref[2 * branch + 1:2 * branch + 2, :])
            for bi in range(nblk):
                blk = out[bi * BLK:(bi + 1) * BLK]
                k_ref[0, branch * PAIRS + c, bi] = blk.astype(bf16)
                if branch == 1:
                    kmean_ref[0, bi:bi + 1, c * LANES:(c + 1) * LANES] = jnp.mean(
                        blk, axis=0, keepdims=True)
        acc_v = col_tile(3 * branch + 2)
        for c in range(PAIRS):
            for bi in range(nblk):
                blk = acc_v[bi * BLK:(bi + 1) * BLK, c * LANES:(c + 1) * LANES]
                vt_ref[0, branch * PAIRS + c, bi] = blk.T.astype(bf16)


def _inproj(x, g1, w_in_bf, gains, cos_t, sin_t, bd, *, tm):
    bsz, seq, d = x.shape
    n_cols = w_in_bf.shape[1]
    nblk = tm // BLK
    spt = seq // tm
    nb = seq // BLK
    qk_shape = jax.ShapeDtypeStruct((bsz, 2 * PAIRS, nb, BLK, LANES), bf16)
    vt_shape = jax.ShapeDtypeStruct((bsz, 2 * PAIRS, nb, LANES, BLK), bf16)
    qk_spec = pl.BlockSpec((1, 2 * PAIRS, nblk, BLK, LANES), lambda b, i: (b, 0, i, 0, 0))
    vt_spec = pl.BlockSpec((1, 2 * PAIRS, nblk, LANES, BLK), lambda b, i: (b, 0, i, 0, 0))
    return pl.pallas_call(
        functools.partial(_inproj_kernel, tm=tm),
        grid=(bsz, spt),
        in_specs=[
            pl.BlockSpec((1, tm, d), lambda b, i: (b, i, 0)),
            pl.BlockSpec((1, d), lambda b, i: (0, 0)),
            pl.BlockSpec((d, n_cols), lambda b, i: (0, 0), pipeline_mode=pl.Buffered(1)),
            pl.BlockSpec((4, LANES), lambda b, i: (0, 0)),
            pl.BlockSpec((tm, LANES), lambda b, i: (i, 0)),
            pl.BlockSpec((tm, LANES), lambda b, i: (i, 0)),
            pl.BlockSpec((LANES, LANES), lambda b, i: (0, 0)),
        ],
        out_specs=[
            qk_spec, qk_spec, vt_spec,
            pl.BlockSpec((1, nblk, B_WIDTH), lambda b, i: (b * spt + i, 0, 0)),
        ],
        out_shape=[
            qk_shape, qk_shape, vt_shape,
            jax.ShapeDtypeStruct((bsz * spt, nblk, B_WIDTH), f32),
        ],
        scratch_shapes=[pltpu.VMEM((tm, d), bf16)],
        compiler_params=pltpu.CompilerParams(
            dimension_semantics=("arbitrary", "arbitrary"), vmem_limit_bytes=VMEM_LIMIT),
        name="inproj",
    )(x, g1, w_in_bf, gains, cos_t, sin_t, bd)


TRIP_STEPS = 34
QK_AHEAD = {False: 6, True: 6}
TILE_UNROLL = 4
BOUND_SLACK = 1.0 + 2.0 ** -6
MAX_FIXED_BOUND = 60.0


def _attn_kernel(itab_ref, jtab_ref, otab_ref, *refs, moba, online, lam_init, n_steps, n_masked):
    if moba:
        (q_ref, k_ref, vt_ref, mref_ref, kmean_ref, o_ref,
         qs_scr, m_scr, l_scr, acc_scr, s_scr, sel_scr) = refs
    else:
        (q_ref, k_ref, vt_ref, mref_ref, lam_ref, subg_ref, o_ref,
         qs_scr, m_scr, l_scr, acc_scr, s_scr) = refs
    nb = k_ref.shape[2]
    acc_rows = acc_scr.shape[1]
    lane = lax.broadcasted_iota(jnp.int32, (BLK, LANES), 1)
    key_minus_row = (lax.broadcasted_iota(jnp.int32, (BLK, 2 * BLK), 0)
                     - (lax.broadcasted_iota(jnp.int32, (BLK, 2 * BLK), 1) & (BLK - 1)))

    if moba:
        km = kmean_ref[0]
        km_parts = []
        for _ in range(3):
            part = km.astype(bf16)
            km_parts.append(part)
            km = km - part.astype(f32)

    nidx = lax.broadcasted_iota(jnp.int32, (nb, 2 * BLK), 0)
    for i in range(nb):
        q = q_ref[0, 0, i].astype(f32)
        qs = jnp.concatenate([jnp.where(lane < HEAD_DIM, q, 0.0),
                              jnp.where(lane >= HEAD_DIM, q, 0.0)], axis=0).astype(bf16)
        qs_scr[i] = qs
        if online:
            m_scr[i] = jnp.full((1, 2 * BLK), NEG_BIG, f32)
        l_scr[i] = jnp.zeros((1, 2 * BLK), f32)
        acc_scr[i] = jnp.zeros((acc_rows, 2 * BLK), f32)
        if moba:
            keep = nidx <= i
            if i > B_TOPK:
                g = _nt_dot(km_parts[0], qs) + _nt_dot(km_parts[1], qs) + _nt_dot(km_parts[2], qs)
                cnt = jnp.zeros((nb, 2 * BLK), f32)
                for mm in range(i):
                    row = g[mm:mm + 1, :]
                    beats = (row > g) | ((row == g) & (nidx > mm))
                    cnt = cnt + jnp.where(beats, 1.0, 0.0)
                keep = ((cnt < float(B_TOPK)) & (nidx < i)) | (nidx == i)
            sel_scr[i] = jnp.where(keep, 1.0, 0.0)

    def qk(step, slot, u):
        s = _nt_dot(k_ref[0, 0, jtab_ref[step]], qs_scr[itab_ref[step]])
        if slot == 0 and u < n_masked:
            s = jnp.where(key_minus_row <= otab_ref[step], s, NEG_BIG)
        s_scr[slot, u] = s

    def update(step, slot, u):
        i = itab_ref[step]
        j = jtab_ref[step]
        s = s_scr[slot, u]
        if moba:
            chosen = sel_scr[i, pl.ds(j, 1), :] > 0.5
        if online:
            m_old = m_scr[i]
            m_new = jnp.maximum(m_old, jnp.max(s, axis=0, keepdims=True))
            if moba:
                m_new = jnp.where(chosen, m_new, m_old)
            m_sub = m_new
            alpha = jnp.exp2(m_old - m_new)
            m_scr[i] = m_new
        else:
            m_sub = mref_ref[...]
        if moba:
            m_sub = jnp.where(chosen, m_sub, POS_BIG)
        p = jnp.exp2(s - m_sub)
        psum = jnp.sum(p, axis=0, keepdims=True)
        pb = p.astype(bf16)
        vt = vt_ref[0, 0, j]
        pv = jnp.dot(vt, pb, preferred_element_type=f32)
        if online:
            l_scr[i] = alpha * l_scr[i] + psum
            acc_scr[i] = alpha * acc_scr[i] + pv
        else:
            l_scr[i] = l_scr[i] + psum
            acc_scr[i] = acc_scr[i] + pv

    ahead = min(QK_AHEAD[moba], TRIP_STEPS)

    def trip(it, carry):
        base = it * TRIP_STEPS
        for u in range(ahead):
            qk(base + u, 0, u)
        for u in range(TRIP_STEPS):
            if u + ahead < TRIP_STEPS:
                qk(base + u + ahead, 0, u + ahead)
            update(base + u, 0, u)
        return carry

    lax.fori_loop(0, n_steps // TRIP_STEPS, trip, 0)

    if not moba:
        lv = lam_ref[...]
        lam = (jnp.exp(jnp.sum(lv[0:1] * lv[1:2], axis=-1, keepdims=True))
               - jnp.exp(jnp.sum(lv[2:3] * lv[3:4], axis=-1, keepdims=True)) + lam_init)

    def finish_tiles(t, carry):
        for r in range(TILE_UNROLL):
            i = t * TILE_UNROLL + r
            o = acc_scr[i] * (1.0 / l_scr[i])
            rows = pl.ds(pl.multiple_of(i * BLK, BLK), BLK)
            if moba:
                ot = jnp.concatenate([o[:HEAD_DIM, :BLK], o[HEAD_DIM:, BLK:]], axis=0).T
                o_ref[0, rows, :] = ot.astype(bf16)
            else:
                od = o[:, :BLK] - lam * o[:, BLK:]
                ms = jnp.mean(od * od, axis=0, keepdims=True)
                od = od * lax.rsqrt(ms + EPS) * subg_ref[...] * (1.0 - lam_init)
                o_ref[0, rows, :] = od.T.astype(bf16)
        return carry

    lax.fori_loop(0, nb // TILE_UNROLL, finish_tiles, 0)


def _step_tables(nb):
    per_trip = TRIP_STEPS
    diag = [(i, i, 0) for i in range(nb)]
    rest = [(i, j, BLK) for j in range(nb) for i in range(j + 1, nb)]
    n_steps = len(diag) + len(rest)
    assert n_steps % per_trip == 0
    n_trips = n_steps // per_trip
    n_masked = -(-len(diag) // n_trips)
    assert n_masked <= per_trip
    steps = []
    for _ in range(n_trips):
        for _ in range(n_masked):
            steps.append(diag.pop(0) if diag else rest.pop(0))
        for _ in range(per_trip - n_masked):
            steps.append(rest.pop(0))
    assert not diag and not rest
    return [jnp.asarray([s[c] for s in steps], jnp.int32) for c in range(3)], n_steps, n_masked


def _attention(q_all, k_all, vt_all, aux, score_bound, *, moba, lam_init):
    bsz, _, nb, _, _ = q_all.shape
    off = PAIRS if moba else 0
    assert nb % TILE_UNROLL == 0
    (itab, jtab, otab), n_steps, n_masked = _step_tables(nb)
    const = lambda b, h, it, jt, ot: (0, 0)
    in_specs = [
        pl.BlockSpec((1, 1, nb, BLK, LANES), lambda b, h, it, jt, ot: (b, h + off, 0, 0, 0)),
        pl.BlockSpec((1, 1, nb, BLK, LANES), lambda b, h, it, jt, ot: (b, h + off, 0, 0, 0)),
        pl.BlockSpec((1, 1, nb, LANES, BLK), lambda b, h, it, jt, ot: (b, h + off, 0, 0, 0)),
        pl.BlockSpec((1, 2 * BLK), const),
    ]
    scratch = [pltpu.VMEM((nb, 2 * BLK, LANES), bf16),
               pltpu.VMEM((nb, 1, 2 * BLK), f32), pltpu.VMEM((nb, 1, 2 * BLK), f32),
               pltpu.VMEM((nb, LANES, 2 * BLK), f32),
               pltpu.VMEM((1, TRIP_STEPS, BLK, 2 * BLK), f32)]
    if moba:
        (kmean,) = aux
        in_specs.append(pl.BlockSpec((1, nb, LANES), lambda b, h, it, jt, ot: (b, 0, h)))
        scratch.append(pltpu.VMEM((nb, nb, 2 * BLK), f32))
    else:
        lamvec, subg = aux
        in_specs.append(pl.BlockSpec(lamvec.shape, const))
        in_specs.append(pl.BlockSpec(subg.shape, const))
    mref = jnp.full((1, 2 * BLK), score_bound, f32)

    def run(online):
        return pl.pallas_call(
            functools.partial(_attn_kernel, moba=moba, online=online, lam_init=lam_init,
                              n_steps=n_steps, n_masked=n_masked),
            grid_spec=pltpu.PrefetchScalarGridSpec(
                num_scalar_prefetch=3,
                grid=(bsz, PAIRS),
                in_specs=in_specs,
                out_specs=pl.BlockSpec((1, nb * BLK, LANES), lambda b, h, it, jt, ot: (b, 0, h)),
                scratch_shapes=scratch),
            out_shape=jax.ShapeDtypeStruct((bsz, nb * BLK, PAIRS * LANES), bf16),
            compiler_params=pltpu.CompilerParams(
                dimension_semantics=("arbitrary", "arbitrary"), vmem_limit_bytes=VMEM_LIMIT),
            name=("moba_attn" if moba else "diff_attn") + ("_online" if online else ""),
        )(itab, jtab, otab, q_all, k_all, vt_all, mref, *aux)

    return lax.cond(score_bound <= MAX_FIXED_BOUND, lambda: run(False), lambda: run(True))


def _merge_kernel(oa_ref, ob_ref, x_ref, g1_ref, wg_ref, bg_ref, wa_ref, wb_ref, wo_ref, g2_ref,
                  x1_ref, h2_ref):
    d = x_ref.shape[-1]
    x = x_ref[0]
    ms = jnp.mean(x * x, axis=-1, keepdims=True)
    h = (x * lax.rsqrt(ms + EPS) * g1_ref[...]).astype(bf16)
    n_qkv = wg_ref.shape[1] - 2 * d
    gates = jax.nn.sigmoid(jnp.dot(h, wg_ref[:, n_qkv:], preferred_element_type=f32) + bg_ref[...])
    merged = (gates[:, :d] * jnp.dot(oa_ref[0], wa_ref[...], preferred_element_type=f32)
              + gates[:, d:] * jnp.dot(ob_ref[0], wb_ref[...], preferred_element_type=f32))
    x1 = x + jnp.dot(merged.astype(bf16), wo_ref[...], preferred_element_type=f32)
    x1_ref[0] = x1
    ms = jnp.mean(x1 * x1, axis=-1, keepdims=True)
    h2_ref[0] = (x1 * lax.rsqrt(ms + EPS) * g2_ref[...]).astype(bf16)


def _merge(oa, ob, x, g1, wg, bg, wa, wb, wo, g2, *, tm):
    bsz, seq, d = x.shape
    const = lambda b, i: (0, 0)
    tile = lambda b, i: (b, i, 0)
    return pl.pallas_call(
        _merge_kernel,
        grid=(bsz, seq // tm),
        in_specs=[
            pl.BlockSpec((1, tm, oa.shape[-1]), tile),
            pl.BlockSpec((1, tm, ob.shape[-1]), tile),
            pl.BlockSpec((1, tm, d), tile),
            pl.BlockSpec((1, d), const),
            pl.BlockSpec(wg.shape, const, pipeline_mode=pl.Buffered(1)),
            pl.BlockSpec(bg.shape, const),
            pl.BlockSpec(wa.shape, const),
            pl.BlockSpec(wb.shape, const),
            pl.BlockSpec(wo.shape, const),
            pl.BlockSpec((1, d), const),
        ],
        out_specs=[pl.BlockSpec((1, tm, d), tile), pl.BlockSpec((1, tm, d), tile)],
        out_shape=[jax.ShapeDtypeStruct((bsz, seq, d), f32),
                   jax.ShapeDtypeStruct((bsz, seq, d), bf16)],
        compiler_params=pltpu.CompilerParams(
            dimension_semantics=("arbitrary", "arbitrary"), vmem_limit_bytes=VMEM_LIMIT),
        name="merge_outproj",
    )(oa, ob, x, g1, wg, bg, wa, wb, wo, g2)


HALO = 8
assert HALO >= CONV_W - 1


def _mlp_kernel(h2_ref, x1_ref, wup_ref, cw_ref, cb_ref, wd_ref, o_ref, u_scr, *, tm, tiles_per_seq):
    i = pl.program_id(0)
    dff = wd_ref.shape[0]

    @pl.when(i % tiles_per_seq == 0)
    def _():
        u_scr[:HALO, :] = jnp.zeros((HALO, dff), f32)

    h2 = h2_ref[...]
    u = jnp.dot(h2, wup_ref[:, :dff], preferred_element_type=f32)
    g = jnp.dot(h2, wup_ref[:, dff:], preferred_element_type=f32)
    u_scr[HALO:, :] = u
    y = cb_ref[...]
    for j in range(CONV_W):
        y = y + u_scr[pl.ds(HALO - (CONV_W - 1) + j, tm), :] * cw_ref[j:j + 1, :]
    u_scr[:HALO, :] = u[tm - HALO:, :]
    act = (jax.nn.gelu(y) * g).astype(bf16)
    o_ref[...] = x1_ref[...] + jnp.dot(act, wd_ref[...], preferred_element_type=f32)


def _mlp(h2, x1, w_up_bf, conv_w, conv_b, w_down_bf, *, seq, tm):
    n, d = h2.shape
    dff = w_down_bf.shape[0]
    const = lambda i: (0, 0)
    return pl.pallas_call(
        functools.partial(_mlp_kernel, tm=tm, tiles_per_seq=seq // tm),
        grid=(n // tm,),
        in_specs=[
            pl.BlockSpec((tm, d), lambda i: (i, 0)),
            pl.BlockSpec((tm, d), lambda i: (i, 0)),
            pl.BlockSpec(w_up_bf.shape, const, pipeline_mode=pl.Buffered(1)),
            pl.BlockSpec(conv_w.shape, const),
            pl.BlockSpec(conv_b.shape, const),
            pl.BlockSpec(w_down_bf.shape, const, pipeline_mode=pl.Buffered(1)),
        ],
        out_specs=pl.BlockSpec((tm, d), lambda i: (i, 0)),
        out_shape=jax.ShapeDtypeStruct((n, d), f32),
        scratch_shapes=[pltpu.VMEM((tm + HALO, dff), f32)],
        compiler_params=pltpu.CompilerParams(
            dimension_semantics=("arbitrary",), vmem_limit_bytes=VMEM_LIMIT),
        name="conv_mlp",
    )(h2, x1, w_up_bf, conv_w, conv_b, w_down_bf)


def _rope_tables(seq):
    inv = 1.0 / (ROPE_THETA ** (jnp.arange(0, HEAD_DIM, 2, dtype=f32) / HEAD_DIM))
    ang = jnp.arange(seq, dtype=f32)[:, None] * inv[None, :]
    cos, sin = jnp.cos(ang), jnp.sin(ang)
    reps = LANES // HEAD_DIM
    return (jnp.tile(cos, (1, 2 * reps)), jnp.tile(jnp.concatenate([-sin, sin], axis=1), (1, reps)))


def kernel(x, norm1_g, w_in, b_gate, qn_a, kn_a, lam_q1, lam_k1, lam_q2, lam_k2, subln_g, qn_b, kn_b,
           w_a_proj, w_b_proj, w_out, norm2_g, w_up, conv_w, conv_b, w_down):
    bsz, seq, d = x.shape
    depth = w_in.shape[0]
    assert seq % 512 == 0 and d % LANES == 0
    cos_t, sin_t = _rope_tables(seq)
    seg = jnp.arange(LANES) // HEAD_DIM
    bd = (seg[:, None] == seg[None, :]).astype(bf16)
    reps = LANES // HEAD_DIM
    qscale = HEAD_DIM ** -0.5 * LOG2E
    for l in range(depth):
        lam_init = 0.8 - 0.6 * math.exp(-0.3 * l)
        gains = jnp.stack([jnp.tile(qn_a[l], reps) * qscale, jnp.tile(kn_a[l], reps),
                           jnp.tile(qn_b[l], reps) * qscale, jnp.tile(kn_b[l], reps)])
        w_in_bf = w_in[l].astype(bf16)
        q_all, k_all, vt_all, kmean = _inproj(
            x, norm1_g[l][None], w_in_bf, gains, cos_t, sin_t, bd, tm=1024)
        kmean = kmean.reshape(bsz, seq // BLK, B_WIDTH)
        lamvec = jnp.stack([lam_q1[l], lam_k1[l], lam_q2[l], lam_k2[l]])
        gmax = jnp.max(jnp.abs(gains), axis=1) * (HEAD_DIM ** 0.5)
        subg_cols = jnp.tile(subln_g[l][:, None], (1, BLK))
        oa = _attention(q_all, k_all, vt_all, (lamvec, subg_cols),
                        gmax[0] * gmax[1] * BOUND_SLACK, moba=False, lam_init=lam_init)
        ob = _attention(q_all, k_all, vt_all, (kmean,),
                        gmax[2] * gmax[3] * BOUND_SLACK, moba=True, lam_init=lam_init)
        x1, h2 = _merge(oa, ob, x, norm1_g[l][None], w_in_bf, b_gate[l][None],
                        w_a_proj[l].astype(bf16), w_b_proj[l].astype(bf16),
                        w_out[l].astype(bf16), norm2_g[l][None], tm=1024)
        dff = w_down.shape[1]
        x = _mlp(h2.reshape(bsz * seq, d), x1.reshape(bsz * seq, d), w_up[l].astype(bf16), conv_w[l],
                 conv_b[l][None], w_down[l].astype(bf16), seq=seq, tm=512).reshape(bsz, seq, d)
    return x
```

```python
import functools
import math

import jax
import jax.numpy as jnp
from jax import lax
from jax.experimental import pallas as pl
from jax.experimental.pallas import tpu as pltpu

HEAD_DIM = 64
A_HEADS = 4
A_VDIM = 2 * HEAD_DIM
A_WIDTH = A_HEADS * A_VDIM
B_HEADS = 8
B_WIDTH = B_HEADS * HEAD_DIM
B_BLOCK = 256
B_TOPK = 3
CONV_W = 3
ROPE_THETA = 10000.0
EPS = 1e-6

LANES = 128
BLK = B_BLOCK
PAIRS = A_WIDTH // LANES
LOG2E = 1.4426950408889634
NEG_BIG = -1e30
POS_BIG = 1e30
VMEM_LIMIT = 56 * 1024 * 1024

f32 = jnp.float32
bf16 = jnp.bfloat16


def _nt_dot(a, b, **kw):
    return lax.dot_general(a, b, (((1,), (1,)), ((), ())), preferred_element_type=f32, **kw)


def _inproj_kernel(x_ref, g1_ref, w_ref, gain_ref, cos_ref, sin_ref, bd_ref, *rest, tm, n_late):
    late_f32 = rest[:n_late]
    q_ref, k_ref, vt_ref, kmean_ref = rest[n_late:n_late + 4]
    late_bf16 = rest[n_late + 4:2 * n_late + 4]
    h_scr = rest[-1]
    for src, dst in zip(late_f32, late_bf16):
        dst[...] = src[...].astype(bf16)
    nblk = tm // BLK
    x = x_ref[0]
    ms = jnp.mean(x * x, axis=-1, keepdims=True)
    h_scr[...] = (x * lax.rsqrt(ms + EPS) * g1_ref[...]).astype(bf16)
    cos = cos_ref[...]
    sin = sin_ref[...]
    bd = bd_ref[...]
    lane = lax.broadcasted_iota(jnp.int32, (tm, LANES), 1)
    first_half = (lane & (HEAD_DIM - 1)) < (HEAD_DIM // 2)

    def col_tile(j):
        return jnp.dot(h_scr[...], w_ref[:, j * A_WIDTH:(j + 1) * A_WIDTH].astype(bf16),
                       preferred_element_type=f32)

    def norm_rope(y, gain):
        ss = jnp.dot((y * y).astype(bf16), bd, preferred_element_type=f32)
        yn = y * lax.rsqrt(ss * (1.0 / HEAD_DIM) + EPS) * gain
        rot = jnp.where(first_half, pltpu.roll(yn, LANES - HEAD_DIM // 2, 1),
                        pltpu.roll(yn, HEAD_DIM // 2, 1))
        return yn * cos + rot * sin

    for branch in range(2):
        acc_q = col_tile(3 * branch)
        for c in range(PAIRS):
            out = norm_rope(acc_q[:, c * LANES:(c + 1) * LANES], gain_ref[2 * branch:2 * branch + 1, :])
            for bi in range(nblk):
                q_ref[0, branch * PAIRS + c, bi] = out[bi * BLK:(bi + 1) * BLK].astype(bf16)
        acc_k = col_tile(3 * branch + 1)
        for c in range(PAIRS):
            out = norm_rope(acc_k[:, c * LANES:(c + 1) * LANES],
                            gain_ref[2 * branch + 1:2 * branch + 2, :])
            for bi in range(nblk):
                blk = out[bi * BLK:(bi + 1) * BLK]
                k_ref[0, branch * PAIRS + c, bi] = blk.astype(bf16)
                if branch == 1:
                    kmean_ref[0, bi:bi + 1, c * LANES:(c + 1) * LANES] = jnp.mean(
                        blk, axis=0, keepdims=True)
        acc_v = col_tile(3 * branch + 2)
        for c in range(PAIRS):
            for bi in range(nblk):
                blk = acc_v[bi * BLK:(bi + 1) * BLK, c * LANES:(c + 1) * LANES]
                vt_ref[0, branch * PAIRS + c, bi] = blk.T.astype(bf16)


def _inproj(x, g1, w_in, gains, cos_t, sin_t, bd, late_weights, *, tm):
    bsz, seq, d = x.shape
    steps = bsz * (seq // tm)
    slab = lambda b, i: (b * (seq // tm) + i, 0)
    late_specs = [pl.BlockSpec((w.shape[0] // steps, w.shape[1]), slab) for w in late_weights]
    assert all(w.shape[0] % (steps * 16) == 0 for w in late_weights)
    n_cols = 3 * A_WIDTH + 3 * B_WIDTH
    nblk = tm // BLK
    spt = seq // tm
    nb = seq // BLK
    qk_shape = jax.ShapeDtypeStruct((bsz, 2 * PAIRS, nb, BLK, LANES), bf16)
    vt_shape = jax.ShapeDtypeStruct((bsz, 2 * PAIRS, nb, LANES, BLK), bf16)
    qk_spec = pl.BlockSpec((1, 2 * PAIRS, nblk, BLK, LANES), lambda b, i: (b, 0, i, 0, 0))
    vt_spec = pl.BlockSpec((1, 2 * PAIRS, nblk, LANES, BLK), lambda b, i: (b, 0, i, 0, 0))
    return pl.pallas_call(
        functools.partial(_inproj_kernel, tm=tm, n_late=len(late_weights)),
        grid=(bsz, spt),
        in_specs=[
            pl.BlockSpec((1, tm, d), lambda b, i: (b, i, 0)),
            pl.BlockSpec((1, d), lambda b, i: (0, 0)),
            pl.BlockSpec((d, n_cols), lambda b, i: (0, 0), pipeline_mode=pl.Buffered(1)),
            pl.BlockSpec((4, LANES), lambda b, i: (0, 0)),
            pl.BlockSpec((tm, LANES), lambda b, i: (i, 0)),
            pl.BlockSpec((tm, LANES), lambda b, i: (i, 0)),
            pl.BlockSpec((LANES, LANES), lambda b, i: (0, 0)),
        ] + late_specs,
        out_specs=[
            qk_spec, qk_spec, vt_spec,
            pl.BlockSpec((1, nblk, B_WIDTH), lambda b, i: (b * spt + i, 0, 0)),
        ] + late_specs,
        out_shape=[
            qk_shape, qk_shape, vt_shape,
            jax.ShapeDtypeStruct((bsz * spt, nblk, B_WIDTH), f32),
        ] + [jax.ShapeDtypeStruct(w.shape, bf16) for w in late_weights],
        scratch_shapes=[pltpu.VMEM((tm, d), bf16)],
        compiler_params=pltpu.CompilerParams(
            dimension_semantics=("arbitrary", "arbitrary"), vmem_limit_bytes=VMEM_LIMIT),
        name="inproj",
    )(x, g1, w_in, gains, cos_t, sin_t, bd, *late_weights)


TRIP_STEPS = 34
QK_AHEAD = {False: 6, True: 6}
TILE_UNROLL = 4
BOUND_SLACK = 1.0 + 2.0 ** -6
MAX_FIXED_BOUND = 60.0


def _attn_kernel(itab_ref, jtab_ref, otab_ref, *refs, moba, online, lam_init, n_steps, n_masked):
    if moba:
        (q_ref, k_ref, vt_ref, mref_ref, kmean_ref, o_ref,
         qs_scr, m_scr, l_scr, acc_scr, s_scr, sel_scr) = refs
    else:
        (q_ref, k_ref, vt_ref, mref_ref, lam_ref, subg_ref, o_ref,
         qs_scr, m_scr, l_scr, acc_scr, s_scr) = refs
    nb = k_ref.shape[2]
    acc_rows = acc_scr.shape[1]
    lane = lax.broadcasted_iota(jnp.int32, (BLK, LANES), 1)
    key_minus_row = (lax.broadcasted_iota(jnp.int32, (BLK, 2 * BLK), 0)
                     - (lax.broadcasted_iota(jnp.int32, (BLK, 2 * BLK), 1) & (BLK - 1)))

    if moba:
        km = kmean_ref[0]
        km_parts = []
        for _ in range(3):
            part = km.astype(bf16)
            km_parts.append(part)
            km = km - part.astype(f32)

    nidx = lax.broadcasted_iota(jnp.int32, (nb, 2 * BLK), 0)
    for i in range(nb):
        q = q_ref[0, 0, i].astype(f32)
        qs = jnp.concatenate([jnp.where(lane < HEAD_DIM, q, 0.0),
                              jnp.where(lane >= HEAD_DIM, q, 0.0)], axis=0).astype(bf16)
        qs_scr[i] = qs
        if online:
            m_scr[i] = jnp.full((1, 2 * BLK), NEG_BIG, f32)
        l_scr[i] = jnp.zeros((1, 2 * BLK), f32)
        acc_scr[i] = jnp.zeros((acc_rows, 2 * BLK), f32)
        if moba:
            keep = nidx <= i
            if i > B_TOPK:
                g = _nt_dot(km_parts[0], qs) + _nt_dot(km_parts[1], qs) + _nt_dot(km_parts[2], qs)
                cnt = jnp.zeros((nb, 2 * BLK), f32)
                for mm in range(i):
                    row = g[mm:mm + 1, :]
                    beats = (row > g) | ((row == g) & (nidx > mm))
                    cnt = cnt + jnp.where(beats, 1.0, 0.0)
                keep = ((cnt < float(B_TOPK)) & (nidx < i)) | (nidx == i)
            sel_scr[i] = jnp.where(keep, 1.0, 0.0)

    def qk(step, slot, u):
        s = _nt_dot(k_ref[0, 0, jtab_ref[step]], qs_scr[itab_ref[step]])
        if slot == 0 and u < n_masked:
            s = jnp.where(key_minus_row <= otab_ref[step], s, NEG_BIG)
        s_scr[slot, u] = s

    def update(step, slot, u):
        i = itab_ref[step]
        j = jtab_ref[step]
        s = s_scr[slot, u]
        if moba:
            chosen = sel_scr[i, pl.ds(j, 1), :] > 0.5
        if online:
            m_old = m_scr[i]
            m_new = jnp.maximum(m_old, jnp.max(s, axis=0, keepdims=True))
            if moba:
                m_new = jnp.where(chosen, m_new, m_old)
            m_sub = m_new
            alpha = jnp.exp2(m_old - m_new)
            m_scr[i] = m_new
        else:
            m_sub = mref_ref[...]
        if moba:
            m_sub = jnp.where(chosen, m_sub, POS_BIG)
        p = jnp.exp2(s - m_sub)
        psum = jnp.sum(p, axis=0, keepdims=True)
        pb = p.astype(bf16)
        vt = vt_ref[0, 0, j]
        pv = jnp.dot(vt, pb, preferred_element_type=f32)
        if online:
            l_scr[i] = alpha * l_scr[i] + psum
            acc_scr[i] = alpha * acc_scr[i] + pv
        else:
            l_scr[i] = l_scr[i] + psum
            acc_scr[i] = acc_scr[i] + pv

    ahead = min(QK_AHEAD[moba], TRIP_STEPS)

    def trip(it, carry):
        base = it * TRIP_STEPS
        for u in range(ahead):
            qk(base + u, 0, u)
        for u in range(TRIP_STEPS):
            if u + ahead < TRIP_STEPS:
                qk(base + u + ahead, 0, u + ahead)
            update(base + u, 0, u)
        return carry

    lax.fori_loop(0, n_steps // TRIP_STEPS, trip, 0)

    if not moba:
        lv = lam_ref[...]
        lam = (jnp.exp(jnp.sum(lv[0:1] * lv[1:2], axis=-1, keepdims=True))
               - jnp.exp(jnp.sum(lv[2:3] * lv[3:4], axis=-1, keepdims=True)) + lam_init)

    def finish_tiles(t, carry):
        for r in range(TILE_UNROLL):
            i = t * TILE_UNROLL + r
            o = acc_scr[i] * (1.0 / l_scr[i])
            rows = pl.ds(pl.multiple_of(i * BLK, BLK), BLK)
            if moba:
                ot = jnp.concatenate([o[:HEAD_DIM, :BLK], o[HEAD_DIM:, BLK:]], axis=0).T
                o_ref[0, rows, :] = ot.astype(bf16)
            else:
                od = o[:, :BLK] - lam * o[:, BLK:]
                ms = jnp.mean(od * od, axis=0, keepdims=True)
                od = od * lax.rsqrt(ms + EPS) * subg_ref[...] * (1.0 - lam_init)
                o_ref[0, rows, :] = od.T.astype(bf16)
        return carry

    lax.fori_loop(0, nb // TILE_UNROLL, finish_tiles, 0)


def _step_tables(nb):
    per_trip = TRIP_STEPS
    diag = [(i, i, 0) for i in range(nb)]
    rest = [(i, j, BLK) for j in range(nb) for i in range(j + 1, nb)]
    n_steps = len(diag) + len(rest)
    assert n_steps % per_trip == 0
    n_trips = n_steps // per_trip
    n_masked = -(-len(diag) // n_trips)
    assert n_masked <= per_trip
    steps = []
    for _ in range(n_trips):
        for _ in range(n_masked):
            steps.append(diag.pop(0) if diag else rest.pop(0))
        for _ in range(per_trip - n_masked):
            steps.append(rest.pop(0))
    assert not diag and not rest
    return [jnp.asarray([s[c] for s in steps], jnp.int32) for c in range(3)], n_steps, n_masked


def _attention(q_all, k_all, vt_all, aux, score_bound, *, moba, lam_init):
    bsz, _, nb, _, _ = q_all.shape
    off = PAIRS if moba else 0
    assert nb % TILE_UNROLL == 0
    (itab, jtab, otab), n_steps, n_masked = _step_tables(nb)
    const = lambda b, h, it, jt, ot: (0, 0)
    in_specs = [
        pl.BlockSpec((1, 1, nb, BLK, LANES), lambda b, h, it, jt, ot: (b, h + off, 0, 0, 0)),
        pl.BlockSpec((1, 1, nb, BLK, LANES), lambda b, h, it, jt, ot: (b, h + off, 0, 0, 0)),
        pl.BlockSpec((1, 1, nb, LANES, BLK), lambda b, h, it, jt, ot: (b, h + off, 0, 0, 0)),
        pl.BlockSpec((1, 2 * BLK), const),
    ]
    scratch = [pltpu.VMEM((nb, 2 * BLK, LANES), bf16),
               pltpu.VMEM((nb, 1, 2 * BLK), f32), pltpu.VMEM((nb, 1, 2 * BLK), f32),
               pltpu.VMEM((nb, LANES, 2 * BLK), f32),
               pltpu.VMEM((1, TRIP_STEPS, BLK, 2 * BLK), f32)]
    if moba:
        (kmean,) = aux
        in_specs.append(pl.BlockSpec((1, nb, LANES), lambda b, h, it, jt, ot: (b, 0, h)))
        scratch.append(pltpu.VMEM((nb, nb, 2 * BLK), f32))
    else:
        lamvec, subg = aux
        in_specs.append(pl.BlockSpec(lamvec.shape, const))
        in_specs.append(pl.BlockSpec(subg.shape, const))
    mref = jnp.full((1, 2 * BLK), score_bound, f32)

    def run(online):
        return pl.pallas_call(
            functools.partial(_attn_kernel, moba=moba, online=online, lam_init=lam_init,
                              n_steps=n_steps, n_masked=n_masked),
            grid_spec=pltpu.PrefetchScalarGridSpec(
                num_scalar_prefetch=3,
                grid=(bsz, PAIRS),
                in_specs=in_specs,
                out_specs=pl.BlockSpec((1, nb * BLK, LANES), lambda b, h, it, jt, ot: (b, 0, h)),
                scratch_shapes=scratch),
            out_shape=jax.ShapeDtypeStruct((bsz, nb * BLK, PAIRS * LANES), bf16),
            compiler_params=pltpu.CompilerParams(
                dimension_semantics=("arbitrary", "arbitrary"), vmem_limit_bytes=VMEM_LIMIT),
            name=("moba_attn" if moba else "diff_attn") + ("_online" if online else ""),
        )(itab, jtab, otab, q_all, k_all, vt_all, mref, *aux)

    return lax.cond(score_bound <= MAX_FIXED_BOUND, lambda: run(False), lambda: run(True))


def _merge_kernel(oa_ref, ob_ref, x_ref, g1_ref, wga_ref, wgb_ref, bg_ref, wa_ref, wb_ref, wo_ref,
                  g2_ref, x1_ref, h2_ref):
    d = x_ref.shape[-1]
    x = x_ref[0]
    ms = jnp.mean(x * x, axis=-1, keepdims=True)
    h = (x * lax.rsqrt(ms + EPS) * g1_ref[...]).astype(bf16)
    merged = None
    for br, (wg_ref, o_ref, w_ref) in enumerate(((wga_ref, oa_ref, wa_ref), (wgb_ref, ob_ref, wb_ref))):
        gate = jax.nn.sigmoid(jnp.dot(h, wg_ref[...].astype(bf16), preferred_element_type=f32)
                              + bg_ref[:, br * d:(br + 1) * d])
        term = gate * jnp.dot(o_ref[0], w_ref[...].astype(bf16), preferred_element_type=f32)
        merged = term if merged is None else merged + term
    x1 = x + jnp.dot(merged.astype(bf16), wo_ref[...].astype(bf16), preferred_element_type=f32)
    x1_ref[0] = x1
    ms = jnp.mean(x1 * x1, axis=-1, keepdims=True)
    h2_ref[0] = (x1 * lax.rsqrt(ms + EPS) * g2_ref[...]).astype(bf16)


def _merge(oa, ob, x, g1, w_in, bg, wa, wb, wo, g2, *, tm):
    bsz, seq, d = x.shape
    gate_blk = (w_in.shape[1] - 2 * d) // d
    assert gate_blk * d + 2 * d == w_in.shape[1]
    const = lambda b, i: (0, 0)
    once = pl.Buffered(1)
    tile = lambda b, i: (b, i, 0)
    return pl.pallas_call(
        _merge_kernel,
        grid=(bsz, seq // tm),
        in_specs=[
            pl.BlockSpec((1, tm, oa.shape[-1]), tile),
            pl.BlockSpec((1, tm, ob.shape[-1]), tile),
            pl.BlockSpec((1, tm, d), tile),
            pl.BlockSpec((1, d), const),
            pl.BlockSpec((d, d), lambda b, i: (0, gate_blk), pipeline_mode=once),
            pl.BlockSpec((d, d), lambda b, i: (0, gate_blk + 1), pipeline_mode=once),
            pl.BlockSpec(bg.shape, const),
            pl.BlockSpec(wa.shape, const, pipeline_mode=once),
            pl.BlockSpec(wb.shape, const, pipeline_mode=once),
            pl.BlockSpec(wo.shape, const, pipeline_mode=once),
            pl.BlockSpec((1, d), const),
        ],
        out_specs=[pl.BlockSpec((1, tm, d), tile), pl.BlockSpec((1, tm, d), tile)],
        out_shape=[jax.ShapeDtypeStruct((bsz, seq, d), f32),
                   jax.ShapeDtypeStruct((bsz, seq, d), bf16)],
        compiler_params=pltpu.CompilerParams(
            dimension_semantics=("arbitrary", "arbitrary"), vmem_limit_bytes=VMEM_LIMIT),
        name="merge_outproj",
    )(oa, ob, x, g1, w_in, w_in, bg, wa, wb, wo, g2)


HALO = 8
assert HALO >= CONV_W - 1


def _mlp_kernel(h2_ref, x1_ref, wup_ref, cw_ref, cb_ref, wd_ref, o_ref, u_scr, *, tm, tiles_per_seq):
    i = pl.program_id(0)
    dff = wd_ref.shape[0]

    @pl.when(i % tiles_per_seq == 0)
    def _():
        u_scr[:HALO, :] = jnp.zeros((HALO, dff), f32)

    h2 = h2_ref[...]
    u = jnp.dot(h2, wup_ref[:, :dff], preferred_element_type=f32)
    g = jnp.dot(h2, wup_ref[:, dff:], preferred_element_type=f32)
    u_scr[HALO:, :] = u
    y = cb_ref[...]
    for j in range(CONV_W):
        y = y + u_scr[pl.ds(HALO - (CONV_W - 1) + j, tm), :] * cw_ref[j:j + 1, :]
    u_scr[:HALO, :] = u[tm - HALO:, :]
    act = (jax.nn.gelu(y) * g).astype(bf16)
    o_ref[...] = x1_ref[...] + jnp.dot(act, wd_ref[...], preferred_element_type=f32)


def _mlp(h2, x1, w_up_bf, conv_w, conv_b, w_down_bf, *, seq, tm):
    n, d = h2.shape
    dff = w_down_bf.shape[0]
    const = lambda i: (0, 0)
    return pl.pallas_call(
        functools.partial(_mlp_kernel, tm=tm, tiles_per_seq=seq // tm),
        grid=(n // tm,),
        in_specs=[
            pl.BlockSpec((tm, d), lambda i: (i, 0)),
            pl.BlockSpec((tm, d), lambda i: (i, 0)),
            pl.BlockSpec(w_up_bf.shape, const, pipeline_mode=pl.Buffered(1)),
            pl.BlockSpec(conv_w.shape, const),
            pl.BlockSpec(conv_b.shape, const),
            pl.BlockSpec(w_down_bf.shape, const, pipeline_mode=pl.Buffered(1)),
        ],
        out_specs=pl.BlockSpec((tm, d), lambda i: (i, 0)),
        out_shape=jax.ShapeDtypeStruct((n, d), f32),
        scratch_shapes=[pltpu.VMEM((tm + HALO, dff), f32)],
        compiler_params=pltpu.CompilerParams(
            dimension_semantics=("arbitrary",), vmem_limit_bytes=VMEM_LIMIT),
        name="conv_mlp",
    )(h2, x1, w_up_bf, conv_w, conv_b, w_down_bf)


def _rope_tables(seq):
    inv = 1.0 / (ROPE_THETA ** (jnp.arange(0, HEAD_DIM, 2, dtype=f32) / HEAD_DIM))
    ang = jnp.arange(seq, dtype=f32)[:, None] * inv[None, :]
    cos, sin = jnp.cos(ang), jnp.sin(ang)
    reps = LANES // HEAD_DIM
    return (jnp.tile(cos, (1, 2 * reps)), jnp.tile(jnp.concatenate([-sin, sin], axis=1), (1, reps)))


def kernel(x, norm1_g, w_in, b_gate, qn_a, kn_a, lam_q1, lam_k1, lam_q2, lam_k2, subln_g, qn_b, kn_b,
           w_a_proj, w_b_proj, w_out, norm2_g, w_up, conv_w, conv_b, w_down):
    bsz, seq, d = x.shape
    depth = w_in.shape[0]
    assert seq % 512 == 0 and d % LANES == 0
    cos_t, sin_t = _rope_tables(seq)
    seg = jnp.arange(LANES) // HEAD_DIM
    bd = (seg[:, None] == seg[None, :]).astype(bf16)
    reps = LANES // HEAD_DIM
    qscale = HEAD_DIM ** -0.5 * LOG2E
    for l in range(depth):
        lam_init = 0.8 - 0.6 * math.exp(-0.3 * l)
        gains = jnp.stack([jnp.tile(qn_a[l], reps) * qscale, jnp.tile(kn_a[l], reps),
                           jnp.tile(qn_b[l], reps) * qscale, jnp.tile(kn_b[l], reps)])
        q_all, k_all, vt_all, kmean, w_up_bf, w_down_bf = _inproj(
            x, norm1_g[l][None], w_in[l], gains, cos_t, sin_t, bd, (w_up[l], w_down[l]), tm=1024)
        kmean = kmean.reshape(bsz, seq // BLK, B_WIDTH)
        lamvec = jnp.stack([lam_q1[l], lam_k1[l], lam_q2[l], lam_k2[l]])
        gmax = jnp.max(jnp.abs(gains), axis=1) * (HEAD_DIM ** 0.5)
        subg_cols = jnp.tile(subln_g[l][:, None], (1, BLK))
        oa = _attention(q_all, k_all, vt_all, (lamvec, subg_cols),
                        gmax[0] * gmax[1] * BOUND_SLACK, moba=False, lam_init=lam_init)
        ob = _attention(q_all, k_all, vt_all, (kmean,),
                        gmax[2] * gmax[3] * BOUND_SLACK, moba=True, lam_init=lam_init)
        x1, h2 = _merge(oa, ob, x, norm1_g[l][None], w_in[l], b_gate[l][None],
                        w_a_proj[l], w_b_proj[l], w_out[l], norm2_g[l][None], tm=1024)
        dff = w_down.shape[1]
        x = _mlp(h2.reshape(bsz * seq, d), x1.reshape(bsz * seq, d), w_up_bf, conv_w[l],
                 conv_b[l][None], w_down_bf, seq=seq, tm=512).reshape(bsz, seq, d)
    return x
```

```python
import functools
import math

import jax
import jax.numpy as jnp
from jax import lax
from jax.experimental import pallas as pl
from jax.experimental.pallas import tpu as pltpu

HEAD_DIM = 64
A_HEADS = 4
A_VDIM = 2 * HEAD_DIM
A_WIDTH = A_HEADS * A_VDIM
B_HEADS = 8
B_WIDTH = B_HEADS * HEAD_DIM
B_BLOCK = 256
B_TOPK = 3
CONV_W = 3
ROPE_THETA = 10000.0
EPS = 1e-6

LANES = 128
BLK = B_BLOCK
PAIRS = A_WIDTH // LANES
LOG2E = 1.4426950408889634
NEG_BIG = -1e30
POS_BIG = 1e30
VMEM_LIMIT = 56 * 1024 * 1024

f32 = jnp.float32
bf16 = jnp.bfloat16


def _nt_dot(a, b, **kw):
    return lax.dot_general(a, b, (((1,), (1,)), ((), ())), preferred_element_type=f32, **kw)


def _inproj_kernel(x_ref, g1_ref, w_ref, gain_ref, cos_ref, sin_ref, bd_ref, *rest, tm, n_late):
    late_f32 = rest[:n_late]
    q_ref, k_ref, vt_ref, kmean_ref = rest[n_late:n_late + 4]
    late_bf16 = rest[n_late + 4:2 * n_late + 4]
    h_scr = rest[-1]
    for src, dst in zip(late_f32, late_bf16):
        dst[...] = src[...].astype(bf16)
    nblk = tm // BLK
    x = x_ref[0]
    ms = jnp.mean(x * x, axis=-1, keepdims=True)
    h_scr[...] = (x * lax.rsqrt(ms + EPS) * g1_ref[...]).astype(bf16)
    cos = cos_ref[...]
    sin = sin_ref[...]
    bd = bd_ref[...]
    lane = lax.broadcasted_iota(jnp.int32, (tm, LANES), 1)
    first_half = (lane & (HEAD_DIM - 1)) < (HEAD_DIM // 2)
    low_half = lane < HEAD_DIM

    def col_tile(j):
        return jnp.dot(h_scr[...], w_ref[:, j * A_WIDTH:(j + 1) * A_WIDTH].astype(bf16),
                       preferred_element_type=f32)

    def norm_rope(y, gain):
        ss = jnp.dot((y * y).astype(bf16), bd, preferred_element_type=f32)
        yn = y * lax.rsqrt(ss * (1.0 / HEAD_DIM) + EPS) * gain
        rot = jnp.where(first_half, pltpu.roll(yn, LANES - HEAD_DIM // 2, 1),
                        pltpu.roll(yn, HEAD_DIM // 2, 1))
        return yn * cos + rot * sin

    for branch in range(2):
        acc_q = col_tile(3 * branch)
        for c in range(PAIRS):
            out = norm_rope(acc_q[:, c * LANES:(c + 1) * LANES], gain_ref[2 * branch:2 * branch + 1, :])
            lo = jnp.where(low_half, out, 0.0).astype(bf16)
            hi = jnp.where(low_half, 0.0, out).astype(bf16)
            for bi in range(nblk):
                q_ref[0, branch * PAIRS + c, bi, :BLK] = lo[bi * BLK:(bi + 1) * BLK]
                q_ref[0, branch * PAIRS + c, bi, BLK:] = hi[bi * BLK:(bi + 1) * BLK]
        acc_k = col_tile(3 * branch + 1)
        for c in range(PAIRS):
            out = norm_rope(acc_k[:, c * LANES:(c + 1) * LANES],
                            gain_ref[2 * branch + 1:2 * branch + 2, :])
            for bi in range(nblk):
                blk = out[bi * BLK:(bi + 1) * BLK]
                k_ref[0, branch * PAIRS + c, bi] = blk.astype(bf16)
                if branch == 1:
                    kmean_ref[0, bi:bi + 1, c * LANES:(c + 1) * LANES] = jnp.mean(
                        blk, axis=0, keepdims=True)
        acc_v = col_tile(3 * branch + 2)
        for c in range(PAIRS):
            for bi in range(nblk):
                blk = acc_v[bi * BLK:(bi + 1) * BLK, c * LANES:(c + 1) * LANES]
                vt_ref[0, branch * PAIRS + c, bi] = blk.T.astype(bf16)


def _inproj(x, g1, w_in, gains, cos_t, sin_t, bd, late_weights, *, tm):
    bsz, seq, d = x.shape
    steps = bsz * (seq // tm)
    slab = lambda b, i: (b * (seq // tm) + i, 0)
    late_specs = [pl.BlockSpec((w.shape[0] // steps, w.shape[1]), slab) for w in late_weights]
    assert all(w.shape[0] % (steps * 16) == 0 for w in late_weights)
    n_cols = 3 * A_WIDTH + 3 * B_WIDTH
    nblk = tm // BLK
    spt = seq // tm
    nb = seq // BLK
    q_shape = jax.ShapeDtypeStruct((bsz, 2 * PAIRS, nb, 2 * BLK, LANES), bf16)
    qk_shape = jax.ShapeDtypeStruct((bsz, 2 * PAIRS, nb, BLK, LANES), bf16)
    vt_shape = jax.ShapeDtypeStruct((bsz, 2 * PAIRS, nb, LANES, BLK), bf16)
    q_spec = pl.BlockSpec((1, 2 * PAIRS, nblk, 2 * BLK, LANES), lambda b, i: (b, 0, i, 0, 0))
    qk_spec = pl.BlockSpec((1, 2 * PAIRS, nblk, BLK, LANES), lambda b, i: (b, 0, i, 0, 0))
    vt_spec = pl.BlockSpec((1, 2 * PAIRS, nblk, LANES, BLK), lambda b, i: (b, 0, i, 0, 0))
    return pl.pallas_call(
        functools.partial(_inproj_kernel, tm=tm, n_late=len(late_weights)),
        grid=(bsz, spt),
        in_specs=[
            pl.BlockSpec((1, tm, d), lambda b, i: (b, i, 0)),
            pl.BlockSpec((1, d), lambda b, i: (0, 0)),
            pl.BlockSpec((d, n_cols), lambda b, i: (0, 0), pipeline_mode=pl.Buffered(1)),
            pl.BlockSpec((4, LANES), lambda b, i: (0, 0)),
            pl.BlockSpec((tm, LANES), lambda b, i: (i, 0)),
            pl.BlockSpec((tm, LANES), lambda b, i: (i, 0)),
            pl.BlockSpec((LANES, LANES), lambda b, i: (0, 0)),
        ] + late_specs,
        out_specs=[
            q_spec, qk_spec, vt_spec,
            pl.BlockSpec((1, nblk, B_WIDTH), lambda b, i: (b * spt + i, 0, 0)),
        ] + late_specs,
        out_shape=[
            q_shape, qk_shape, vt_shape,
            jax.ShapeDtypeStruct((bsz * spt, nblk, B_WIDTH), f32),
        ] + [jax.ShapeDtypeStruct(w.shape, bf16) for w in late_weights],
        scratch_shapes=[pltpu.VMEM((tm, d), bf16)],
        compiler_params=pltpu.CompilerParams(
            dimension_semantics=("arbitrary", "arbitrary"), vmem_limit_bytes=VMEM_LIMIT),
        name="inproj",
    )(x, g1, w_in, gains, cos_t, sin_t, bd, *late_weights)


TRIP_STEPS = 68
QK_AHEAD = {False: 6, True: 6}
TILE_UNROLL = 4
BOUND_SLACK = 1.0 + 2.0 ** -6
MAX_FIXED_BOUND = 60.0


def _attn_kernel(itab_ref, jtab_ref, otab_ref, *refs, moba, online, lam_init, n_steps, n_masked):
    if moba:
        (q_ref, k_ref, vt_ref, mref_ref, kmean_ref, o_ref,
         m_scr, l_scr, acc_scr, s_scr, sel_scr) = refs
    else:
        (q_ref, k_ref, vt_ref, mref_ref, lam_ref, subg_ref, o_ref,
         m_scr, l_scr, acc_scr, s_scr) = refs
    nb = k_ref.shape[2]
    acc_rows = acc_scr.shape[1]
    key_minus_row = (lax.broadcasted_iota(jnp.int32, (BLK, 2 * BLK), 0)
                     - (lax.broadcasted_iota(jnp.int32, (BLK, 2 * BLK), 1) & (BLK - 1)))

    if moba:
        km = kmean_ref[0]
        km_parts = []
        for _ in range(3):
            part = km.astype(bf16)
            km_parts.append(part)
            km = km - part.astype(f32)

    nidx = lax.broadcasted_iota(jnp.int32, (nb, 2 * BLK), 0)
    for i in range(nb):
        if online:
            m_scr[i] = jnp.full((1, 2 * BLK), NEG_BIG, f32)
        l_scr[i] = jnp.zeros((1, 2 * BLK), f32)
        acc_scr[i] = jnp.zeros((acc_rows, 2 * BLK), f32)
        if moba:
            keep = nidx <= i
            if i > B_TOPK:
                qs = q_ref[0, 0, i]
                g = _nt_dot(km_parts[0], qs) + _nt_dot(km_parts[1], qs) + _nt_dot(km_parts[2], qs)
                cnt = jnp.zeros((nb, 2 * BLK), f32)
                for mm in range(i):
                    row = g[mm:mm + 1, :]
                    beats = (row > g) | ((row == g) & (nidx > mm))
                    cnt = cnt + jnp.where(beats, 1.0, 0.0)
                keep = ((cnt < float(B_TOPK)) & (nidx < i)) | (nidx == i)
            sel_scr[i] = jnp.where(keep, 1.0, 0.0)

    def qk(step, slot, u):
        s = _nt_dot(k_ref[0, 0, jtab_ref[step]], q_ref[0, 0, itab_ref[step]])
        if slot == 0 and u < n_masked:
            s = jnp.where(key_minus_row <= otab_ref[step], s, NEG_BIG)
        s_scr[slot, u] = s

    def update(step, slot, u):
        i = itab_ref[step]
        j = jtab_ref[step]
        s = s_scr[slot, u]
        if moba:
            chosen = sel_scr[i, pl.ds(j, 1), :] > 0.5
        if online:
            m_old = m_scr[i]
            m_new = jnp.maximum(m_old, jnp.max(s, axis=0, keepdims=True))
            if moba:
                m_new = jnp.where(chosen, m_new, m_old)
            m_sub = m_new
            alpha = jnp.exp2(m_old - m_new)
            m_scr[i] = m_new
        else:
            m_sub = mref_ref[...]
        if moba:
            m_sub = jnp.where(chosen, m_sub, POS_BIG)
        p = jnp.exp2(s - m_sub)
        psum = jnp.sum(p, axis=0, keepdims=True)
        pb = p.astype(bf16)
        vt = vt_ref[0, 0, j]
        pv = jnp.dot(vt, pb, preferred_element_type=f32)
        if online:
            l_scr[i] = alpha * l_scr[i] + psum
            acc_scr[i] = alpha * acc_scr[i] + pv
        else:
            l_scr[i] = l_scr[i] + psum
            acc_scr[i] = acc_scr[i] + pv

    ahead = min(QK_AHEAD[moba], TRIP_STEPS)

    def trip(it, carry):
        base = it * TRIP_STEPS
        for u in range(ahead):
            qk(base + u, 0, u)
        for u in range(TRIP_STEPS):
            if u + ahead < TRIP_STEPS:
                qk(base + u + ahead, 0, u + ahead)
            update(base + u, 0, u)
        return carry

    lax.fori_loop(0, n_steps // TRIP_STEPS, trip, 0)

    if not moba:
        lv = lam_ref[...]
        lam = (jnp.exp(jnp.sum(lv[0:1] * lv[1:2], axis=-1, keepdims=True))
               - jnp.exp(jnp.sum(lv[2:3] * lv[3:4], axis=-1, keepdims=True)) + lam_init)

    def finish_tiles(t, carry):
        for r in range(TILE_UNROLL):
            i = t * TILE_UNROLL + r
            o = acc_scr[i] * (1.0 / l_scr[i])
            rows = pl.ds(pl.multiple_of(i * BLK, BLK), BLK)
            if moba:
                ot = jnp.concatenate([o[:HEAD_DIM, :BLK], o[HEAD_DIM:, BLK:]], axis=0).T
                o_ref[0, rows, :] = ot.astype(bf16)
            else:
                od = o[:, :BLK] - lam * o[:, BLK:]
                ms = jnp.mean(od * od, axis=0, keepdims=True)
                od = od * lax.rsqrt(ms + EPS) * subg_ref[...] * (1.0 - lam_init)
                o_ref[0, rows, :] = od.T.astype(bf16)
        return carry

    lax.fori_loop(0, nb // TILE_UNROLL, finish_tiles, 0)


def _step_tables(nb):
    per_trip = TRIP_STEPS
    diag = [(i, i, 0) for i in range(nb)]
    rest = [(i, j, BLK) for j in range(nb) for i in range(j + 1, nb)]
    n_steps = len(diag) + len(rest)
    assert n_steps % per_trip == 0
    n_trips = n_steps // per_trip
    n_masked = -(-len(diag) // n_trips)
    assert n_masked <= per_trip
    steps = []
    for _ in range(n_trips):
        for _ in range(n_masked):
            steps.append(diag.pop(0) if diag else rest.pop(0))
        for _ in range(per_trip - n_masked):
            steps.append(rest.pop(0))
    assert not diag and not rest
    return [jnp.asarray([s[c] for s in steps], jnp.int32) for c in range(3)], n_steps, n_masked


def _attention(q_all, k_all, vt_all, aux, score_bound, *, moba, lam_init):
    bsz, _, nb, _, _ = k_all.shape
    off = PAIRS if moba else 0
    assert nb % TILE_UNROLL == 0
    (itab, jtab, otab), n_steps, n_masked = _step_tables(nb)
    const = lambda b, h, it, jt, ot: (0, 0)
    in_specs = [
        pl.BlockSpec((1, 1, nb, 2 * BLK, LANES), lambda b, h, it, jt, ot: (b, h + off, 0, 0, 0)),
        pl.BlockSpec((1, 1, nb, BLK, LANES), lambda b, h, it, jt, ot: (b, h + off, 0, 0, 0)),
        pl.BlockSpec((1, 1, nb, LANES, BLK), lambda b, h, it, jt, ot: (b, h + off, 0, 0, 0)),
        pl.BlockSpec((1, 2 * BLK), const),
    ]
    scratch = [pltpu.VMEM((nb, 1, 2 * BLK), f32), pltpu.VMEM((nb, 1, 2 * BLK), f32),
               pltpu.VMEM((nb, LANES, 2 * BLK), f32),
               pltpu.VMEM((1, TRIP_STEPS, BLK, 2 * BLK), f32)]
    if moba:
        (kmean,) = aux
        in_specs.append(pl.BlockSpec((1, nb, LANES), lambda b, h, it, jt, ot: (b, 0, h)))
        scratch.append(pltpu.VMEM((nb, nb, 2 * BLK), f32))
    else:
        lamvec, subg = aux
        in_specs.append(pl.BlockSpec(lamvec.shape, const))
        in_specs.append(pl.BlockSpec(subg.shape, const))
    mref = jnp.full((1, 2 * BLK), score_bound, f32)

    def run(online):
        return pl.pallas_call(
            functools.partial(_attn_kernel, moba=moba, online=online, lam_init=lam_init,
                              n_steps=n_steps, n_masked=n_masked),
            grid_spec=pltpu.PrefetchScalarGridSpec(
                num_scalar_prefetch=3,
                grid=(bsz, PAIRS),
                in_specs=in_specs,
                out_specs=pl.BlockSpec((1, nb * BLK, LANES), lambda b, h, it, jt, ot: (b, 0, h)),
                scratch_shapes=scratch),
            out_shape=jax.ShapeDtypeStruct((bsz, nb * BLK, PAIRS * LANES), bf16),
            compiler_params=pltpu.CompilerParams(
                dimension_semantics=("arbitrary", "arbitrary"), vmem_limit_bytes=VMEM_LIMIT),
            name=("moba_attn" if moba else "diff_attn") + ("_online" if online else ""),
        )(itab, jtab, otab, q_all, k_all, vt_all, mref, *aux)

    return lax.cond(score_bound <= MAX_FIXED_BOUND, lambda: run(False), lambda: run(True))


def _merge_kernel(oa_ref, ob_ref, x_ref, g1_ref, wga_ref, wgb_ref, bg_ref, wa_ref, wb_ref, wo_ref,
                  g2_ref, x1_ref, h2_ref):
    d = x_ref.shape[-1]
    x = x_ref[0]
    ms = jnp.mean(x * x, axis=-1, keepdims=True)
    h = (x * lax.rsqrt(ms + EPS) * g1_ref[...]).astype(bf16)
    merged = None
    for br, (wg_ref, o_ref, w_ref) in enumerate(((wga_ref, oa_ref, wa_ref), (wgb_ref, ob_ref, wb_ref))):
        gate = jax.nn.sigmoid(jnp.dot(h, wg_ref[...].astype(bf16), preferred_element_type=f32)
                              + bg_ref[:, br * d:(br + 1) * d])
        term = gate * jnp.dot(o_ref[0], w_ref[...].astype(bf16), preferred_element_type=f32)
        merged = term if merged is None else merged + term
    x1 = x + jnp.dot(merged.astype(bf16), wo_ref[...].astype(bf16), preferred_element_type=f32)
    x1_ref[0] = x1
    ms = jnp.mean(x1 * x1, axis=-1, keepdims=True)
    h2_ref[0] = (x1 * lax.rsqrt(ms + EPS) * g2_ref[...]).astype(bf16)


def _merge(oa, ob, x, g1, w_in, bg, wa, wb, wo, g2, *, tm):
    bsz, seq, d = x.shape
    gate_blk = (w_in.shape[1] - 2 * d) // d
    assert gate_blk * d + 2 * d == w_in.shape[1]
    const = lambda b, i: (0, 0)
    once = pl.Buffered(1)
    tile = lambda b, i: (b, i, 0)
    return pl.pallas_call(
        _merge_kernel,
        grid=(bsz, seq // tm),
        in_specs=[
            pl.BlockSpec((1, tm, oa.shape[-1]), tile),
            pl.BlockSpec((1, tm, ob.shape[-1]), tile),
            pl.BlockSpec((1, tm, d), tile),
            pl.BlockSpec((1, d), const),
            pl.BlockSpec((d, d), lambda b, i: (0, gate_blk), pipeline_mode=once),
            pl.BlockSpec((d, d), lambda b, i: (0, gate_blk + 1), pipeline_mode=once),
            pl.BlockSpec(bg.shape, const),
            pl.BlockSpec(wa.shape, const, pipeline_mode=once),
            pl.BlockSpec(wb.shape, const, pipeline_mode=once),
            pl.BlockSpec(wo.shape, const, pipeline_mode=once),
            pl.BlockSpec((1, d), const),
        ],
        out_specs=[pl.BlockSpec((1, tm, d), tile), pl.BlockSpec((1, tm, d), tile)],
        out_shape=[jax.ShapeDtypeStruct((bsz, seq, d), f32),
                   jax.ShapeDtypeStruct((bsz, seq, d), bf16)],
        compiler_params=pltpu.CompilerParams(
            dimension_semantics=("arbitrary", "arbitrary"), vmem_limit_bytes=VMEM_LIMIT),
        name="merge_outproj",
    )(oa, ob, x, g1, w_in, w_in, bg, wa, wb, wo, g2)


HALO = 8
assert HALO >= CONV_W - 1


def _mlp_kernel(h2_ref, x1_ref, wup_ref, cw_ref, cb_ref, wd_ref, o_ref, u_scr, *, tm, tiles_per_seq):
    i = pl.program_id(0)
    dff = wd_ref.shape[0]

    @pl.when(i % tiles_per_seq == 0)
    def _():
        u_scr[:HALO, :] = jnp.zeros((HALO, dff), f32)

    h2 = h2_ref[...]
    u = jnp.dot(h2, wup_ref[:, :dff], preferred_element_type=f32)
    g = jnp.dot(h2, wup_ref[:, dff:], preferred_element_type=f32)
    u_scr[HALO:, :] = u
    y = cb_ref[...]
    for j in range(CONV_W):
        y = y + u_scr[pl.ds(HALO - (CONV_W - 1) + j, tm), :] * cw_ref[j:j + 1, :]
    u_scr[:HALO, :] = u[tm - HALO:, :]
    act = (jax.nn.gelu(y) * g).astype(bf16)
    o_ref[...] = x1_ref[...] + jnp.dot(act, wd_ref[...], preferred_element_type=f32)


def _mlp(h2, x1, w_up_bf, conv_w, conv_b, w_down_bf, *, seq, tm):
    n, d = h2.shape
    dff = w_down_bf.shape[0]
    const = lambda i: (0, 0)
    return pl.pallas_call(
        functools.partial(_mlp_kernel, tm=tm, tiles_per_seq=seq // tm),
        grid=(n // tm,),
        in_specs=[
            pl.BlockSpec((tm, d), lambda i: (i, 0)),
            pl.BlockSpec((tm, d), lambda i: (i, 0)),
            pl.BlockSpec(w_up_bf.shape, const, pipeline_mode=pl.Buffered(1)),
            pl.BlockSpec(conv_w.shape, const),
            pl.BlockSpec(conv_b.shape, const),
            pl.BlockSpec(w_down_bf.shape, const, pipeline_mode=pl.Buffered(1)),
        ],
        out_specs=pl.BlockSpec((tm, d), lambda i: (i, 0)),
        out_shape=jax.ShapeDtypeStruct((n, d), f32),
        scratch_shapes=[pltpu.VMEM((tm + HALO, dff), f32)],
        compiler_params=pltpu.CompilerParams(
            dimension_semantics=("arbitrary",), vmem_limit_bytes=VMEM_LIMIT),
        name="conv_mlp",
    )(h2, x1, w_up_bf, conv_w, conv_b, w_down_bf)


def _rope_tables(seq):
    inv = 1.0 / (ROPE_THETA ** (jnp.arange(0, HEAD_DIM, 2, dtype=f32) / HEAD_DIM))
    ang = jnp.arange(seq, dtype=f32)[:, None] * inv[None, :]
    cos, sin = jnp.cos(ang), jnp.sin(ang)
    reps = LANES // HEAD_DIM
    return (jnp.tile(cos, (1, 2 * reps)), jnp.tile(jnp.concatenate([-sin, sin], axis=1), (1, reps)))


def kernel(x, norm1_g, w_in, b_gate, qn_a, kn_a, lam_q1, lam_k1, lam_q2, lam_k2, subln_g, qn_b, kn_b,
           w_a_proj, w_b_proj, w_out, norm2_g, w_up, conv_w, conv_b, w_down):
    bsz, seq, d = x.shape
    depth = w_in.shape[0]
    assert seq % 512 == 0 and d % LANES == 0
    cos_t, sin_t = _rope_tables(seq)
    seg = jnp.arange(LANES) // HEAD_DIM
    bd = (seg[:, None] == seg[None, :]).astype(bf16)
    reps = LANES // HEAD_DIM
    qscale = HEAD_DIM ** -0.5 * LOG2E
    for l in range(depth):
        lam_init = 0.8 - 0.6 * math.exp(-0.3 * l)
        gains = jnp.stack([jnp.tile(qn_a[l], reps) * qscale, jnp.tile(kn_a[l], reps),
                           jnp.tile(qn_b[l], reps) * qscale, jnp.tile(kn_b[l], reps)])
        q_all, k_all, vt_all, kmean, w_up_bf, w_down_bf = _inproj(
            x, norm1_g[l][None], w_in[l], gains, cos_t, sin_t, bd, (w_up[l], w_down[l]), tm=1024)
        kmean = kmean.reshape(bsz, seq // BLK, B_WIDTH)
        lamvec = jnp.stack([lam_q1[l], lam_k1[l], lam_q2[l], lam_k2[l]])
        gmax = jnp.max(jnp.abs(gains), axis=1) * (HEAD_DIM ** 0.5)
        subg_cols = jnp.tile(subln_g[l][:, None], (1, BLK))
        oa = _attention(q_all, k_all, vt_all, (lamvec, subg_cols),
                        gmax[0] * gmax[1] * BOUND_SLACK, moba=False, lam_init=lam_init)
        ob = _attention(q_all, k_all, vt_all, (kmean,),
                        gmax[2] * gmax[3] * BOUND_SLACK, moba=True, lam_init=lam_init)
        x1, h2 = _merge(oa, ob, x, norm1_g[l][None], w_in[l], b_gate[l][None],
                        w_a_proj[l], w_b_proj[l], w_out[l], norm2_g[l][None], tm=1024)
        dff = w_down.shape[1]
        x = _mlp(h2.reshape(bsz * seq, d), x1.reshape(bsz * seq, d), w_up_bf, conv_w[l],
                 conv_b[l][None], w_down_bf, seq=seq, tm=512).reshape(bsz, seq, d)
    return x
```

```python
import functools
import math

import jax
import jax.numpy as jnp
from jax import lax
from jax.experimental import pallas as pl
from jax.experimental.pallas import tpu as pltpu

HEAD_DIM = 64
A_HEADS = 4
A_VDIM = 2 * HEAD_DIM
A_WIDTH = A_HEADS * A_VDIM
B_HEADS = 8
B_WIDTH = B_HEADS * HEAD_DIM
B_BLOCK = 256
B_TOPK = 3
CONV_W = 3
ROPE_THETA = 10000.0
EPS = 1e-6

LANES = 128
BLK = B_BLOCK
PAIRS = A_WIDTH // LANES
LOG2E = 1.4426950408889634
NEG_BIG = -1e30
POS_BIG = 1e30
VMEM_LIMIT = 56 * 1024 * 1024
BF16_SUBLANES = 16
TM_INPROJ = 1024
TM_MERGE = 1024
TM_MLP = 512

f32 = jnp.float32
bf16 = jnp.bfloat16


def _nt_dot(a, b, **kw):
    return lax.dot_general(a, b, (((1,), (1,)), ((), ())), preferred_element_type=f32, **kw)


def _inproj_kernel(x_ref, g1_ref, w_ref, gain_ref, cos_ref, sin_ref, bd_ref, *rest, tm, n_late):
    late_f32 = rest[:n_late]
    q_ref, k_ref, vt_ref, kmean_ref = rest[n_late:n_late + 4]
    late_bf16 = rest[n_late + 4:2 * n_late + 4]
    h_scr = rest[-1]
    for src, dst in zip(late_f32, late_bf16):
        dst[...] = src[...].astype(bf16)
    nblk = tm // BLK
    x = x_ref[0]
    ms = jnp.mean(x * x, axis=-1, keepdims=True)
    h_scr[...] = (x * lax.rsqrt(ms + EPS) * g1_ref[...]).astype(bf16)
    cos = cos_ref[...]
    sin = sin_ref[...]
    bd = bd_ref[...]
    lane = lax.broadcasted_iota(jnp.int32, (tm, LANES), 1)
    first_half = (lane & (HEAD_DIM - 1)) < (HEAD_DIM // 2)
    low_half = lane < HEAD_DIM

    def col_tile(j):
        return jnp.dot(h_scr[...], w_ref[:, j * A_WIDTH:(j + 1) * A_WIDTH].astype(bf16),
                       preferred_element_type=f32)

    def norm_rope(y, gain):
        ss = jnp.dot((y * y).astype(bf16), bd, preferred_element_type=f32)
        yn = y * lax.rsqrt(ss * (1.0 / HEAD_DIM) + EPS) * gain
        rot = jnp.where(first_half, pltpu.roll(yn, LANES - HEAD_DIM // 2, 1),
                        pltpu.roll(yn, HEAD_DIM // 2, 1))
        return yn * cos + rot * sin

    for branch in range(2):
        acc_q = col_tile(3 * branch)
        for c in range(PAIRS):
            out = norm_rope(acc_q[:, c * LANES:(c + 1) * LANES], gain_ref[2 * branch:2 * branch + 1, :])
            lo = jnp.where(low_half, out, 0.0).astype(bf16)
            hi = jnp.where(low_half, 0.0, out).astype(bf16)
            for bi in range(nblk):
                q_ref[0, branch * PAIRS + c, bi, :BLK] = lo[bi * BLK:(bi + 1) * BLK]
                q_ref[0, branch * PAIRS + c, bi, BLK:] = hi[bi * BLK:(bi + 1) * BLK]
        acc_k = col_tile(3 * branch + 1)
        for c in range(PAIRS):
            out = norm_rope(acc_k[:, c * LANES:(c + 1) * LANES],
                            gain_ref[2 * branch + 1:2 * branch + 2, :])
            for bi in range(nblk):
                blk = out[bi * BLK:(bi + 1) * BLK]
                k_ref[0, branch * PAIRS + c, bi] = blk.astype(bf16)
                if branch == 1:
                    kmean_ref[0, bi:bi + 1, c * LANES:(c + 1) * LANES] = jnp.mean(
                        blk, axis=0, keepdims=True)
        acc_v = col_tile(3 * branch + 2)
        for c in range(PAIRS):
            for bi in range(nblk):
                blk = acc_v[bi * BLK:(bi + 1) * BLK, c * LANES:(c + 1) * LANES]
                vt_ref[0, branch * PAIRS + c, bi] = blk.T.astype(bf16)


def _inproj(x, g1, w_in, gains, cos_t, sin_t, bd, late_weights, *, tm):
    bsz, seq, d = x.shape
    steps = bsz * (seq // tm)
    slab = lambda b, i: (b * (seq // tm) + i, 0)
    late_specs = [pl.BlockSpec((w.shape[0] // steps, w.shape[1]), slab) for w in late_weights]
    assert all(w.shape[0] % (steps * BF16_SUBLANES) == 0 for w in late_weights)
    n_cols = 3 * A_WIDTH + 3 * B_WIDTH
    nblk = tm // BLK
    spt = seq // tm
    nb = seq // BLK
    q_shape = jax.ShapeDtypeStruct((bsz, 2 * PAIRS, nb, 2 * BLK, LANES), bf16)
    qk_shape = jax.ShapeDtypeStruct((bsz, 2 * PAIRS, nb, BLK, LANES), bf16)
    vt_shape = jax.ShapeDtypeStruct((bsz, 2 * PAIRS, nb, LANES, BLK), bf16)
    q_spec = pl.BlockSpec((1, 2 * PAIRS, nblk, 2 * BLK, LANES), lambda b, i: (b, 0, i, 0, 0))
    qk_spec = pl.BlockSpec((1, 2 * PAIRS, nblk, BLK, LANES), lambda b, i: (b, 0, i, 0, 0))
    vt_spec = pl.BlockSpec((1, 2 * PAIRS, nblk, LANES, BLK), lambda b, i: (b, 0, i, 0, 0))
    return pl.pallas_call(
        functools.partial(_inproj_kernel, tm=tm, n_late=len(late_weights)),
        grid=(bsz, spt),
        in_specs=[
            pl.BlockSpec((1, tm, d), lambda b, i: (b, i, 0)),
            pl.BlockSpec((1, d), lambda b, i: (0, 0)),
            pl.BlockSpec((d, n_cols), lambda b, i: (0, 0), pipeline_mode=pl.Buffered(1)),
            pl.BlockSpec((4, LANES), lambda b, i: (0, 0)),
            pl.BlockSpec((tm, LANES), lambda b, i: (i, 0)),
            pl.BlockSpec((tm, LANES), lambda b, i: (i, 0)),
            pl.BlockSpec((LANES, LANES), lambda b, i: (0, 0)),
        ] + late_specs,
        out_specs=[
            q_spec, qk_spec, vt_spec,
            pl.BlockSpec((1, nblk, B_WIDTH), lambda b, i: (b * spt + i, 0, 0)),
        ] + late_specs,
        out_shape=[
            q_shape, qk_shape, vt_shape,
            jax.ShapeDtypeStruct((bsz * spt, nblk, B_WIDTH), f32),
        ] + [jax.ShapeDtypeStruct(w.shape, bf16) for w in late_weights],
        scratch_shapes=[pltpu.VMEM((tm, d), bf16)],
        compiler_params=pltpu.CompilerParams(
            dimension_semantics=("arbitrary", "arbitrary"), vmem_limit_bytes=VMEM_LIMIT),
        name="inproj",
    )(x, g1, w_in, gains, cos_t, sin_t, bd, *late_weights)


TRIP_STEPS = 34
QK_AHEAD = 6
TILE_UNROLL = 4
BOUND_SLACK = 1.0 + 2.0 ** -6
MAX_FIXED_BOUND = 60.0


def _attn_kernel(itab_ref, jtab_ref, otab_ref, *refs, moba, online, lam_init, n_steps, n_masked):
    if moba:
        (q_ref, k_ref, vt_ref, mref_ref, kmean_ref, o_ref,
         m_scr, l_scr, acc_scr, s_scr, sel_scr) = refs
    else:
        (q_ref, k_ref, vt_ref, mref_ref, lam_ref, subg_ref, o_ref,
         m_scr, l_scr, acc_scr, s_scr) = refs
    nb = k_ref.shape[2]
    acc_rows = acc_scr.shape[1]
    key_minus_row = (lax.broadcasted_iota(jnp.int32, (BLK, 2 * BLK), 0)
                     - (lax.broadcasted_iota(jnp.int32, (BLK, 2 * BLK), 1) & (BLK - 1)))

    if moba:
        km = kmean_ref[0]
        km_parts = []
        for _ in range(3):
            part = km.astype(bf16)
            km_parts.append(part)
            km = km - part.astype(f32)

    nidx = lax.broadcasted_iota(jnp.int32, (nb, 2 * BLK), 0)
    for i in range(nb):
        if online:
            m_scr[i] = jnp.full((1, 2 * BLK), NEG_BIG, f32)
        l_scr[i] = jnp.zeros((1, 2 * BLK), f32)
        acc_scr[i] = jnp.zeros((acc_rows, 2 * BLK), f32)
        if moba:
            keep = nidx <= i
            if i > B_TOPK:
                qs = q_ref[0, 0, i]
                g = _nt_dot(km_parts[0], qs) + _nt_dot(km_parts[1], qs) + _nt_dot(km_parts[2], qs)
                cnt = jnp.zeros((nb, 2 * BLK), f32)
                for mm in range(i):
                    row = g[mm:mm + 1, :]
                    beats = (row > g) | ((row == g) & (nidx > mm))
                    cnt = cnt + jnp.where(beats, 1.0, 0.0)
                keep = ((cnt < float(B_TOPK)) & (nidx < i)) | (nidx == i)
            sel_scr[i] = jnp.where(keep, 1.0, 0.0)

    ring = s_scr.shape[0]

    def qk(step, u):
        s = _nt_dot(k_ref[0, 0, jtab_ref[step]], q_ref[0, 0, itab_ref[step]])
        if u < n_masked:
            s = jnp.where(key_minus_row <= otab_ref[step], s, NEG_BIG)
        s_scr[u % ring] = s

    def update(step, u):
        i = itab_ref[step]
        j = jtab_ref[step]
        s = s_scr[u % ring]
        if moba:
            chosen = sel_scr[i, pl.ds(j, 1), :] > 0.5
        if online:
            m_old = m_scr[i]
            m_new = jnp.maximum(m_old, jnp.max(s, axis=0, keepdims=True))
            if moba:
                m_new = jnp.where(chosen, m_new, m_old)
            m_sub = m_new
            alpha = jnp.exp2(m_old - m_new)
            m_scr[i] = m_new
        else:
            m_sub = mref_ref[...]
        if moba:
            m_sub = jnp.where(chosen, m_sub, POS_BIG)
        p = jnp.exp2(s - m_sub)
        psum = jnp.sum(p, axis=0, keepdims=True)
        pb = p.astype(bf16)
        vt = vt_ref[0, 0, j]
        pv = jnp.dot(vt, pb, preferred_element_type=f32)
        if online:
            l_scr[i] = alpha * l_scr[i] + psum
            acc_scr[i] = alpha * acc_scr[i] + pv
        else:
            l_scr[i] = l_scr[i] + psum
            acc_scr[i] = acc_scr[i] + pv

    ahead = ring - 1

    def trip(it, carry):
        base = it * TRIP_STEPS
        for u in range(ahead):
            qk(base + u, u)
        for u in range(TRIP_STEPS):
            if u + ahead < TRIP_STEPS:
                qk(base + u + ahead, u + ahead)
            update(base + u, u)
        return carry

    lax.fori_loop(0, n_steps // TRIP_STEPS, trip, 0)

    if not moba:
        lv = lam_ref[...]
        lam = (jnp.exp(jnp.sum(lv[0:1] * lv[1:2], axis=-1, keepdims=True))
               - jnp.exp(jnp.sum(lv[2:3] * lv[3:4], axis=-1, keepdims=True)) + lam_init)

    def finish_tiles(t, carry):
        for r in range(TILE_UNROLL):
            i = t * TILE_UNROLL + r
            o = acc_scr[i] * (1.0 / l_scr[i])
            rows = pl.ds(pl.multiple_of(i * BLK, BLK), BLK)
            if moba:
                ot = jnp.concatenate([o[:HEAD_DIM, :BLK], o[HEAD_DIM:, BLK:]], axis=0).T
                o_ref[0, rows, :] = ot.astype(bf16)
            else:
                od = o[:, :BLK] - lam * o[:, BLK:]
                ms = jnp.mean(od * od, axis=0, keepdims=True)
                od = od * lax.rsqrt(ms + EPS) * subg_ref[...] * (1.0 - lam_init)
                o_ref[0, rows, :] = od.T.astype(bf16)
        return carry

    lax.fori_loop(0, nb // TILE_UNROLL, finish_tiles, 0)


def _step_tables(nb):
    per_trip = TRIP_STEPS
    diag = [(i, i, 0) for i in range(nb)]
    rest = [(i, j, BLK) for j in range(nb) for i in range(j + 1, nb)]
    n_steps = len(diag) + len(rest)
    assert n_steps % per_trip == 0
    n_trips = n_steps // per_trip
    n_masked = -(-len(diag) // n_trips)
    assert n_masked <= per_trip
    steps = []
    for _ in range(n_trips):
        for _ in range(n_masked):
            steps.append(diag.pop(0) if diag else rest.pop(0))
        for _ in range(per_trip - n_masked):
            steps.append(rest.pop(0))
    assert not diag and not rest
    return [jnp.asarray([s[c] for s in steps], jnp.int32) for c in range(3)], n_steps, n_masked


def _attention(q_all, k_all, vt_all, aux, score_bound, *, moba, lam_init):
    bsz, _, nb, _, _ = k_all.shape
    off = PAIRS if moba else 0
    assert nb % TILE_UNROLL == 0
    (itab, jtab, otab), n_steps, n_masked = _step_tables(nb)
    const = lambda b, h, it, jt, ot: (0, 0)
    in_specs = [
        pl.BlockSpec((1, 1, nb, 2 * BLK, LANES), lambda b, h, it, jt, ot: (b, h + off, 0, 0, 0)),
        pl.BlockSpec((1, 1, nb, BLK, LANES), lambda b, h, it, jt, ot: (b, h + off, 0, 0, 0)),
        pl.BlockSpec((1, 1, nb, LANES, BLK), lambda b, h, it, jt, ot: (b, h + off, 0, 0, 0)),
        pl.BlockSpec((1, 2 * BLK), const),
    ]
    scratch = [pltpu.VMEM((nb, 1, 2 * BLK), f32), pltpu.VMEM((nb, 1, 2 * BLK), f32),
               pltpu.VMEM((nb, LANES, 2 * BLK), f32),
               pltpu.VMEM((QK_AHEAD + 1, BLK, 2 * BLK), f32)]
    if moba:
        (kmean,) = aux
        in_specs.append(pl.BlockSpec((1, nb, LANES), lambda b, h, it, jt, ot: (b, 0, h)))
        scratch.append(pltpu.VMEM((nb, nb, 2 * BLK), f32))
    else:
        lamvec, subg = aux
        in_specs.append(pl.BlockSpec(lamvec.shape, const))
        in_specs.append(pl.BlockSpec(subg.shape, const))
    mref = jnp.full((1, 2 * BLK), score_bound, f32)

    def run(online):
        return pl.pallas_call(
            functools.partial(_attn_kernel, moba=moba, online=online, lam_init=lam_init,
                              n_steps=n_steps, n_masked=n_masked),
            grid_spec=pltpu.PrefetchScalarGridSpec(
                num_scalar_prefetch=3,
                grid=(bsz, PAIRS),
                in_specs=in_specs,
                out_specs=pl.BlockSpec((1, nb * BLK, LANES), lambda b, h, it, jt, ot: (b, 0, h)),
                scratch_shapes=scratch),
            out_shape=jax.ShapeDtypeStruct((bsz, nb * BLK, PAIRS * LANES), bf16),
            compiler_params=pltpu.CompilerParams(
                dimension_semantics=("arbitrary", "arbitrary"), vmem_limit_bytes=VMEM_LIMIT),
            name=("moba_attn" if moba else "diff_attn") + ("_online" if online else ""),
        )(itab, jtab, otab, q_all, k_all, vt_all, mref, *aux)

    return lax.cond(score_bound <= MAX_FIXED_BOUND, lambda: run(False), lambda: run(True))


def _merge_kernel(oa_ref, ob_ref, x_ref, g1_ref, wga_ref, wgb_ref, bg_ref, wa_ref, wb_ref, wo_ref,
                  g2_ref, x1_ref, h2_ref):
    d = x_ref.shape[-1]
    x = x_ref[0]
    ms = jnp.mean(x * x, axis=-1, keepdims=True)
    h = (x * lax.rsqrt(ms + EPS) * g1_ref[...]).astype(bf16)
    merged = None
    for br, (wg_ref, o_ref, w_ref) in enumerate(((wga_ref, oa_ref, wa_ref), (wgb_ref, ob_ref, wb_ref))):
        gate = jax.nn.sigmoid(jnp.dot(h, wg_ref[...].astype(bf16), preferred_element_type=f32)
                              + bg_ref[:, br * d:(br + 1) * d])
        term = gate * jnp.dot(o_ref[0], w_ref[...].astype(bf16), preferred_element_type=f32)
        merged = term if merged is None else merged + term
    x1 = x + jnp.dot(merged.astype(bf16), wo_ref[...].astype(bf16), preferred_element_type=f32)
    x1_ref[0] = x1
    ms = jnp.mean(x1 * x1, axis=-1, keepdims=True)
    h2_ref[0] = (x1 * lax.rsqrt(ms + EPS) * g2_ref[...]).astype(bf16)


def _merge(oa, ob, x, g1, w_in, bg, wa, wb, wo, g2, *, tm):
    bsz, seq, d = x.shape
    gate_blk = (w_in.shape[1] - 2 * d) // d
    assert gate_blk * d + 2 * d == w_in.shape[1]
    const = lambda b, i: (0, 0)
    once = pl.Buffered(1)
    tile = lambda b, i: (b, i, 0)
    return pl.pallas_call(
        _merge_kernel,
        grid=(bsz, seq // tm),
        in_specs=[
            pl.BlockSpec((1, tm, oa.shape[-1]), tile),
            pl.BlockSpec((1, tm, ob.shape[-1]), tile),
            pl.BlockSpec((1, tm, d), tile),
            pl.BlockSpec((1, d), const),
            pl.BlockSpec((d, d), lambda b, i: (0, gate_blk), pipeline_mode=once),
            pl.BlockSpec((d, d), lambda b, i: (0, gate_blk + 1), pipeline_mode=once),
            pl.BlockSpec(bg.shape, const),
            pl.BlockSpec(wa.shape, const, pipeline_mode=once),
            pl.BlockSpec(wb.shape, const, pipeline_mode=once),
            pl.BlockSpec(wo.shape, const, pipeline_mode=once),
            pl.BlockSpec((1, d), const),
        ],
        out_specs=[pl.BlockSpec((1, tm, d), tile), pl.BlockSpec((1, tm, d), tile)],
        out_shape=[jax.ShapeDtypeStruct((bsz, seq, d), f32),
                   jax.ShapeDtypeStruct((bsz, seq, d), bf16)],
        compiler_params=pltpu.CompilerParams(
            dimension_semantics=("arbitrary", "arbitrary"), vmem_limit_bytes=VMEM_LIMIT),
        name="merge_outproj",
    )(oa, ob, x, g1, w_in, w_in, bg, wa, wb, wo, g2)


HALO = 8
assert HALO >= CONV_W - 1


def _mlp_kernel(h2_ref, x1_ref, wup_ref, cw_ref, cb_ref, wd_ref, o_ref, u_scr, *, tm, tiles_per_seq):
    i = pl.program_id(0)
    dff = wd_ref.shape[0]

    @pl.when(i % tiles_per_seq == 0)
    def _():
        u_scr[:HALO, :] = jnp.zeros((HALO, dff), f32)

    h2 = h2_ref[...]
    u = jnp.dot(h2, wup_ref[:, :dff], preferred_element_type=f32)
    g = jnp.dot(h2, wup_ref[:, dff:], preferred_element_type=f32)
    u_scr[HALO:, :] = u
    y = cb_ref[...]
    for j in range(CONV_W):
        y = y + u_scr[pl.ds(HALO - (CONV_W - 1) + j, tm), :] * cw_ref[j:j + 1, :]
    u_scr[:HALO, :] = u[tm - HALO:, :]
    act = (jax.nn.gelu(y) * g).astype(bf16)
    o_ref[...] = x1_ref[...] + jnp.dot(act, wd_ref[...], preferred_element_type=f32)


def _mlp(h2, x1, w_up_bf, conv_w, conv_b, w_down_bf, *, seq, tm):
    n, d = h2.shape
    dff = w_down_bf.shape[0]
    const = lambda i: (0, 0)
    return pl.pallas_call(
        functools.partial(_mlp_kernel, tm=tm, tiles_per_seq=seq // tm),
        grid=(n // tm,),
        in_specs=[
            pl.BlockSpec((tm, d), lambda i: (i, 0)),
            pl.BlockSpec((tm, d), lambda i: (i, 0)),
            pl.BlockSpec(w_up_bf.shape, const, pipeline_mode=pl.Buffered(1)),
            pl.BlockSpec(conv_w.shape, const),
            pl.BlockSpec(conv_b.shape, const),
            pl.BlockSpec(w_down_bf.shape, const, pipeline_mode=pl.Buffered(1)),
        ],
        out_specs=pl.BlockSpec((tm, d), lambda i: (i, 0)),
        out_shape=jax.ShapeDtypeStruct((n, d), f32),
        scratch_shapes=[pltpu.VMEM((tm + HALO, dff), f32)],
        compiler_params=pltpu.CompilerParams(
            dimension_semantics=("arbitrary",), vmem_limit_bytes=VMEM_LIMIT),
        name="conv_mlp",
    )(h2, x1, w_up_bf, conv_w, conv_b, w_down_bf)


def _rope_tables(seq):
    inv = 1.0 / (ROPE_THETA ** (jnp.arange(0, HEAD_DIM, 2, dtype=f32) / HEAD_DIM))
    ang = jnp.arange(seq, dtype=f32)[:, None] * inv[None, :]
    cos, sin = jnp.cos(ang), jnp.sin(ang)
    reps = LANES // HEAD_DIM
    return (jnp.tile(cos, (1, 2 * reps)), jnp.tile(jnp.concatenate([-sin, sin], axis=1), (1, reps)))


def kernel(x, norm1_g, w_in, b_gate, qn_a, kn_a, lam_q1, lam_k1, lam_q2, lam_k2, subln_g, qn_b, kn_b,
           w_a_proj, w_b_proj, w_out, norm2_g, w_up, conv_w, conv_b, w_down):
    bsz, seq, d = x.shape
    depth = w_in.shape[0]
    assert seq % max(TM_INPROJ, TM_MERGE, TM_MLP) == 0 and d % LANES == 0
    cos_t, sin_t = _rope_tables(seq)
    seg = jnp.arange(LANES) // HEAD_DIM
    bd = (seg[:, None] == seg[None, :]).astype(bf16)
    reps = LANES // HEAD_DIM
    qscale = HEAD_DIM ** -0.5 * LOG2E
    for l in range(depth):
        lam_init = 0.8 - 0.6 * math.exp(-0.3 * l)
        gains = jnp.stack([jnp.tile(qn_a[l], reps) * qscale, jnp.tile(kn_a[l], reps),
                           jnp.tile(qn_b[l], reps) * qscale, jnp.tile(kn_b[l], reps)])
        q_all, k_all, vt_all, kmean, w_up_bf, w_down_bf = _inproj(
            x, norm1_g[l][None], w_in[l], gains, cos_t, sin_t, bd, (w_up[l], w_down[l]), tm=TM_INPROJ)
        kmean = kmean.reshape(bsz, seq // BLK, B_WIDTH)
        lamvec = jnp.stack([lam_q1[l], lam_k1[l], lam_q2[l], lam_k2[l]])
        gmax = jnp.max(jnp.abs(gains), axis=1) * (HEAD_DIM ** 0.5)
        subg_cols = jnp.tile(subln_g[l][:, None], (1, BLK))
        oa = _attention(q_all, k_all, vt_all, (lamvec, subg_cols),
                        gmax[0] * gmax[1] * BOUND_SLACK, moba=False, lam_init=lam_init)
        ob = _attention(q_all, k_all, vt_all, (kmean,),
                        gmax[2] * gmax[3] * BOUND_SLACK, moba=True, lam_init=lam_init)
        x1, h2 = _merge(oa, ob, x, norm1_g[l][None], w_in[l], b_gate[l][None],
                        w_a_proj[l], w_b_proj[l], w_out[l], norm2_g[l][None], tm=TM_MERGE)
        dff = w_down.shape[1]
        x = _mlp(h2.reshape(bsz * seq, d), x1.reshape(bsz * seq, d), w_up_bf, conv_w[l],
                 conv_b[l][None], w_down_bf, seq=seq, tm=TM_MLP).reshape(bsz, seq, d)
    return x
```

```python
import functools
import math

import jax
import jax.numpy as jnp
from jax import lax
from jax.experimental import pallas as pl
from jax.experimental.pallas import tpu as pltpu

HEAD_DIM = 64
A_HEADS = 4
A_VDIM = 2 * HEAD_DIM
A_WIDTH = A_HEADS * A_VDIM
B_HEADS = 8
B_WIDTH = B_HEADS * HEAD_DIM
B_BLOCK = 256
B_TOPK = 3
CONV_W = 3
ROPE_THETA = 10000.0
EPS = 1e-6

LANES = 128
BLK = B_BLOCK
PAIRS = A_WIDTH // LANES
LOG2E = 1.4426950408889634
NEG_BIG = -1e30
POS_BIG = 1e30
VMEM_LIMIT = 56 * 1024 * 1024
BF16_SUBLANES = 16
TM_INPROJ = 1024
TM_MERGE = 1024
TM_MLP = 512

f32 = jnp.float32
bf16 = jnp.bfloat16


def _nt_dot(a, b):
    return lax.dot_general(a, b, (((1,), (1,)), ((), ())), preferred_element_type=f32)


def _inproj_kernel(x_ref, g1_ref, w_ref, gain_ref, cos_ref, sin_ref, bd_ref, *rest, tm, n_late):
    late_f32 = rest[:n_late]
    q_ref, k_ref, vt_ref, kmean_ref = rest[n_late:n_late + 4]
    late_bf16 = rest[n_late + 4:2 * n_late + 4]
    h_scr = rest[-1]
    for src, dst in zip(late_f32, late_bf16):
        dst[...] = src[...].astype(bf16)
    nblk = tm // BLK
    x = x_ref[0]
    ms = jnp.mean(x * x, axis=-1, keepdims=True)
    h_scr[...] = (x * lax.rsqrt(ms + EPS) * g1_ref[...]).astype(bf16)
    cos = cos_ref[...]
    sin = sin_ref[...]
    bd = bd_ref[...]
    lane = lax.broadcasted_iota(jnp.int32, (tm, LANES), 1)
    first_half = (lane & (HEAD_DIM - 1)) < (HEAD_DIM // 2)
    low_half = lane < HEAD_DIM

    def col_tile(j):
        return jnp.dot(h_scr[...], w_ref[:, j * A_WIDTH:(j + 1) * A_WIDTH].astype(bf16),
                       preferred_element_type=f32)

    def norm_rope(y, gain):
        ss = jnp.dot((y * y).astype(bf16), bd, preferred_element_type=f32)
        yn = y * lax.rsqrt(ss * (1.0 / HEAD_DIM) + EPS) * gain
        rot = jnp.where(first_half, pltpu.roll(yn, LANES - HEAD_DIM // 2, 1),
                        pltpu.roll(yn, HEAD_DIM // 2, 1))
        return yn * cos + rot * sin

    for branch in range(2):
        acc_q = col_tile(3 * branch)
        for c in range(PAIRS):
            out = norm_rope(acc_q[:, c * LANES:(c + 1) * LANES], gain_ref[2 * branch:2 * branch + 1, :])
            lo = jnp.where(low_half, out, 0.0).astype(bf16)
            hi = jnp.where(low_half, 0.0, out).astype(bf16)
            for bi in range(nblk):
                q_ref[0, branch * PAIRS + c, bi, :BLK] = lo[bi * BLK:(bi + 1) * BLK]
                q_ref[0, branch * PAIRS + c, bi, BLK:] = hi[bi * BLK:(bi + 1) * BLK]
        acc_k = col_tile(3 * branch + 1)
        for c in range(PAIRS):
            out = norm_rope(acc_k[:, c * LANES:(c + 1) * LANES],
                            gain_ref[2 * branch + 1:2 * branch + 2, :])
            for bi in range(nblk):
                blk = out[bi * BLK:(bi + 1) * BLK]
                k_ref[0, branch * PAIRS + c, bi] = blk.astype(bf16)
                if branch == 1:
                    kmean_ref[0, bi:bi + 1, c * LANES:(c + 1) * LANES] = jnp.mean(
                        blk, axis=0, keepdims=True)
        acc_v = col_tile(3 * branch + 2)
        for c in range(PAIRS):
            for bi in range(nblk):
                blk = acc_v[bi * BLK:(bi + 1) * BLK, c * LANES:(c + 1) * LANES]
                vt_ref[0, branch * PAIRS + c, bi] = blk.T.astype(bf16)


def _inproj(x, g1, w_in, gains, cos_t, sin_t, bd, late_weights, *, tm):
    bsz, seq, d = x.shape
    steps = bsz * (seq // tm)
    slab = lambda b, i: (b * (seq // tm) + i, 0)
    late_specs = [pl.BlockSpec((w.shape[0] // steps, w.shape[1]), slab) for w in late_weights]
    assert all(w.shape[0] % (steps * BF16_SUBLANES) == 0 for w in late_weights)
    n_cols = 3 * A_WIDTH + 3 * B_WIDTH
    nblk = tm // BLK
    spt = seq // tm
    nb = seq // BLK
    q_shape = jax.ShapeDtypeStruct((bsz, 2 * PAIRS, nb, 2 * BLK, LANES), bf16)
    qk_shape = jax.ShapeDtypeStruct((bsz, 2 * PAIRS, nb, BLK, LANES), bf16)
    vt_shape = jax.ShapeDtypeStruct((bsz, 2 * PAIRS, nb, LANES, BLK), bf16)
    q_spec = pl.BlockSpec((1, 2 * PAIRS, nblk, 2 * BLK, LANES), lambda b, i: (b, 0, i, 0, 0))
    qk_spec = pl.BlockSpec((1, 2 * PAIRS, nblk, BLK, LANES), lambda b, i: (b, 0, i, 0, 0))
    vt_spec = pl.BlockSpec((1, 2 * PAIRS, nblk, LANES, BLK), lambda b, i: (b, 0, i, 0, 0))
    return pl.pallas_call(
        functools.partial(_inproj_kernel, tm=tm, n_late=len(late_weights)),
        grid=(bsz, spt),
        in_specs=[
            pl.BlockSpec((1, tm, d), lambda b, i: (b, i, 0)),
            pl.BlockSpec((1, d), lambda b, i: (0, 0)),
            pl.BlockSpec((d, n_cols), lambda b, i: (0, 0), pipeline_mode=pl.Buffered(1)),
            pl.BlockSpec((4, LANES), lambda b, i: (0, 0)),
            pl.BlockSpec((tm, LANES), lambda b, i: (i, 0)),
            pl.BlockSpec((tm, LANES), lambda b, i: (i, 0)),
            pl.BlockSpec((LANES, LANES), lambda b, i: (0, 0)),
        ] + late_specs,
        out_specs=[
            q_spec, qk_spec, vt_spec,
            pl.BlockSpec((1, nblk, B_WIDTH), lambda b, i: (b * spt + i, 0, 0)),
        ] + late_specs,
        out_shape=[
            q_shape, qk_shape, vt_shape,
            jax.ShapeDtypeStruct((bsz * spt, nblk, B_WIDTH), f32),
        ] + [jax.ShapeDtypeStruct(w.shape, bf16) for w in late_weights],
        scratch_shapes=[pltpu.VMEM((tm, d), bf16)],
        compiler_params=pltpu.CompilerParams(
            dimension_semantics=("arbitrary", "arbitrary"), vmem_limit_bytes=VMEM_LIMIT),
        name="inproj",
    )(x, g1, w_in, gains, cos_t, sin_t, bd, *late_weights)


TRIP_STEPS = 34
QK_AHEAD = 6
TILE_UNROLL = 4
BOUND_SLACK = 1.0 + 2.0 ** -6
MAX_FIXED_BOUND = 60.0


def _attn_kernel(itab_ref, jtab_ref, otab_ref, *refs, moba, online, lam_init, n_steps, n_masked,
                 n_single):
    if moba:
        (q_ref, k_ref, vt_ref, mref_ref, kmean_ref, o_ref,
         m_scr, l_scr, acc_scr, s_scr, sel_scr) = refs
    else:
        (q_ref, k_ref, vt_ref, mref_ref, lam_ref, subg_ref, o_ref,
         m_scr, l_scr, acc_scr, s_scr) = refs
    nb = k_ref.shape[2]
    acc_rows = acc_scr.shape[1]
    key_minus_row = (lax.broadcasted_iota(jnp.int32, (BLK, 2 * BLK), 0)
                     - (lax.broadcasted_iota(jnp.int32, (BLK, 2 * BLK), 1) & (BLK - 1)))

    if moba:
        km = kmean_ref[0]
        km_parts = []
        for _ in range(3):
            part = km.astype(bf16)
            km_parts.append(part)
            km = km - part.astype(f32)

    nidx = lax.broadcasted_iota(jnp.int32, (nb, 2 * BLK), 0)
    for i in range(nb):
        if online:
            m_scr[i] = jnp.full((1, 2 * BLK), NEG_BIG, f32)
        l_scr[i] = jnp.zeros((1, 2 * BLK), f32)
        acc_scr[i] = jnp.zeros((acc_rows, 2 * BLK), f32)
        if moba:
            keep = nidx <= i
            if i > B_TOPK:
                qs = q_ref[0, 0, i]
                g = _nt_dot(km_parts[0], qs) + _nt_dot(km_parts[1], qs) + _nt_dot(km_parts[2], qs)
                cnt = jnp.zeros((nb, 2 * BLK), f32)
                for mm in range(i):
                    row = g[mm:mm + 1, :]
                    beats = (row > g) | ((row == g) & (nidx > mm))
                    cnt = cnt + jnp.where(beats, 1.0, 0.0)
                keep = ((cnt < float(B_TOPK)) & (nidx < i)) | (nidx == i)
            sel_scr[i] = jnp.where(keep, 1.0, 0.0)

    ring = s_scr.shape[0]

    def qk(step, u):
        s = _nt_dot(k_ref[0, 0, jtab_ref[step]], q_ref[0, 0, itab_ref[step]])
        if u < n_masked:
            s = jnp.where(key_minus_row <= otab_ref[step], s, NEG_BIG)
        s_scr[u % ring] = s

    def qk_pair(step, u):
        i = itab_ref[step]
        q2 = q_ref[0, 0, pl.ds(i, 2)].reshape(4 * BLK, LANES)
        s2 = _nt_dot(k_ref[0, 0, jtab_ref[step]], q2)
        s_scr[u % ring] = s2[:, :2 * BLK]
        s_scr[(u + 1) % ring] = s2[:, 2 * BLK:]

    def update(step, u):
        i = itab_ref[step]
        j = jtab_ref[step]
        s = s_scr[u % ring]
        if moba:
            chosen = sel_scr[i, pl.ds(j, 1), :] > 0.5
        if online:
            m_old = m_scr[i]
            m_new = jnp.maximum(m_old, jnp.max(s, axis=0, keepdims=True))
            if moba:
                m_new = jnp.where(chosen, m_new, m_old)
            m_sub = m_new
            alpha = jnp.exp2(m_old - m_new)
            m_scr[i] = m_new
        else:
            m_sub = mref_ref[...]
        if moba:
            m_sub = jnp.where(chosen, m_sub, POS_BIG)
        p = jnp.exp2(s - m_sub)
        psum = jnp.sum(p, axis=0, keepdims=True)
        pb = p.astype(bf16)
        vt = vt_ref[0, 0, j]
        pv = jnp.dot(vt, pb, preferred_element_type=f32)
        if online:
            l_scr[i] = alpha * l_scr[i] + psum
            acc_scr[i] = alpha * acc_scr[i] + pv
        else:
            l_scr[i] = l_scr[i] + psum
            acc_scr[i] = acc_scr[i] + pv

    ahead = ring - 2
    units = [(u, 1) for u in range(n_single)] + [(u, 2) for u in range(n_single, TRIP_STEPS, 2)]

    def trip(it, carry):
        base = it * TRIP_STEPS
        issued = 0
        for u in range(TRIP_STEPS):
            while issued < len(units) and units[issued][0] <= u + ahead:
                start, size = units[issued]
                (qk if size == 1 else qk_pair)(base + start, start)
                issued += 1
            update(base + u, u)
        return carry

    lax.fori_loop(0, n_steps // TRIP_STEPS, trip, 0)

    if not moba:
        lv = lam_ref[...]
        lam = (jnp.exp(jnp.sum(lv[0:1] * lv[1:2], axis=-1, keepdims=True))
               - jnp.exp(jnp.sum(lv[2:3] * lv[3:4], axis=-1, keepdims=True)) + lam_init)

    def finish_tiles(t, carry):
        for r in range(TILE_UNROLL):
            i = t * TILE_UNROLL + r
            o = acc_scr[i] * (1.0 / l_scr[i])
            rows = pl.ds(pl.multiple_of(i * BLK, BLK), BLK)
            if moba:
                ot = jnp.concatenate([o[:HEAD_DIM, :BLK], o[HEAD_DIM:, BLK:]], axis=0).T
                o_ref[0, rows, :] = ot.astype(bf16)
            else:
                od = o[:, :BLK] - lam * o[:, BLK:]
                ms = jnp.mean(od * od, axis=0, keepdims=True)
                od = od * lax.rsqrt(ms + EPS) * subg_ref[...] * (1.0 - lam_init)
                o_ref[0, rows, :] = od.T.astype(bf16)
        return carry

    lax.fori_loop(0, nb // TILE_UNROLL, finish_tiles, 0)


def _step_tables(nb):
    diag = [(i, i, 0) for i in range(nb)]
    pairs, odd = [], []
    for j in range(nb):
        col = [(i, j, BLK) for i in range(j + 1, nb)]
        while len(col) >= 2:
            pairs.append((col.pop(0), col.pop(0)))
        odd += col
    n_steps = len(diag) + len(odd) + 2 * len(pairs)
    n_trips = n_steps // TRIP_STEPS
    assert n_steps % TRIP_STEPS == 0
    assert len(diag) % n_trips == 0 and len(odd) % n_trips == 0 and len(pairs) % n_trips == 0
    n_masked, n_odd, n_pair = len(diag) // n_trips, len(odd) // n_trips, len(pairs) // n_trips
    steps = []
    for t in range(n_trips):
        steps += diag[t * n_masked:(t + 1) * n_masked] + odd[t * n_odd:(t + 1) * n_odd]
        for first, second in pairs[t * n_pair:(t + 1) * n_pair]:
            steps += [first, second]
    tables = [jnp.asarray([s[c] for s in steps], jnp.int32) for c in range(3)]
    return tables, n_steps, n_masked, n_masked + n_odd


def _attention(q_all, k_all, vt_all, aux, score_bound, *, moba, lam_init):
    bsz, _, nb, _, _ = k_all.shape
    off = PAIRS if moba else 0
    assert nb % TILE_UNROLL == 0
    (itab, jtab, otab), n_steps, n_masked, n_single = _step_tables(nb)
    const = lambda b, h, it, jt, ot: (0, 0)
    in_specs = [
        pl.BlockSpec((1, 1, nb, 2 * BLK, LANES), lambda b, h, it, jt, ot: (b, h + off, 0, 0, 0)),
        pl.BlockSpec((1, 1, nb, BLK, LANES), lambda b, h, it, jt, ot: (b, h + off, 0, 0, 0)),
        pl.BlockSpec((1, 1, nb, LANES, BLK), lambda b, h, it, jt, ot: (b, h + off, 0, 0, 0)),
        pl.BlockSpec((1, 2 * BLK), const),
    ]
    scratch = [pltpu.VMEM((nb, 1, 2 * BLK), f32), pltpu.VMEM((nb, 1, 2 * BLK), f32),
               pltpu.VMEM((nb, LANES, 2 * BLK), f32),
               pltpu.VMEM((QK_AHEAD + 2, BLK, 2 * BLK), f32)]
    if moba:
        (kmean,) = aux
        in_specs.append(pl.BlockSpec((1, nb, LANES), lambda b, h, it, jt, ot: (b, 0, h)))
        scratch.append(pltpu.VMEM((nb, nb, 2 * BLK), f32))
    else:
        lamvec, subg = aux
        in_specs.append(pl.BlockSpec(lamvec.shape, const))
        in_specs.append(pl.BlockSpec(subg.shape, const))
    mref = jnp.full((1, 2 * BLK), score_bound, f32)

    def run(online):
        return pl.pallas_call(
            functools.partial(_attn_kernel, moba=moba, online=online, lam_init=lam_init,
                              n_steps=n_steps, n_masked=n_masked, n_single=n_single),
            grid_spec=pltpu.PrefetchScalarGridSpec(
                num_scalar_prefetch=3,
                grid=(bsz, PAIRS),
                in_specs=in_specs,
                out_specs=pl.BlockSpec((1, nb * BLK, LANES), lambda b, h, it, jt, ot: (b, 0, h)),
                scratch_shapes=scratch),
            out_shape=jax.ShapeDtypeStruct((bsz, nb * BLK, PAIRS * LANES), bf16),
            compiler_params=pltpu.CompilerParams(
                dimension_semantics=("arbitrary", "arbitrary"), vmem_limit_bytes=VMEM_LIMIT),
            name=("moba_attn" if moba else "diff_attn") + ("_online" if online else ""),
        )(itab, jtab, otab, q_all, k_all, vt_all, mref, *aux)

    return lax.cond(score_bound <= MAX_FIXED_BOUND, lambda: run(False), lambda: run(True))


def _merge_kernel(oa_ref, ob_ref, x_ref, g1_ref, wga_ref, wgb_ref, bg_ref, wa_ref, wb_ref, wo_ref,
                  g2_ref, x1_ref, h2_ref):
    d = x_ref.shape[-1]
    x = x_ref[0]
    ms = jnp.mean(x * x, axis=-1, keepdims=True)
    h = (x * lax.rsqrt(ms + EPS) * g1_ref[...]).astype(bf16)
    merged = None
    for br, (wg_ref, o_ref, w_ref) in enumerate(((wga_ref, oa_ref, wa_ref), (wgb_ref, ob_ref, wb_ref))):
        gate = jax.nn.sigmoid(jnp.dot(h, wg_ref[...].astype(bf16), preferred_element_type=f32)
                              + bg_ref[:, br * d:(br + 1) * d])
        term = gate * jnp.dot(o_ref[0], w_ref[...].astype(bf16), preferred_element_type=f32)
        merged = term if merged is None else merged + term
    x1 = x + jnp.dot(merged.astype(bf16), wo_ref[...].astype(bf16), preferred_element_type=f32)
    x1_ref[0] = x1
    ms = jnp.mean(x1 * x1, axis=-1, keepdims=True)
    h2_ref[0] = (x1 * lax.rsqrt(ms + EPS) * g2_ref[...]).astype(bf16)


def _merge(oa, ob, x, g1, w_in, bg, wa, wb, wo, g2, *, tm):
    bsz, seq, d = x.shape
    gate_blk = (w_in.shape[1] - 2 * d) // d
    assert gate_blk * d + 2 * d == w_in.shape[1]
    const = lambda b, i: (0, 0)
    once = pl.Buffered(1)
    tile = lambda b, i: (b, i, 0)
    return pl.pallas_call(
        _merge_kernel,
        grid=(bsz, seq // tm),
        in_specs=[
            pl.BlockSpec((1, tm, oa.shape[-1]), tile),
            pl.BlockSpec((1, tm, ob.shape[-1]), tile),
            pl.BlockSpec((1, tm, d), tile),
            pl.BlockSpec((1, d), const),
            pl.BlockSpec((d, d), lambda b, i: (0, gate_blk), pipeline_mode=once),
            pl.BlockSpec((d, d), lambda b, i: (0, gate_blk + 1), pipeline_mode=once),
            pl.BlockSpec(bg.shape, const),
            pl.BlockSpec(wa.shape, const, pipeline_mode=once),
            pl.BlockSpec(wb.shape, const, pipeline_mode=once),
            pl.BlockSpec(wo.shape, const, pipeline_mode=once),
            pl.BlockSpec((1, d), const),
        ],
        out_specs=[pl.BlockSpec((1, tm, d), tile), pl.BlockSpec((1, tm, d), tile)],
        out_shape=[jax.ShapeDtypeStruct((bsz, seq, d), f32),
                   jax.ShapeDtypeStruct((bsz, seq, d), bf16)],
        compiler_params=pltpu.CompilerParams(
            dimension_semantics=("arbitrary", "arbitrary"), vmem_limit_bytes=VMEM_LIMIT),
        name="merge_outproj",
    )(oa, ob, x, g1, w_in, w_in, bg, wa, wb, wo, g2)


HALO = 8
assert HALO >= CONV_W - 1


def _mlp_kernel(h2_ref, x1_ref, wup_ref, cw_ref, cb_ref, wd_ref, o_ref, u_scr, *, tm, tiles_per_seq):
    i = pl.program_id(0)
    dff = wd_ref.shape[0]

    @pl.when(i % tiles_per_seq == 0)
    def _():
        u_scr[:HALO, :] = jnp.zeros((HALO, dff), f32)

    ug = jnp.dot(h2_ref[...], wup_ref[...], preferred_element_type=f32)
    u, g = ug[:, :dff], ug[:, dff:]
    u_scr[HALO:, :] = u
    y = cb_ref[...]
    for j in range(CONV_W):
        y = y + u_scr[pl.ds(HALO - (CONV_W - 1) + j, tm), :] * cw_ref[j:j + 1, :]
    u_scr[:HALO, :] = u[tm - HALO:, :]
    act = (jax.nn.gelu(y) * g).astype(bf16)
    o_ref[...] = x1_ref[...] + jnp.dot(act, wd_ref[...], preferred_element_type=f32)


def _mlp(h2, x1, w_up_bf, conv_w, conv_b, w_down_bf, *, seq, tm):
    n, d = h2.shape
    dff = w_down_bf.shape[0]
    const = lambda i: (0, 0)
    return pl.pallas_call(
        functools.partial(_mlp_kernel, tm=tm, tiles_per_seq=seq // tm),
        grid=(n // tm,),
        in_specs=[
            pl.BlockSpec((tm, d), lambda i: (i, 0)),
            pl.BlockSpec((tm, d), lambda i: (i, 0)),
            pl.BlockSpec(w_up_bf.shape, const, pipeline_mode=pl.Buffered(1)),
            pl.BlockSpec(conv_w.shape, const),
            pl.BlockSpec(conv_b.shape, const),
            pl.BlockSpec(w_down_bf.shape, const, pipeline_mode=pl.Buffered(1)),
        ],
        out_specs=pl.BlockSpec((tm, d), lambda i: (i, 0)),
        out_shape=jax.ShapeDtypeStruct((n, d), f32),
        scratch_shapes=[pltpu.VMEM((tm + HALO, dff), f32)],
        compiler_params=pltpu.CompilerParams(
            dimension_semantics=("arbitrary",), vmem_limit_bytes=VMEM_LIMIT),
        name="conv_mlp",
    )(h2, x1, w_up_bf, conv_w, conv_b, w_down_bf)


def _rope_tables(seq):
    inv = 1.0 / (ROPE_THETA ** (jnp.arange(0, HEAD_DIM, 2, dtype=f32) / HEAD_DIM))
    ang = jnp.arange(seq, dtype=f32)[:, None] * inv[None, :]
    cos, sin = jnp.cos(ang), jnp.sin(ang)
    reps = LANES // HEAD_DIM
    return (jnp.tile(cos, (1, 2 * reps)), jnp.tile(jnp.concatenate([-sin, sin], axis=1), (1, reps)))


def kernel(x, norm1_g, w_in, b_gate, qn_a, kn_a, lam_q1, lam_k1, lam_q2, lam_k2, subln_g, qn_b, kn_b,
           w_a_proj, w_b_proj, w_out, norm2_g, w_up, conv_w, conv_b, w_down):
    bsz, seq, d = x.shape
    depth = w_in.shape[0]
    assert seq % max(TM_INPROJ, TM_MERGE, TM_MLP) == 0 and d % LANES == 0
    cos_t, sin_t = _rope_tables(seq)
    seg = jnp.arange(LANES) // HEAD_DIM
    bd = (seg[:, None] == seg[None, :]).astype(bf16)
    reps = LANES // HEAD_DIM
    qscale = HEAD_DIM ** -0.5 * LOG2E
    for l in range(depth):
        lam_init = 0.8 - 0.6 * math.exp(-0.3 * l)
        gains = jnp.stack([jnp.tile(qn_a[l], reps) * qscale, jnp.tile(kn_a[l], reps),
                           jnp.tile(qn_b[l], reps) * qscale, jnp.tile(kn_b[l], reps)])
        q_all, k_all, vt_all, kmean, w_up_bf, w_down_bf = _inproj(
            x, norm1_g[l][None], w_in[l], gains, cos_t, sin_t, bd, (w_up[l], w_down[l]), tm=TM_INPROJ)
        kmean = kmean.reshape(bsz, seq // BLK, B_WIDTH)
        lamvec = jnp.stack([lam_q1[l], lam_k1[l], lam_q2[l], lam_k2[l]])
        gmax = jnp.max(jnp.abs(gains), axis=1) * (HEAD_DIM ** 0.5)
        subg_cols = jnp.tile(subln_g[l][:, None], (1, BLK))
        oa = _attention(q_all, k_all, vt_all, (lamvec, subg_cols),
                        gmax[0] * gmax[1] * BOUND_SLACK, moba=False, lam_init=lam_init)
        ob = _attention(q_all, k_all, vt_all, (kmean,),
                        gmax[2] * gmax[3] * BOUND_SLACK, moba=True, lam_init=lam_init)
        x1, h2 = _merge(oa, ob, x, norm1_g[l][None], w_in[l], b_gate[l][None],
                        w_a_proj[l], w_b_proj[l], w_out[l], norm2_g[l][None], tm=TM_MERGE)
        x = _mlp(h2.reshape(bsz * seq, d), x1.reshape(bsz * seq, d), w_up_bf, conv_w[l],
                 conv_b[l][None], w_down_bf, seq=seq, tm=TM_MLP).reshape(bsz, seq, d)
    return x
```

```python
import functools
import math

import jax
import jax.numpy as jnp
from jax import lax
from jax.experimental import pallas as pl
from jax.experimental.pallas import tpu as pltpu

HEAD_DIM = 64
A_HEADS = 4
A_VDIM = 2 * HEAD_DIM
A_WIDTH = A_HEADS * A_VDIM
B_HEADS = 8
B_WIDTH = B_HEADS * HEAD_DIM
B_BLOCK = 256
B_TOPK = 3
CONV_W = 3
ROPE_THETA = 10000.0
EPS = 1e-6

LANES = 128
BLK = B_BLOCK
PAIRS = A_WIDTH // LANES
LOG2E = 1.4426950408889634
NEG_BIG = -1e30
POS_BIG = 1e30
VMEM_LIMIT = 56 * 1024 * 1024
BF16_SUBLANES = 16
TM_INPROJ = 1024
TM_MERGE = 1024
TM_MLP = 512

f32 = jnp.float32
bf16 = jnp.bfloat16


def _nt_dot(a, b):
    return lax.dot_general(a, b, (((1,), (1,)), ((), ())), preferred_element_type=f32)


def _inproj_kernel(x_ref, g1_ref, w_ref, gain_ref, cos_ref, sin_ref, bd_ref, *rest, tm, n_late):
    late_f32 = rest[:n_late]
    q_ref, k_ref, vt_ref, kmean_ref = rest[n_late:n_late + 4]
    late_bf16 = rest[n_late + 4:2 * n_late + 4]
    h_scr = rest[-1]
    for src, dst in zip(late_f32, late_bf16):
        dst[...] = src[...].astype(bf16)
    nblk = tm // BLK
    x = x_ref[0]
    ms = jnp.mean(x * x, axis=-1, keepdims=True)
    h_scr[...] = (x * lax.rsqrt(ms + EPS) * g1_ref[...]).astype(bf16)
    cos = cos_ref[...]
    sin = sin_ref[...]
    bd = bd_ref[...]
    lane = lax.broadcasted_iota(jnp.int32, (tm, LANES), 1)
    first_half = (lane & (HEAD_DIM - 1)) < (HEAD_DIM // 2)
    low_half = lane < HEAD_DIM

    def col_tile(j):
        return jnp.dot(h_scr[...], w_ref[:, j * A_WIDTH:(j + 1) * A_WIDTH].astype(bf16),
                       preferred_element_type=f32)

    def norm_rope(y, gain):
        ss = jnp.dot((y * y).astype(bf16), bd, preferred_element_type=f32)
        yn = y * lax.rsqrt(ss * (1.0 / HEAD_DIM) + EPS) * gain
        rot = jnp.where(first_half, pltpu.roll(yn, LANES - HEAD_DIM // 2, 1),
                        pltpu.roll(yn, HEAD_DIM // 2, 1))
        return yn * cos + rot * sin

    for branch in range(2):
        acc_q = col_tile(3 * branch)
        for c in range(PAIRS):
            out = norm_rope(acc_q[:, c * LANES:(c + 1) * LANES], gain_ref[2 * branch:2 * branch + 1, :])
            lo = jnp.where(low_half, out, 0.0).astype(bf16)
            hi = jnp.where(low_half, 0.0, out).astype(bf16)
            for bi in range(nblk):
                q_ref[0, branch * PAIRS + c, bi, :BLK] = lo[bi * BLK:(bi + 1) * BLK]
                q_ref[0, branch * PAIRS + c, bi, BLK:] = hi[bi * BLK:(bi + 1) * BLK]
        acc_k = col_tile(3 * branch + 1)
        for c in range(PAIRS):
            out = norm_rope(acc_k[:, c * LANES:(c + 1) * LANES],
                            gain_ref[2 * branch + 1:2 * branch + 2, :])
            for bi in range(nblk):
                blk = out[bi * BLK:(bi + 1) * BLK]
                k_ref[0, branch * PAIRS + c, bi] = blk.astype(bf16)
                if branch == 1:
                    kmean_ref[0, bi:bi + 1, c * LANES:(c + 1) * LANES] = jnp.mean(
                        blk, axis=0, keepdims=True)
        acc_v = col_tile(3 * branch + 2)
        for c in range(PAIRS):
            for bi in range(nblk):
                blk = acc_v[bi * BLK:(bi + 1) * BLK, c * LANES:(c + 1) * LANES]
                vt_ref[0, branch * PAIRS + c, bi] = blk.T.astype(bf16)


def _inproj(x, g1, w_in, gains, cos_t, sin_t, bd, late_weights, *, tm):
    bsz, seq, d = x.shape
    steps = bsz * (seq // tm)
    slab = lambda b, i: (b * (seq // tm) + i, 0)
    late_specs = [pl.BlockSpec((w.shape[0] // steps, w.shape[1]), slab) for w in late_weights]
    assert all(w.shape[0] % (steps * BF16_SUBLANES) == 0 for w in late_weights)
    n_cols = 3 * A_WIDTH + 3 * B_WIDTH
    nblk = tm // BLK
    spt = seq // tm
    nb = seq // BLK
    q_shape = jax.ShapeDtypeStruct((bsz, 2 * PAIRS, nb, 2 * BLK, LANES), bf16)
    qk_shape = jax.ShapeDtypeStruct((bsz, 2 * PAIRS, nb, BLK, LANES), bf16)
    vt_shape = jax.ShapeDtypeStruct((bsz, 2 * PAIRS, nb, LANES, BLK), bf16)
    q_spec = pl.BlockSpec((1, 2 * PAIRS, nblk, 2 * BLK, LANES), lambda b, i: (b, 0, i, 0, 0))
    qk_spec = pl.BlockSpec((1, 2 * PAIRS, nblk, BLK, LANES), lambda b, i: (b, 0, i, 0, 0))
    vt_spec = pl.BlockSpec((1, 2 * PAIRS, nblk, LANES, BLK), lambda b, i: (b, 0, i, 0, 0))
    return pl.pallas_call(
        functools.partial(_inproj_kernel, tm=tm, n_late=len(late_weights)),
        grid=(bsz, spt),
        in_specs=[
            pl.BlockSpec((1, tm, d), lambda b, i: (b, i, 0)),
            pl.BlockSpec((1, d), lambda b, i: (0, 0)),
            pl.BlockSpec((d, n_cols), lambda b, i: (0, 0), pipeline_mode=pl.Buffered(1)),
            pl.BlockSpec((4, LANES), lambda b, i: (0, 0)),
            pl.BlockSpec((tm, LANES), lambda b, i: (i, 0)),
            pl.BlockSpec((tm, LANES), lambda b, i: (i, 0)),
            pl.BlockSpec((LANES, LANES), lambda b, i: (0, 0)),
        ] + late_specs,
        out_specs=[
            q_spec, qk_spec, vt_spec,
            pl.BlockSpec((1, nblk, B_WIDTH), lambda b, i: (b * spt + i, 0, 0)),
        ] + late_specs,
        out_shape=[
            q_shape, qk_shape, vt_shape,
            jax.ShapeDtypeStruct((bsz * spt, nblk, B_WIDTH), f32),
        ] + [jax.ShapeDtypeStruct(w.shape, bf16) for w in late_weights],
        scratch_shapes=[pltpu.VMEM((tm, d), bf16)],
        compiler_params=pltpu.CompilerParams(
            dimension_semantics=("arbitrary", "arbitrary"), vmem_limit_bytes=VMEM_LIMIT),
        name="inproj",
    )(x, g1, w_in, gains, cos_t, sin_t, bd, *late_weights)


TRIP_STEPS = 34
QK_AHEAD = 6
TILE_UNROLL = 4
BOUND_SLACK = 1.0 + 2.0 ** -6
MAX_FIXED_BOUND = 60.0


def _attn_kernel(itab_ref, jtab_ref, otab_ref, *refs, moba, online, lam_init, n_steps, n_masked,
                 n_single):
    if moba:
        (q_ref, k_ref, vt_ref, mref_ref, kmean_ref, o_ref,
         m_scr, l_scr, acc_scr, s_scr, sel_scr) = refs
    else:
        (q_ref, k_ref, vt_ref, mref_ref, lam_ref, subg_ref, o_ref,
         m_scr, l_scr, acc_scr, s_scr) = refs
    nb = k_ref.shape[2]
    acc_rows = acc_scr.shape[1]
    key_minus_row = (lax.broadcasted_iota(jnp.int32, (BLK, 2 * BLK), 0)
                     - (lax.broadcasted_iota(jnp.int32, (BLK, 2 * BLK), 1) & (BLK - 1)))

    if moba:
        km = kmean_ref[0]
        km_parts = []
        for _ in range(3):
            part = km.astype(bf16)
            km_parts.append(part)
            km = km - part.astype(f32)

    nidx = lax.broadcasted_iota(jnp.int32, (nb, 2 * BLK), 0)
    for i in range(nb):
        if online:
            m_scr[i] = jnp.full((1, 2 * BLK), NEG_BIG, f32)
        l_scr[i] = jnp.zeros((1, 2 * BLK), f32)
        acc_scr[i] = jnp.zeros((acc_rows, 2 * BLK), f32)
        if moba:
            keep = nidx <= i
            if i > B_TOPK:
                qs = q_ref[0, 0, i]
                g = _nt_dot(km_parts[0], qs) + _nt_dot(km_parts[1], qs) + _nt_dot(km_parts[2], qs)
                cnt = jnp.zeros((nb, 2 * BLK), f32)
                for mm in range(i):
                    row = g[mm:mm + 1, :]
                    beats = (row > g) | ((row == g) & (nidx > mm))
                    cnt = cnt + jnp.where(beats, 1.0, 0.0)
                keep = ((cnt < float(B_TOPK)) & (nidx < i)) | (nidx == i)
            sel_scr[i] = jnp.where(keep, 1.0, 0.0)

    ring = s_scr.shape[0]

    def qk(step, u):
        s = _nt_dot(k_ref[0, 0, jtab_ref[step]], q_ref[0, 0, itab_ref[step]])
        if u < n_masked:
            s = jnp.where(key_minus_row <= otab_ref[step], s, NEG_BIG)
        s_scr[u % ring] = s

    def qk_pair(step, u):
        k2 = k_ref[0, 0, pl.ds(jtab_ref[step], 2)].reshape(2 * BLK, LANES)
        s2 = _nt_dot(k2, q_ref[0, 0, itab_ref[step]])
        s_scr[u % ring] = s2[:BLK]
        s_scr[(u + 1) % ring] = s2[BLK:]

    def update(step, u):
        i = itab_ref[step]
        j = jtab_ref[step]
        s = s_scr[u % ring]
        if moba:
            chosen = sel_scr[i, pl.ds(j, 1), :] > 0.5
        if online:
            m_old = m_scr[i]
            m_new = jnp.maximum(m_old, jnp.max(s, axis=0, keepdims=True))
            if moba:
                m_new = jnp.where(chosen, m_new, m_old)
            m_sub = m_new
            alpha = jnp.exp2(m_old - m_new)
            m_scr[i] = m_new
        else:
            m_sub = mref_ref[...]
        if moba:
            m_sub = jnp.where(chosen, m_sub, POS_BIG)
        p = jnp.exp2(s - m_sub)
        psum = jnp.sum(p, axis=0, keepdims=True)
        pb = p.astype(bf16)
        vt = vt_ref[0, 0, j]
        pv = jnp.dot(vt, pb, preferred_element_type=f32)
        if online:
            l_scr[i] = alpha * l_scr[i] + psum
            acc_scr[i] = alpha * acc_scr[i] + pv
        else:
            l_scr[i] = l_scr[i] + psum
            acc_scr[i] = acc_scr[i] + pv

    ahead = ring - 2
    units = [(u, 1) for u in range(n_single)] + [(u, 2) for u in range(n_single, TRIP_STEPS, 2)]

    def trip(it, carry):
        base = it * TRIP_STEPS
        issued = 0
        for u in range(TRIP_STEPS):
            while issued < len(units) and units[issued][0] <= u + ahead:
                start, size = units[issued]
                (qk if size == 1 else qk_pair)(base + start, start)
                issued += 1
            update(base + u, u)
        return carry

    lax.fori_loop(0, n_steps // TRIP_STEPS, trip, 0)

    if not moba:
        lv = lam_ref[...]
        lam = (jnp.exp(jnp.sum(lv[0:1] * lv[1:2], axis=-1, keepdims=True))
               - jnp.exp(jnp.sum(lv[2:3] * lv[3:4], axis=-1, keepdims=True)) + lam_init)

    def finish_tiles(t, carry):
        for r in range(TILE_UNROLL):
            i = t * TILE_UNROLL + r
            o = acc_scr[i] * (1.0 / l_scr[i])
            rows = pl.ds(pl.multiple_of(i * BLK, BLK), BLK)
            if moba:
                ot = jnp.concatenate([o[:HEAD_DIM, :BLK], o[HEAD_DIM:, BLK:]], axis=0).T
                o_ref[0, rows, :] = ot.astype(bf16)
            else:
                od = o[:, :BLK] - lam * o[:, BLK:]
                ms = jnp.mean(od * od, axis=0, keepdims=True)
                od = od * lax.rsqrt(ms + EPS) * subg_ref[...] * (1.0 - lam_init)
                o_ref[0, rows, :] = od.T.astype(bf16)
        return carry

    lax.fori_loop(0, nb // TILE_UNROLL, finish_tiles, 0)


def _step_tables(nb):
    diag = [(i, i, 0) for i in range(nb)]
    pairs, odd = [], []
    for i in range(nb):
        row = [(i, j, BLK) for j in range(i)]
        while len(row) >= 2:
            pairs.append((row.pop(0), row.pop(0)))
        odd += row
    n_steps = len(diag) + len(odd) + 2 * len(pairs)
    n_trips = n_steps // TRIP_STEPS
    assert n_steps % TRIP_STEPS == 0
    assert len(diag) % n_trips == 0 and len(odd) % n_trips == 0 and len(pairs) % n_trips == 0
    n_masked, n_odd, n_pair = len(diag) // n_trips, len(odd) // n_trips, len(pairs) // n_trips
    steps = []
    for t in range(n_trips):
        steps += diag[t * n_masked:(t + 1) * n_masked] + odd[t * n_odd:(t + 1) * n_odd]
        for first, second in pairs[t::n_trips]:
            steps += [first, second]
    tables = [jnp.asarray([s[c] for s in steps], jnp.int32) for c in range(3)]
    return tables, n_steps, n_masked, n_masked + n_odd


def _attention(q_all, k_all, vt_all, aux, score_bound, *, moba, lam_init):
    bsz, _, nb, _, _ = k_all.shape
    off = PAIRS if moba else 0
    assert nb % TILE_UNROLL == 0
    (itab, jtab, otab), n_steps, n_masked, n_single = _step_tables(nb)
    const = lambda b, h, it, jt, ot: (0, 0)
    in_specs = [
        pl.BlockSpec((1, 1, nb, 2 * BLK, LANES), lambda b, h, it, jt, ot: (b, h + off, 0, 0, 0)),
        pl.BlockSpec((1, 1, nb, BLK, LANES), lambda b, h, it, jt, ot: (b, h + off, 0, 0, 0)),
        pl.BlockSpec((1, 1, nb, LANES, BLK), lambda b, h, it, jt, ot: (b, h + off, 0, 0, 0)),
        pl.BlockSpec((1, 2 * BLK), const),
    ]
    scratch = [pltpu.VMEM((nb, 1, 2 * BLK), f32), pltpu.VMEM((nb, 1, 2 * BLK), f32),
               pltpu.VMEM((nb, LANES, 2 * BLK), f32),
               pltpu.VMEM((QK_AHEAD + 2, BLK, 2 * BLK), f32)]
    if moba:
        (kmean,) = aux
        in_specs.append(pl.BlockSpec((1, nb, LANES), lambda b, h, it, jt, ot: (b, 0, h)))
        scratch.append(pltpu.VMEM((nb, nb, 2 * BLK), f32))
    else:
        lamvec, subg = aux
        in_specs.append(pl.BlockSpec(lamvec.shape, const))
        in_specs.append(pl.BlockSpec(subg.shape, const))
    mref = jnp.full((1, 2 * BLK), score_bound, f32)

    def run(online):
        return pl.pallas_call(
            functools.partial(_attn_kernel, moba=moba, online=online, lam_init=lam_init,
                              n_steps=n_steps, n_masked=n_masked, n_single=n_single),
            grid_spec=pltpu.PrefetchScalarGridSpec(
                num_scalar_prefetch=3,
                grid=(bsz, PAIRS),
                in_specs=in_specs,
                out_specs=pl.BlockSpec((1, nb * BLK, LANES), lambda b, h, it, jt, ot: (b, 0, h)),
                scratch_shapes=scratch),
            out_shape=jax.ShapeDtypeStruct((bsz, nb * BLK, PAIRS * LANES), bf16),
            compiler_params=pltpu.CompilerParams(
                dimension_semantics=("arbitrary", "arbitrary"), vmem_limit_bytes=VMEM_LIMIT),
            name=("moba_attn" if moba else "diff_attn") + ("_online" if online else ""),
        )(itab, jtab, otab, q_all, k_all, vt_all, mref, *aux)

    return lax.cond(score_bound <= MAX_FIXED_BOUND, lambda: run(False), lambda: run(True))


def _merge_kernel(oa_ref, ob_ref, x_ref, g1_ref, wga_ref, wgb_ref, bg_ref, wa_ref, wb_ref, wo_ref,
                  g2_ref, x1_ref, h2_ref):
    d = x_ref.shape[-1]
    x = x_ref[0]
    ms = jnp.mean(x * x, axis=-1, keepdims=True)
    h = (x * lax.rsqrt(ms + EPS) * g1_ref[...]).astype(bf16)
    merged = None
    for br, (wg_ref, o_ref, w_ref) in enumerate(((wga_ref, oa_ref, wa_ref), (wgb_ref, ob_ref, wb_ref))):
        gate = jax.nn.sigmoid(jnp.dot(h, wg_ref[...].astype(bf16), preferred_element_type=f32)
                              + bg_ref[:, br * d:(br + 1) * d])
        term = gate * jnp.dot(o_ref[0], w_ref[...].astype(bf16), preferred_element_type=f32)
        merged = term if merged is None else merged + term
    x1 = x + jnp.dot(merged.astype(bf16), wo_ref[...].astype(bf16), preferred_element_type=f32)
    x1_ref[0] = x1
    ms = jnp.mean(x1 * x1, axis=-1, keepdims=True)
    h2_ref[0] = (x1 * lax.rsqrt(ms + EPS) * g2_ref[...]).astype(bf16)


def _merge(oa, ob, x, g1, w_in, bg, wa, wb, wo, g2, *, tm):
    bsz, seq, d = x.shape
    gate_blk = (w_in.shape[1] - 2 * d) // d
    assert gate_blk * d + 2 * d == w_in.shape[1]
    const = lambda b, i: (0, 0)
    once = pl.Buffered(1)
    tile = lambda b, i: (b, i, 0)
    return pl.pallas_call(
        _merge_kernel,
        grid=(bsz, seq // tm),
        in_specs=[
            pl.BlockSpec((1, tm, oa.shape[-1]), tile),
            pl.BlockSpec((1, tm, ob.shape[-1]), tile),
            pl.BlockSpec((1, tm, d), tile),
            pl.BlockSpec((1, d), const),
            pl.BlockSpec((d, d), lambda b, i: (0, gate_blk), pipeline_mode=once),
            pl.BlockSpec((d, d), lambda b, i: (0, gate_blk + 1), pipeline_mode=once),
            pl.BlockSpec(bg.shape, const),
            pl.BlockSpec(wa.shape, const, pipeline_mode=once),
            pl.BlockSpec(wb.shape, const, pipeline_mode=once),
            pl.BlockSpec(wo.shape, const, pipeline_mode=once),
            pl.BlockSpec((1, d), const),
        ],
        out_specs=[pl.BlockSpec((1, tm, d), tile), pl.BlockSpec((1, tm, d), tile)],
        out_shape=[jax.ShapeDtypeStruct((bsz, seq, d), f32),
                   jax.ShapeDtypeStruct((bsz, seq, d), bf16)],
        compiler_params=pltpu.CompilerParams(
            dimension_semantics=("arbitrary", "arbitrary"), vmem_limit_bytes=VMEM_LIMIT),
        name="merge_outproj",
    )(oa, ob, x, g1, w_in, w_in, bg, wa, wb, wo, g2)


HALO = 8
assert HALO >= CONV_W - 1


def _mlp_kernel(h2_ref, x1_ref, wup_ref, cw_ref, cb_ref, wd_ref, o_ref, u_scr, *, tm, tiles_per_seq):
    i = pl.program_id(0)
    dff = wd_ref.shape[0]

    @pl.when(i % tiles_per_seq == 0)
    def _():
        u_scr[:HALO, :] = jnp.zeros((HALO, dff), f32)

    ug = jnp.dot(h2_ref[...], wup_ref[...], preferred_element_type=f32)
    u, g = ug[:, :dff], ug[:, dff:]
    u_scr[HALO:, :] = u
    y = cb_ref[...]
    for j in range(CONV_W):
        y = y + u_scr[pl.ds(HALO - (CONV_W - 1) + j, tm), :] * cw_ref[j:j + 1, :]
    u_scr[:HALO, :] = u[tm - HALO:, :]
    act = (jax.nn.gelu(y) * g).astype(bf16)
    o_ref[...] = x1_ref[...] + jnp.dot(act, wd_ref[...], preferred_element_type=f32)


def _mlp(h2, x1, w_up_bf, conv_w, conv_b, w_down_bf, *, seq, tm):
    n, d = h2.shape
    dff = w_down_bf.shape[0]
    const = lambda i: (0, 0)
    return pl.pallas_call(
        functools.partial(_mlp_kernel, tm=tm, tiles_per_seq=seq // tm),
        grid=(n // tm,),
        in_specs=[
            pl.BlockSpec((tm, d), lambda i: (i, 0)),
            pl.BlockSpec((tm, d), lambda i: (i, 0)),
            pl.BlockSpec(w_up_bf.shape, const, pipeline_mode=pl.Buffered(1)),
            pl.BlockSpec(conv_w.shape, const),
            pl.BlockSpec(conv_b.shape, const),
            pl.BlockSpec(w_down_bf.shape, const, pipeline_mode=pl.Buffered(1)),
        ],
        out_specs=pl.BlockSpec((tm, d), lambda i: (i, 0)),
        out_shape=jax.ShapeDtypeStruct((n, d), f32),
        scratch_shapes=[pltpu.VMEM((tm + HALO, dff), f32)],
        compiler_params=pltpu.CompilerParams(
            dimension_semantics=("arbitrary",), vmem_limit_bytes=VMEM_LIMIT),
        name="conv_mlp",
    )(h2, x1, w_up_bf, conv_w, conv_b, w_down_bf)


def _rope_tables(seq):
    inv = 1.0 / (ROPE_THETA ** (jnp.arange(0, HEAD_DIM, 2, dtype=f32) / HEAD_DIM))
    ang = jnp.arange(seq, dtype=f32)[:, None] * inv[None, :]
    cos, sin = jnp.cos(ang), jnp.sin(ang)
    reps = LANES // HEAD_DIM
    return (jnp.tile(cos, (1, 2 * reps)), jnp.tile(jnp.concatenate([-sin, sin], axis=1), (1, reps)))


def kernel(x, norm1_g, w_in, b_gate, qn_a, kn_a, lam_q1, lam_k1, lam_q2, lam_k2, subln_g, qn_b, kn_b,
           w_a_proj, w_b_proj, w_out, norm2_g, w_up, conv_w, conv_b, w_down):
    bsz, seq, d = x.shape
    depth = w_in.shape[0]
    assert seq % max(TM_INPROJ, TM_MERGE, TM_MLP) == 0 and d % LANES == 0
    cos_t, sin_t = _rope_tables(seq)
    seg = jnp.arange(LANES) // HEAD_DIM
    bd = (seg[:, None] == seg[None, :]).astype(bf16)
    reps = LANES // HEAD_DIM
    qscale = HEAD_DIM ** -0.5 * LOG2E
    for l in range(depth):
        lam_init = 0.8 - 0.6 * math.exp(-0.3 * l)
        gains = jnp.stack([jnp.tile(qn_a[l], reps) * qscale, jnp.tile(kn_a[l], reps),
                           jnp.tile(qn_b[l], reps) * qscale, jnp.tile(kn_b[l], reps)])
        q_all, k_all, vt_all, kmean, w_up_bf, w_down_bf = _inproj(
            x, norm1_g[l][None], w_in[l], gains, cos_t, sin_t, bd, (w_up[l], w_down[l]), tm=TM_INPROJ)
        kmean = kmean.reshape(bsz, seq // BLK, B_WIDTH)
        lamvec = jnp.stack([lam_q1[l], lam_k1[l], lam_q2[l], lam_k2[l]])
        gmax = jnp.max(jnp.abs(gains), axis=1) * (HEAD_DIM ** 0.5)
        subg_cols = jnp.tile(subln_g[l][:, None], (1, BLK))
        oa = _attention(q_all, k_all, vt_all, (lamvec, subg_cols),
                        gmax[0] * gmax[1] * BOUND_SLACK, moba=False, lam_init=lam_init)
        ob = _attention(q_all, k_all, vt_all, (kmean,),
                        gmax[2] * gmax[3] * BOUND_SLACK, moba=True, lam_init=lam_init)
        x1, h2 = _merge(oa, ob, x, norm1_g[l][None], w_in[l], b_gate[l][None],
                        w_a_proj[l], w_b_proj[l], w_out[l], norm2_g[l][None], tm=TM_MERGE)
        x = _mlp(h2.reshape(bsz * seq, d), x1.reshape(bsz * seq, d), w_up_bf, conv_w[l],
                 conv_b[l][None], w_down_bf, seq=seq, tm=TM_MLP).reshape(bsz, seq, d)
    return x
```

```python
import functools
import math

import jax
import jax.numpy as jnp
from jax import lax
from jax.experimental import pallas as pl
from jax.experimental.pallas import tpu as pltpu

HEAD_DIM = 64
A_HEADS = 4
A_VDIM = 2 * HEAD_DIM
A_WIDTH = A_HEADS * A_VDIM
B_HEADS = 8
B_WIDTH = B_HEADS * HEAD_DIM
B_BLOCK = 256
B_TOPK = 3
CONV_W = 3
ROPE_THETA = 10000.0
EPS = 1e-6

LANES = 128
BLK = B_BLOCK
PAIRS = A_WIDTH // LANES
LOG2E = 1.4426950408889634
NEG_BIG = -1e30
POS_BIG = 1e30
VMEM_LIMIT = 56 * 1024 * 1024
BF16_SUBLANES = 16
TM_INPROJ = 1024
TM_MERGE = 1024
TM_MLP = 512

f32 = jnp.float32
bf16 = jnp.bfloat16


def _nt_dot(a, b):
    return lax.dot_general(a, b, (((1,), (1,)), ((), ())), preferred_element_type=f32)


def _inproj_kernel(x_ref, g1_ref, w_ref, gain_ref, cos_ref, sin_ref, bd_ref, *rest, tm, n_late):
    late_f32 = rest[:n_late]
    q_ref, k_ref, vt_ref, kmean_ref = rest[n_late:n_late + 4]
    late_bf16 = rest[n_late + 4:2 * n_late + 4]
    h_scr = rest[-1]
    for src, dst in zip(late_f32, late_bf16):
        dst[...] = src[...].astype(bf16)
    nblk = tm // BLK
    x = x_ref[0]
    ms = jnp.mean(x * x, axis=-1, keepdims=True)
    h_scr[...] = (x * lax.rsqrt(ms + EPS) * g1_ref[...]).astype(bf16)
    cos = cos_ref[...]
    sin = sin_ref[...]
    bd = bd_ref[...]
    lane = lax.broadcasted_iota(jnp.int32, (tm, LANES), 1)
    first_half = (lane & (HEAD_DIM - 1)) < (HEAD_DIM // 2)
    low_half = lane < HEAD_DIM

    def col_tile(j):
        return jnp.dot(h_scr[...], w_ref[:, j * A_WIDTH:(j + 1) * A_WIDTH].astype(bf16),
                       preferred_element_type=f32)

    def norm_rope(y, gain):
        ss = jnp.dot((y * y).astype(bf16), bd, preferred_element_type=f32)
        yn = y * lax.rsqrt(ss * (1.0 / HEAD_DIM) + EPS) * gain
        rot = jnp.where(first_half, pltpu.roll(yn, LANES - HEAD_DIM // 2, 1),
                        pltpu.roll(yn, HEAD_DIM // 2, 1))
        return yn * cos + rot * sin

    for branch in range(2):
        acc_q = col_tile(3 * branch)
        for c in range(PAIRS):
            out = norm_rope(acc_q[:, c * LANES:(c + 1) * LANES], gain_ref[2 * branch:2 * branch + 1, :])
            lo = jnp.where(low_half, out, 0.0).astype(bf16)
            hi = jnp.where(low_half, 0.0, out).astype(bf16)
            for bi in range(nblk):
                q_ref[0, branch * PAIRS + c, bi, :BLK] = lo[bi * BLK:(bi + 1) * BLK]
                q_ref[0, branch * PAIRS + c, bi, BLK:] = hi[bi * BLK:(bi + 1) * BLK]
        acc_k = col_tile(3 * branch + 1)
        for c in range(PAIRS):
            out = norm_rope(acc_k[:, c * LANES:(c + 1) * LANES],
                            gain_ref[2 * branch + 1:2 * branch + 2, :])
            for bi in range(nblk):
                blk = out[bi * BLK:(bi + 1) * BLK]
                k_ref[0, branch * PAIRS + c, bi] = blk.astype(bf16)
                if branch == 1:
                    kmean_ref[0, bi:bi + 1, c * LANES:(c + 1) * LANES] = jnp.mean(
                        blk, axis=0, keepdims=True)
        acc_v = col_tile(3 * branch + 2)
        for c in range(PAIRS):
            for bi in range(nblk):
                blk = acc_v[bi * BLK:(bi + 1) * BLK, c * LANES:(c + 1) * LANES]
                vt_ref[0, branch * PAIRS + c, bi] = blk.T.astype(bf16)


def _inproj(x, g1, w_in, gains, cos_t, sin_t, bd, late_weights, *, tm):
    bsz, seq, d = x.shape
    steps = bsz * (seq // tm)
    slab = lambda b, i: (b * (seq // tm) + i, 0)
    late_specs = [pl.BlockSpec((w.shape[0] // steps, w.shape[1]), slab) for w in late_weights]
    assert all(w.shape[0] % (steps * BF16_SUBLANES) == 0 for w in late_weights)
    n_cols = 3 * A_WIDTH + 3 * B_WIDTH
    nblk = tm // BLK
    spt = seq // tm
    nb = seq // BLK
    q_shape = jax.ShapeDtypeStruct((bsz, 2 * PAIRS, nb, 2 * BLK, LANES), bf16)
    qk_shape = jax.ShapeDtypeStruct((bsz, 2 * PAIRS, nb, BLK, LANES), bf16)
    vt_shape = jax.ShapeDtypeStruct((bsz, 2 * PAIRS, nb, LANES, BLK), bf16)
    q_spec = pl.BlockSpec((1, 2 * PAIRS, nblk, 2 * BLK, LANES), lambda b, i: (b, 0, i, 0, 0))
    qk_spec = pl.BlockSpec((1, 2 * PAIRS, nblk, BLK, LANES), lambda b, i: (b, 0, i, 0, 0))
    vt_spec = pl.BlockSpec((1, 2 * PAIRS, nblk, LANES, BLK), lambda b, i: (b, 0, i, 0, 0))
    return pl.pallas_call(
        functools.partial(_inproj_kernel, tm=tm, n_late=len(late_weights)),
        grid=(bsz, spt),
        in_specs=[
            pl.BlockSpec((1, tm, d), lambda b, i: (b, i, 0)),
            pl.BlockSpec((1, d), lambda b, i: (0, 0)),
            pl.BlockSpec((d, n_cols), lambda b, i: (0, 0), pipeline_mode=pl.Buffered(1)),
            pl.BlockSpec((4, LANES), lambda b, i: (0, 0)),
            pl.BlockSpec((tm, LANES), lambda b, i: (i, 0)),
            pl.BlockSpec((tm, LANES), lambda b, i: (i, 0)),
            pl.BlockSpec((LANES, LANES), lambda b, i: (0, 0)),
        ] + late_specs,
        out_specs=[
            q_spec, qk_spec, vt_spec,
            pl.BlockSpec((1, nblk, B_WIDTH), lambda b, i: (b * spt + i, 0, 0)),
        ] + late_specs,
        out_shape=[
            q_shape, qk_shape, vt_shape,
            jax.ShapeDtypeStruct((bsz * spt, nblk, B_WIDTH), f32),
        ] + [jax.ShapeDtypeStruct(w.shape, bf16) for w in late_weights],
        scratch_shapes=[pltpu.VMEM((tm, d), bf16)],
        compiler_params=pltpu.CompilerParams(
            dimension_semantics=("arbitrary", "arbitrary"), vmem_limit_bytes=VMEM_LIMIT),
        name="inproj",
    )(x, g1, w_in, gains, cos_t, sin_t, bd, *late_weights)


TRIP_STEPS = 34
QK_AHEAD = 6
TILE_UNROLL = 16
BOUND_SLACK = 1.0 + 2.0 ** -6
MAX_FIXED_BOUND = 60.0


def _attn_kernel(itab_ref, jtab_ref, otab_ref, *refs, moba, online, lam_init, n_steps, n_masked):
    if moba:
        (q_ref, k_ref, vt_ref, mref_ref, kmean_ref, o_ref,
         m_scr, l_scr, acc_scr, s_scr, sel_scr) = refs
    else:
        (q_ref, k_ref, vt_ref, mref_ref, lam_ref, subg_ref, o_ref,
         m_scr, l_scr, acc_scr, s_scr) = refs
    nb = k_ref.shape[2]
    acc_rows = acc_scr.shape[1]
    key_minus_row = (lax.broadcasted_iota(jnp.int32, (BLK, 2 * BLK), 0)
                     - (lax.broadcasted_iota(jnp.int32, (BLK, 2 * BLK), 1) & (BLK - 1)))

    if moba:
        km = kmean_ref[0]
        km_parts = []
        for _ in range(3):
            part = km.astype(bf16)
            km_parts.append(part)
            km = km - part.astype(f32)

    nidx = lax.broadcasted_iota(jnp.int32, (nb, 2 * BLK), 0)
    for i in range(nb):
        if online:
            m_scr[i] = jnp.full((1, 2 * BLK), NEG_BIG, f32)
        l_scr[i] = jnp.zeros((1, 2 * BLK), f32)
        acc_scr[i] = jnp.zeros((acc_rows, 2 * BLK), f32)
        if moba:
            keep = nidx <= i
            if i > B_TOPK:
                qs = q_ref[0, 0, i]
                g = _nt_dot(km_parts[0], qs) + _nt_dot(km_parts[1], qs) + _nt_dot(km_parts[2], qs)
                cnt = jnp.zeros((nb, 2 * BLK), f32)
                for mm in range(i):
                    row = g[mm:mm + 1, :]
                    beats = (row > g) | ((row == g) & (nidx > mm))
                    cnt = cnt + jnp.where(beats, 1.0, 0.0)
                keep = ((cnt < float(B_TOPK)) & (nidx < i)) | (nidx == i)
            sel_scr[i] = jnp.where(keep, 1.0, 0.0)

    ring = s_scr.shape[0]

    def qk(step, u):
        s = _nt_dot(k_ref[0, 0, jtab_ref[step]], q_ref[0, 0, itab_ref[step]])
        if u < n_masked:
            s = jnp.where(key_minus_row <= otab_ref[step], s, NEG_BIG)
        s_scr[u % ring] = s

    def update(step, u):
        i = itab_ref[step]
        j = jtab_ref[step]
        s = s_scr[u % ring]
        if moba:
            chosen = sel_scr[i, pl.ds(j, 1), :] > 0.5
        if online:
            m_old = m_scr[i]
            m_new = jnp.maximum(m_old, jnp.max(s, axis=0, keepdims=True))
            if moba:
                m_new = jnp.where(chosen, m_new, m_old)
            m_sub = m_new
            alpha = jnp.exp2(m_old - m_new)
            m_scr[i] = m_new
        else:
            m_sub = mref_ref[...]
        if moba:
            m_sub = jnp.where(chosen, m_sub, POS_BIG)
        p = jnp.exp2(s - m_sub)
        psum = jnp.sum(p, axis=0, keepdims=True)
        pb = p.astype(bf16)
        vt = vt_ref[0, 0, j]
        pv = jnp.dot(vt, pb, preferred_element_type=f32)
        if online:
            l_scr[i] = alpha * l_scr[i] + psum
            acc_scr[i] = alpha * acc_scr[i] + pv
        else:
            l_scr[i] = l_scr[i] + psum
            acc_scr[i] = acc_scr[i] + pv

    ahead = ring - 1

    def trip(it, carry):
        base = it * TRIP_STEPS
        for u in range(ahead):
            qk(base + u, u)
        for u in range(TRIP_STEPS):
            if u + ahead < TRIP_STEPS:
                qk(base + u + ahead, u + ahead)
            update(base + u, u)
        return carry

    lax.fori_loop(0, n_steps // TRIP_STEPS, trip, 0)

    if not moba:
        lv = lam_ref[...]
        lam = (jnp.exp(jnp.sum(lv[0:1] * lv[1:2], axis=-1, keepdims=True))
               - jnp.exp(jnp.sum(lv[2:3] * lv[3:4], axis=-1, keepdims=True)) + lam_init)

    def finish_tiles(t, carry):
        for r in range(TILE_UNROLL):
            i = t * TILE_UNROLL + r
            o = acc_scr[i] * (1.0 / l_scr[i])
            rows = pl.ds(pl.multiple_of(i * BLK, BLK), BLK)
            if moba:
                ot = jnp.concatenate([o[:HEAD_DIM, :BLK], o[HEAD_DIM:, BLK:]], axis=0).T
                o_ref[0, rows, :] = ot.astype(bf16)
            else:
                od = o[:, :BLK] - lam * o[:, BLK:]
                ms = jnp.mean(od * od, axis=0, keepdims=True)
                od = od * lax.rsqrt(ms + EPS) * subg_ref[...] * (1.0 - lam_init)
                o_ref[0, rows, :] = od.T.astype(bf16)
        return carry

    lax.fori_loop(0, nb // TILE_UNROLL, finish_tiles, 0)


def _step_tables(nb):
    per_trip = TRIP_STEPS
    diag = [(i, i, 0) for i in range(nb)]
    rest = [(i, j, BLK) for j in range(nb) for i in range(j + 1, nb)]
    n_steps = len(diag) + len(rest)
    assert n_steps % per_trip == 0
    n_trips = n_steps // per_trip
    n_masked = -(-len(diag) // n_trips)
    assert n_masked <= per_trip
    steps = []
    for _ in range(n_trips):
        for _ in range(n_masked):
            steps.append(diag.pop(0) if diag else rest.pop(0))
        for _ in range(per_trip - n_masked):
            steps.append(rest.pop(0))
    assert not diag and not rest
    return [jnp.asarray([s[c] for s in steps], jnp.int32) for c in range(3)], n_steps, n_masked


def _attention(q_all, k_all, vt_all, aux, score_bound, *, moba, lam_init):
    bsz, _, nb, _, _ = k_all.shape
    off = PAIRS if moba else 0
    assert nb % TILE_UNROLL == 0
    (itab, jtab, otab), n_steps, n_masked = _step_tables(nb)
    const = lambda b, h, it, jt, ot: (0, 0)
    in_specs = [
        pl.BlockSpec((1, 1, nb, 2 * BLK, LANES), lambda b, h, it, jt, ot: (b, h + off, 0, 0, 0)),
        pl.BlockSpec((1, 1, nb, BLK, LANES), lambda b, h, it, jt, ot: (b, h + off, 0, 0, 0)),
        pl.BlockSpec((1, 1, nb, LANES, BLK), lambda b, h, it, jt, ot: (b, h + off, 0, 0, 0)),
        pl.BlockSpec((1, 2 * BLK), const),
    ]
    scratch = [pltpu.VMEM((nb, 1, 2 * BLK), f32), pltpu.VMEM((nb, 1, 2 * BLK), f32),
               pltpu.VMEM((nb, LANES, 2 * BLK), f32),
               pltpu.VMEM((QK_AHEAD + 1, BLK, 2 * BLK), f32)]
    if moba:
        (kmean,) = aux
        in_specs.append(pl.BlockSpec((1, nb, LANES), lambda b, h, it, jt, ot: (b, 0, h)))
        scratch.append(pltpu.VMEM((nb, nb, 2 * BLK), f32))
    else:
        lamvec, subg = aux
        in_specs.append(pl.BlockSpec(lamvec.shape, const))
        in_specs.append(pl.BlockSpec(subg.shape, const))
    mref = jnp.full((1, 2 * BLK), score_bound, f32)

    def run(online):
        return pl.pallas_call(
            functools.partial(_attn_kernel, moba=moba, online=online, lam_init=lam_init,
                              n_steps=n_steps, n_masked=n_masked),
            grid_spec=pltpu.PrefetchScalarGridSpec(
                num_scalar_prefetch=3,
                grid=(bsz, PAIRS),
                in_specs=in_specs,
                out_specs=pl.BlockSpec((1, nb * BLK, LANES), lambda b, h, it, jt, ot: (b, 0, h)),
                scratch_shapes=scratch),
            out_shape=jax.ShapeDtypeStruct((bsz, nb * BLK, PAIRS * LANES), bf16),
            compiler_params=pltpu.CompilerParams(
                dimension_semantics=("arbitrary", "arbitrary"), vmem_limit_bytes=VMEM_LIMIT),
            name=("moba_attn" if moba else "diff_attn") + ("_online" if online else ""),
        )(itab, jtab, otab, q_all, k_all, vt_all, mref, *aux)

    return lax.cond(score_bound <= MAX_FIXED_BOUND, lambda: run(False), lambda: run(True))


def _merge_kernel(oa_ref, ob_ref, x_ref, g1_ref, wga_ref, wgb_ref, bg_ref, wa_ref, wb_ref, wo_ref,
                  g2_ref, x1_ref, h2_ref):
    d = x_ref.shape[-1]
    x = x_ref[0]
    ms = jnp.mean(x * x, axis=-1, keepdims=True)
    h = (x * lax.rsqrt(ms + EPS) * g1_ref[...]).astype(bf16)
    merged = None
    for br, (wg_ref, o_ref, w_ref) in enumerate(((wga_ref, oa_ref, wa_ref), (wgb_ref, ob_ref, wb_ref))):
        gate = jax.nn.sigmoid(jnp.dot(h, wg_ref[...].astype(bf16), preferred_element_type=f32)
                              + bg_ref[:, br * d:(br + 1) * d])
        term = gate * jnp.dot(o_ref[0], w_ref[...].astype(bf16), preferred_element_type=f32)
        merged = term if merged is None else merged + term
    x1 = x + jnp.dot(merged.astype(bf16), wo_ref[...].astype(bf16), preferred_element_type=f32)
    x1_ref[0] = x1
    ms = jnp.mean(x1 * x1, axis=-1, keepdims=True)
    h2_ref[0] = (x1 * lax.rsqrt(ms + EPS) * g2_ref[...]).astype(bf16)


def _merge(oa, ob, x, g1, w_in, bg, wa, wb, wo, g2, *, tm):
    bsz, seq, d = x.shape
    gate_blk = (w_in.shape[1] - 2 * d) // d
    assert gate_blk * d + 2 * d == w_in.shape[1]
    const = lambda b, i: (0, 0)
    once = pl.Buffered(1)
    tile = lambda b, i: (b, i, 0)
    return pl.pallas_call(
        _merge_kernel,
        grid=(bsz, seq // tm),
        in_specs=[
            pl.BlockSpec((1, tm, oa.shape[-1]), tile),
            pl.BlockSpec((1, tm, ob.shape[-1]), tile),
            pl.BlockSpec((1, tm, d), tile),
            pl.BlockSpec((1, d), const),
            pl.BlockSpec((d, d), lambda b, i: (0, gate_blk), pipeline_mode=once),
            pl.BlockSpec((d, d), lambda b, i: (0, gate_blk + 1), pipeline_mode=once),
            pl.BlockSpec(bg.shape, const),
            pl.BlockSpec(wa.shape, const, pipeline_mode=once),
            pl.BlockSpec(wb.shape, const, pipeline_mode=once),
            pl.BlockSpec(wo.shape, const, pipeline_mode=once),
            pl.BlockSpec((1, d), const),
        ],
        out_specs=[pl.BlockSpec((1, tm, d), tile), pl.BlockSpec((1, tm, d), tile)],
        out_shape=[jax.ShapeDtypeStruct((bsz, seq, d), f32),
                   jax.ShapeDtypeStruct((bsz, seq, d), bf16)],
        compiler_params=pltpu.CompilerParams(
            dimension_semantics=("arbitrary", "arbitrary"), vmem_limit_bytes=VMEM_LIMIT),
        name="merge_outproj",
    )(oa, ob, x, g1, w_in, w_in, bg, wa, wb, wo, g2)


HALO = 8
assert HALO >= CONV_W - 1


def _mlp_kernel(h2_ref, x1_ref, wup_ref, cw_ref, cb_ref, wd_ref, o_ref, u_scr, *, tm, tiles_per_seq):
    i = pl.program_id(0)
    dff = wd_ref.shape[0]

    @pl.when(i % tiles_per_seq == 0)
    def _():
        u_scr[:HALO, :] = jnp.zeros((HALO, dff), f32)

    ug = jnp.dot(h2_ref[...], wup_ref[...], preferred_element_type=f32)
    u, g = ug[:, :dff], ug[:, dff:]
    u_scr[HALO:, :] = u
    y = cb_ref[...]
    for j in range(CONV_W):
        y = y + u_scr[pl.ds(HALO - (CONV_W - 1) + j, tm), :] * cw_ref[j:j + 1, :]
    u_scr[:HALO, :] = u[tm - HALO:, :]
    act = (jax.nn.gelu(y) * g).astype(bf16)
    o_ref[...] = x1_ref[...] + jnp.dot(act, wd_ref[...], preferred_element_type=f32)


def _mlp(h2, x1, w_up_bf, conv_w, conv_b, w_down_bf, *, seq, tm):
    n, d = h2.shape
    dff = w_down_bf.shape[0]
    const = lambda i: (0, 0)
    return pl.pallas_call(
        functools.partial(_mlp_kernel, tm=tm, tiles_per_seq=seq // tm),
        grid=(n // tm,),
        in_specs=[
            pl.BlockSpec((tm, d), lambda i: (i, 0)),
            pl.BlockSpec((tm, d), lambda i: (i, 0)),
            pl.BlockSpec(w_up_bf.shape, const, pipeline_mode=pl.Buffered(1)),
            pl.BlockSpec(conv_w.shape, const),
            pl.BlockSpec(conv_b.shape, const),
            pl.BlockSpec(w_down_bf.shape, const, pipeline_mode=pl.Buffered(1)),
        ],
        out_specs=pl.BlockSpec((tm, d), lambda i: (i, 0)),
        out_shape=jax.ShapeDtypeStruct((n, d), f32),
        scratch_shapes=[pltpu.VMEM((tm + HALO, dff), f32)],
        compiler_params=pltpu.CompilerParams(
            dimension_semantics=("arbitrary",), vmem_limit_bytes=VMEM_LIMIT),
        name="conv_mlp",
    )(h2, x1, w_up_bf, conv_w, conv_b, w_down_bf)


def _rope_tables(seq):
    inv = 1.0 / (ROPE_THETA ** (jnp.arange(0, HEAD_DIM, 2, dtype=f32) / HEAD_DIM))
    ang = jnp.arange(seq, dtype=f32)[:, None] * inv[None, :]
    cos, sin = jnp.cos(ang), jnp.sin(ang)
    reps = LANES // HEAD_DIM
    return (jnp.tile(cos, (1, 2 * reps)), jnp.tile(jnp.concatenate([-sin, sin], axis=1), (1, reps)))


def kernel(x, norm1_g, w_in, b_gate, qn_a, kn_a, lam_q1, lam_k1, lam_q2, lam_k2, subln_g, qn_b, kn_b,
           w_a_proj, w_b_proj, w_out, norm2_g, w_up, conv_w, conv_b, w_down):
    bsz, seq, d = x.shape
    depth = w_in.shape[0]
    assert seq % max(TM_INPROJ, TM_MERGE, TM_MLP) == 0 and d % LANES == 0
    cos_t, sin_t = _rope_tables(seq)
    seg = jnp.arange(LANES) // HEAD_DIM
    bd = (seg[:, None] == seg[None, :]).astype(bf16)
    reps = LANES // HEAD_DIM
    qscale = HEAD_DIM ** -0.5 * LOG2E
    for l in range(depth):
        lam_init = 0.8 - 0.6 * math.exp(-0.3 * l)
        gains = jnp.stack([jnp.tile(qn_a[l], reps) * qscale, jnp.tile(kn_a[l], reps),
                           jnp.tile(qn_b[l], reps) * qscale, jnp.tile(kn_b[l], reps)])
        q_all, k_all, vt_all, kmean, w_up_bf, w_down_bf = _inproj(
            x, norm1_g[l][None], w_in[l], gains, cos_t, sin_t, bd, (w_up[l], w_down[l]), tm=TM_INPROJ)
        kmean = kmean.reshape(bsz, seq // BLK, B_WIDTH)
        lamvec = jnp.stack([lam_q1[l], lam_k1[l], lam_q2[l], lam_k2[l]])
        gmax = jnp.max(jnp.abs(gains), axis=1) * (HEAD_DIM ** 0.5)
        subg_cols = jnp.tile(subln_g[l][:, None], (1, BLK))
        oa = _attention(q_all, k_all, vt_all, (lamvec, subg_cols),
                        gmax[0] * gmax[1] * BOUND_SLACK, moba=False, lam_init=lam_init)
        ob = _attention(q_all, k_all, vt_all, (kmean,),
                        gmax[2] * gmax[3] * BOUND_SLACK, moba=True, lam_init=lam_init)
        x1, h2 = _merge(oa, ob, x, norm1_g[l][None], w_in[l], b_gate[l][None],
                        w_a_proj[l], w_b_proj[l], w_out[l], norm2_g[l][None], tm=TM_MERGE)
        x = _mlp(h2.reshape(bsz * seq, d), x1.reshape(bsz * seq, d), w_up_bf, conv_w[l],
                 conv_b[l][None], w_down_bf, seq=seq, tm=TM_MLP).reshape(bsz, seq, d)
    return x
```

```python
import functools
import math

import jax
import jax.numpy as jnp
from jax import lax
from jax.experimental import pallas as pl
from jax.experimental.pallas import tpu as pltpu

HEAD_DIM = 64
A_HEADS = 4
A_VDIM = 2 * HEAD_DIM
A_WIDTH = A_HEADS * A_VDIM
B_HEADS = 8
B_WIDTH = B_HEADS * HEAD_DIM
B_BLOCK = 256
B_TOPK = 3
CONV_W = 3
ROPE_THETA = 10000.0
EPS = 1e-6

LANES = 128
BLK = B_BLOCK
PAIRS = A_WIDTH // LANES
LOG2E = 1.4426950408889634
NEG_BIG = -1e30
POS_BIG = 1e30
VMEM_LIMIT = 56 * 1024 * 1024
BF16_SUBLANES = 16
TM_INPROJ = 1024
TM_MERGE = 1024
TM_MLP = 512

f32 = jnp.float32
bf16 = jnp.bfloat16


def _nt_dot(a, b):
    return lax.dot_general(a, b, (((1,), (1,)), ((), ())), preferred_element_type=f32)


def _inproj_kernel(x_ref, g1_ref, w_ref, gain_ref, cos_ref, sin_ref, bd_ref, *rest, tm, n_late):
    late_f32 = rest[:n_late]
    q_ref, k_ref, vt_ref, kmean_ref = rest[n_late:n_late + 4]
    late_bf16 = rest[n_late + 4:2 * n_late + 4]
    h_scr = rest[-1]
    for src, dst in zip(late_f32, late_bf16):
        dst[...] = src[...].astype(bf16)
    nblk = tm // BLK
    x = x_ref[0]
    ms = jnp.mean(x * x, axis=-1, keepdims=True)
    h_scr[...] = (x * lax.rsqrt(ms + EPS) * g1_ref[...]).astype(bf16)
    cos = cos_ref[...]
    sin = sin_ref[...]
    bd = bd_ref[...]
    lane = lax.broadcasted_iota(jnp.int32, (tm, LANES), 1)
    first_half = (lane & (HEAD_DIM - 1)) < (HEAD_DIM // 2)
    low_half = lane < HEAD_DIM

    def col_tile(j):
        return jnp.dot(h_scr[...], w_ref[:, j * A_WIDTH:(j + 1) * A_WIDTH].astype(bf16),
                       preferred_element_type=f32)

    def norm_rope(y, gain):
        ss = jnp.dot((y * y).astype(bf16), bd, preferred_element_type=f32)
        yn = y * lax.rsqrt(ss * (1.0 / HEAD_DIM) + EPS) * gain
        rot = jnp.where(first_half, pltpu.roll(yn, LANES - HEAD_DIM // 2, 1),
                        pltpu.roll(yn, HEAD_DIM // 2, 1))
        return yn * cos + rot * sin

    for branch in range(2):
        acc_q = col_tile(3 * branch)
        for c in range(PAIRS):
            out = norm_rope(acc_q[:, c * LANES:(c + 1) * LANES], gain_ref[2 * branch:2 * branch + 1, :])
            lo = jnp.where(low_half, out, 0.0).astype(bf16)
            hi = jnp.where(low_half, 0.0, out).astype(bf16)
            for bi in range(nblk):
                q_ref[0, branch * PAIRS + c, bi, :BLK] = lo[bi * BLK:(bi + 1) * BLK]
                q_ref[0, branch * PAIRS + c, bi, BLK:] = hi[bi * BLK:(bi + 1) * BLK]
        acc_k = col_tile(3 * branch + 1)
        for c in range(PAIRS):
            out = norm_rope(acc_k[:, c * LANES:(c + 1) * LANES],
                            gain_ref[2 * branch + 1:2 * branch + 2, :])
            for bi in range(nblk):
                blk = out[bi * BLK:(bi + 1) * BLK]
                k_ref[0, branch * PAIRS + c, bi] = blk.astype(bf16)
                if branch == 1:
                    kmean_ref[0, bi:bi + 1, c * LANES:(c + 1) * LANES] = jnp.mean(
                        blk, axis=0, keepdims=True)
        acc_v = col_tile(3 * branch + 2)
        for c in range(PAIRS):
            for bi in range(nblk):
                blk = acc_v[bi * BLK:(bi + 1) * BLK, c * LANES:(c + 1) * LANES]
                vt_ref[0, branch * PAIRS + c, bi] = blk.T.astype(bf16)


def _inproj(x, g1, w_in, gains, cos_t, sin_t, bd, late_weights, *, tm):
    bsz, seq, d = x.shape
    steps = bsz * (seq // tm)
    slab = lambda b, i: (b * (seq // tm) + i, 0)
    late_specs = [pl.BlockSpec((w.shape[0] // steps, w.shape[1]), slab) for w in late_weights]
    assert all(w.shape[0] % (steps * BF16_SUBLANES) == 0 for w in late_weights)
    n_cols = 3 * A_WIDTH + 3 * B_WIDTH
    nblk = tm // BLK
    spt = seq // tm
    nb = seq // BLK
    q_shape = jax.ShapeDtypeStruct((bsz, 2 * PAIRS, nb, 2 * BLK, LANES), bf16)
    qk_shape = jax.ShapeDtypeStruct((bsz, 2 * PAIRS, nb, BLK, LANES), bf16)
    vt_shape = jax.ShapeDtypeStruct((bsz, 2 * PAIRS, nb, LANES, BLK), bf16)
    q_spec = pl.BlockSpec((1, 2 * PAIRS, nblk, 2 * BLK, LANES), lambda b, i: (b, 0, i, 0, 0))
    qk_spec = pl.BlockSpec((1, 2 * PAIRS, nblk, BLK, LANES), lambda b, i: (b, 0, i, 0, 0))
    vt_spec = pl.BlockSpec((1, 2 * PAIRS, nblk, LANES, BLK), lambda b, i: (b, 0, i, 0, 0))
    return pl.pallas_call(
        functools.partial(_inproj_kernel, tm=tm, n_late=len(late_weights)),
        grid=(bsz, spt),
        in_specs=[
            pl.BlockSpec((1, tm, d), lambda b, i: (b, i, 0)),
            pl.BlockSpec((1, d), lambda b, i: (0, 0)),
            pl.BlockSpec((d, n_cols), lambda b, i: (0, 0), pipeline_mode=pl.Buffered(1)),
            pl.BlockSpec((4, LANES), lambda b, i: (0, 0)),
            pl.BlockSpec((tm, LANES), lambda b, i: (i, 0)),
            pl.BlockSpec((tm, LANES), lambda b, i: (i, 0)),
            pl.BlockSpec((LANES, LANES), lambda b, i: (0, 0)),
        ] + late_specs,
        out_specs=[
            q_spec, qk_spec, vt_spec,
            pl.BlockSpec((1, nblk, B_WIDTH), lambda b, i: (b * spt + i, 0, 0)),
        ] + late_specs,
        out_shape=[
            q_shape, qk_shape, vt_shape,
            jax.ShapeDtypeStruct((bsz * spt, nblk, B_WIDTH), f32),
        ] + [jax.ShapeDtypeStruct(w.shape, bf16) for w in late_weights],
        scratch_shapes=[pltpu.VMEM((tm, d), bf16)],
        compiler_params=pltpu.CompilerParams(
            dimension_semantics=("arbitrary", "arbitrary"), vmem_limit_bytes=VMEM_LIMIT),
        name="inproj",
    )(x, g1, w_in, gains, cos_t, sin_t, bd, *late_weights)


TRIP_STEPS = 34
QK_AHEAD = 6
TILE_UNROLL = {False: 16, True: 4}
BOUND_SLACK = 1.0 + 2.0 ** -6
MAX_FIXED_BOUND = 60.0


def _attn_kernel(itab_ref, jtab_ref, otab_ref, *refs, moba, online, lam_init, n_steps, n_masked):
    if moba:
        (q_ref, k_ref, vt_ref, mref_ref, kmean_ref, o_ref,
         m_scr, l_scr, acc_scr, s_scr, sel_scr) = refs
    else:
        (q_ref, k_ref, vt_ref, mref_ref, lam_ref, subg_ref, o_ref,
         m_scr, l_scr, acc_scr, s_scr) = refs
    nb = k_ref.shape[2]
    acc_rows = acc_scr.shape[1]
    key_minus_row = (lax.broadcasted_iota(jnp.int32, (BLK, 2 * BLK), 0)
                     - (lax.broadcasted_iota(jnp.int32, (BLK, 2 * BLK), 1) & (BLK - 1)))

    if moba:
        km = kmean_ref[0]
        km_parts = []
        for _ in range(3):
            part = km.astype(bf16)
            km_parts.append(part)
            km = km - part.astype(f32)

    nidx = lax.broadcasted_iota(jnp.int32, (nb, 2 * BLK), 0)
    for i in range(nb):
        if online:
            m_scr[i] = jnp.full((1, 2 * BLK), NEG_BIG, f32)
        l_scr[i] = jnp.zeros((1, 2 * BLK), f32)
        acc_scr[i] = jnp.zeros((acc_rows, 2 * BLK), f32)
        if moba:
            keep = nidx <= i
            if i > B_TOPK:
                qs = q_ref[0, 0, i]
                g = _nt_dot(km_parts[0], qs) + _nt_dot(km_parts[1], qs) + _nt_dot(km_parts[2], qs)
                cnt = jnp.zeros((nb, 2 * BLK), f32)
                for mm in range(i):
                    row = g[mm:mm + 1, :]
                    beats = (row > g) | ((row == g) & (nidx > mm))
                    cnt = cnt + jnp.where(beats, 1.0, 0.0)
                keep = ((cnt < float(B_TOPK)) & (nidx < i)) | (nidx == i)
            sel_scr[i] = jnp.where(keep, 1.0, 0.0)

    ring = s_scr.shape[0]

    def qk(step, u):
        s = _nt_dot(k_ref[0, 0, jtab_ref[step]], q_ref[0, 0, itab_ref[step]])
        if u < n_masked:
            s = jnp.where(key_minus_row <= otab_ref[step], s, NEG_BIG)
        s_scr[u % ring] = s

    def update(step, u):
        i = itab_ref[step]
        j = jtab_ref[step]
        s = s_scr[u % ring]
        if moba:
            chosen = sel_scr[i, pl.ds(j, 1), :] > 0.5
        if online:
            m_old = m_scr[i]
            m_new = jnp.maximum(m_old, jnp.max(s, axis=0, keepdims=True))
            if moba:
                m_new = jnp.where(chosen, m_new, m_old)
            m_sub = m_new
            alpha = jnp.exp2(m_old - m_new)
            m_scr[i] = m_new
        else:
            m_sub = mref_ref[...]
        if moba:
            m_sub = jnp.where(chosen, m_sub, POS_BIG)
        p = jnp.exp2(s - m_sub)
        psum = jnp.sum(p, axis=0, keepdims=True)
        pb = p.astype(bf16)
        vt = vt_ref[0, 0, j]
        pv = jnp.dot(vt, pb, preferred_element_type=f32)
        if online:
            l_scr[i] = alpha * l_scr[i] + psum
            acc_scr[i] = alpha * acc_scr[i] + pv
        else:
            l_scr[i] = l_scr[i] + psum
            acc_scr[i] = acc_scr[i] + pv

    ahead = ring - 1

    def trip(it, carry):
        base = it * TRIP_STEPS
        for u in range(ahead):
            qk(base + u, u)
        for u in range(TRIP_STEPS):
            if u + ahead < TRIP_STEPS:
                qk(base + u + ahead, u + ahead)
            update(base + u, u)
        return carry

    lax.fori_loop(0, n_steps // TRIP_STEPS, trip, 0)

    if not moba:
        lv = lam_ref[...]
        lam = (jnp.exp(jnp.sum(lv[0:1] * lv[1:2], axis=-1, keepdims=True))
               - jnp.exp(jnp.sum(lv[2:3] * lv[3:4], axis=-1, keepdims=True)) + lam_init)

    def finish_tiles(t, carry):
        for r in range(TILE_UNROLL[moba]):
            i = t * TILE_UNROLL[moba] + r
            o = acc_scr[i] * (1.0 / l_scr[i])
            rows = pl.ds(pl.multiple_of(i * BLK, BLK), BLK)
            if moba:
                ot = jnp.concatenate([o[:HEAD_DIM, :BLK], o[HEAD_DIM:, BLK:]], axis=0).T
                o_ref[0, rows, :] = ot.astype(bf16)
            else:
                od = o[:, :BLK] - lam * o[:, BLK:]
                ms = jnp.mean(od * od, axis=0, keepdims=True)
                od = od * lax.rsqrt(ms + EPS) * subg_ref[...] * (1.0 - lam_init)
                o_ref[0, rows, :] = od.T.astype(bf16)
        return carry

    lax.fori_loop(0, nb // TILE_UNROLL[moba], finish_tiles, 0)


def _step_tables(nb):
    per_trip = TRIP_STEPS
    diag = [(i, i, 0) for i in range(nb)]
    rest = [(i, j, BLK) for j in range(nb) for i in range(j + 1, nb)]
    n_steps = len(diag) + len(rest)
    assert n_steps % per_trip == 0
    n_trips = n_steps // per_trip
    n_masked = -(-len(diag) // n_trips)
    assert n_masked <= per_trip
    steps = []
    for _ in range(n_trips):
        for _ in range(n_masked):
            steps.append(diag.pop(0) if diag else rest.pop(0))
        for _ in range(per_trip - n_masked):
            steps.append(rest.pop(0))
    assert not diag and not rest
    return [jnp.asarray([s[c] for s in steps], jnp.int32) for c in range(3)], n_steps, n_masked


def _attention(q_all, k_all, vt_all, aux, score_bound, *, moba, lam_init):
    bsz, _, nb, _, _ = k_all.shape
    off = PAIRS if moba else 0
    assert nb % TILE_UNROLL[moba] == 0
    (itab, jtab, otab), n_steps, n_masked = _step_tables(nb)
    const = lambda b, h, it, jt, ot: (0, 0)
    in_specs = [
        pl.BlockSpec((1, 1, nb, 2 * BLK, LANES), lambda b, h, it, jt, ot: (b, h + off, 0, 0, 0)),
        pl.BlockSpec((1, 1, nb, BLK, LANES), lambda b, h, it, jt, ot: (b, h + off, 0, 0, 0)),
        pl.BlockSpec((1, 1, nb, LANES, BLK), lambda b, h, it, jt, ot: (b, h + off, 0, 0, 0)),
        pl.BlockSpec((1, 2 * BLK), const),
    ]
    scratch = [pltpu.VMEM((nb, 1, 2 * BLK), f32), pltpu.VMEM((nb, 1, 2 * BLK), f32),
               pltpu.VMEM((nb, LANES, 2 * BLK), f32),
               pltpu.VMEM((QK_AHEAD + 1, BLK, 2 * BLK), f32)]
    if moba:
        (kmean,) = aux
        in_specs.append(pl.BlockSpec((1, nb, LANES), lambda b, h, it, jt, ot: (b, 0, h)))
        scratch.append(pltpu.VMEM((nb, nb, 2 * BLK), f32))
    else:
        lamvec, subg = aux
        in_specs.append(pl.BlockSpec(lamvec.shape, const))
        in_specs.append(pl.BlockSpec(subg.shape, const))
    mref = jnp.full((1, 2 * BLK), score_bound, f32)

    def run(online):
        return pl.pallas_call(
            functools.partial(_attn_kernel, moba=moba, online=online, lam_init=lam_init,
                              n_steps=n_steps, n_masked=n_masked),
            grid_spec=pltpu.PrefetchScalarGridSpec(
                num_scalar_prefetch=3,
                grid=(bsz, PAIRS),
                in_specs=in_specs,
                out_specs=pl.BlockSpec((1, nb * BLK, LANES), lambda b, h, it, jt, ot: (b, 0, h)),
                scratch_shapes=scratch),
            out_shape=jax.ShapeDtypeStruct((bsz, nb * BLK, PAIRS * LANES), bf16),
            compiler_params=pltpu.CompilerParams(
                dimension_semantics=("arbitrary", "arbitrary"), vmem_limit_bytes=VMEM_LIMIT),
            name=("moba_attn" if moba else "diff_attn") + ("_online" if online else ""),
        )(itab, jtab, otab, q_all, k_all, vt_all, mref, *aux)

    return lax.cond(score_bound <= MAX_FIXED_BOUND, lambda: run(False), lambda: run(True))


def _merge_kernel(oa_ref, ob_ref, x_ref, g1_ref, wga_ref, wgb_ref, bg_ref, wa_ref, wb_ref, wo_ref,
                  g2_ref, x1_ref, h2_ref):
    d = x_ref.shape[-1]
    x = x_ref[0]
    ms = jnp.mean(x * x, axis=-1, keepdims=True)
    h = (x * lax.rsqrt(ms + EPS) * g1_ref[...]).astype(bf16)
    merged = None
    for br, (wg_ref, o_ref, w_ref) in enumerate(((wga_ref, oa_ref, wa_ref), (wgb_ref, ob_ref, wb_ref))):
        gate = jax.nn.sigmoid(jnp.dot(h, wg_ref[...].astype(bf16), preferred_element_type=f32)
                              + bg_ref[:, br * d:(br + 1) * d])
        term = gate * jnp.dot(o_ref[0], w_ref[...].astype(bf16), preferred_element_type=f32)
        merged = term if merged is None else merged + term
    x1 = x + jnp.dot(merged.astype(bf16), wo_ref[...].astype(bf16), preferred_element_type=f32)
    x1_ref[0] = x1
    ms = jnp.mean(x1 * x1, axis=-1, keepdims=True)
    h2_ref[0] = (x1 * lax.rsqrt(ms + EPS) * g2_ref[...]).astype(bf16)


def _merge(oa, ob, x, g1, w_in, bg, wa, wb, wo, g2, *, tm):
    bsz, seq, d = x.shape
    gate_blk = (w_in.shape[1] - 2 * d) // d
    assert gate_blk * d + 2 * d == w_in.shape[1]
    const = lambda b, i: (0, 0)
    once = pl.Buffered(1)
    tile = lambda b, i: (b, i, 0)
    return pl.pallas_call(
        _merge_kernel,
        grid=(bsz, seq // tm),
        in_specs=[
            pl.BlockSpec((1, tm, oa.shape[-1]), tile),
            pl.BlockSpec((1, tm, ob.shape[-1]), tile),
            pl.BlockSpec((1, tm, d), tile),
            pl.BlockSpec((1, d), const),
            pl.BlockSpec((d, d), lambda b, i: (0, gate_blk), pipeline_mode=once),
            pl.BlockSpec((d, d), lambda b, i: (0, gate_blk + 1), pipeline_mode=once),
            pl.BlockSpec(bg.shape, const),
            pl.BlockSpec(wa.shape, const, pipeline_mode=once),
            pl.BlockSpec(wb.shape, const, pipeline_mode=once),
            pl.BlockSpec(wo.shape, const, pipeline_mode=once),
            pl.BlockSpec((1, d), const),
        ],
        out_specs=[pl.BlockSpec((1, tm, d), tile), pl.BlockSpec((1, tm, d), tile)],
        out_shape=[jax.ShapeDtypeStruct((bsz, seq, d), f32),
                   jax.ShapeDtypeStruct((bsz, seq, d), bf16)],
        compiler_params=pltpu.CompilerParams(
            dimension_semantics=("arbitrary", "arbitrary"), vmem_limit_bytes=VMEM_LIMIT),
        name="merge_outproj",
    )(oa, ob, x, g1, w_in, w_in, bg, wa, wb, wo, g2)


HALO = 8
assert HALO >= CONV_W - 1


def _mlp_kernel(h2_ref, x1_ref, wup_ref, cw_ref, cb_ref, wd_ref, o_ref, u_scr, *, tm, tiles_per_seq):
    i = pl.program_id(0)
    dff = wd_ref.shape[0]

    @pl.when(i % tiles_per_seq == 0)
    def _():
        u_scr[:HALO, :] = jnp.zeros((HALO, dff), f32)

    ug = jnp.dot(h2_ref[...], wup_ref[...], preferred_element_type=f32)
    u, g = ug[:, :dff], ug[:, dff:]
    u_scr[HALO:, :] = u
    y = cb_ref[...]
    for j in range(CONV_W):
        y = y + u_scr[pl.ds(HALO - (CONV_W - 1) + j, tm), :] * cw_ref[j:j + 1, :]
    u_scr[:HALO, :] = u[tm - HALO:, :]
    act = (jax.nn.gelu(y) * g).astype(bf16)
    o_ref[...] = x1_ref[...] + jnp.dot(act, wd_ref[...], preferred_element_type=f32)


def _mlp(h2, x1, w_up_bf, conv_w, conv_b, w_down_bf, *, seq, tm):
    n, d = h2.shape
    dff = w_down_bf.shape[0]
    const = lambda i: (0, 0)
    return pl.pallas_call(
        functools.partial(_mlp_kernel, tm=tm, tiles_per_seq=seq // tm),
        grid=(n // tm,),
        in_specs=[
            pl.BlockSpec((tm, d), lambda i: (i, 0)),
            pl.BlockSpec((tm, d), lambda i: (i, 0)),
            pl.BlockSpec(w_up_bf.shape, const, pipeline_mode=pl.Buffered(1)),
            pl.BlockSpec(conv_w.shape, const),
            pl.BlockSpec(conv_b.shape, const),
            pl.BlockSpec(w_down_bf.shape, const, pipeline_mode=pl.Buffered(1)),
        ],
        out_specs=pl.BlockSpec((tm, d), lambda i: (i, 0)),
        out_shape=jax.ShapeDtypeStruct((n, d), f32),
        scratch_shapes=[pltpu.VMEM((tm + HALO, dff), f32)],
        compiler_params=pltpu.CompilerParams(
            dimension_semantics=("arbitrary",), vmem_limit_bytes=VMEM_LIMIT),
        name="conv_mlp",
    )(h2, x1, w_up_bf, conv_w, conv_b, w_down_bf)


def _rope_tables(seq):
    inv = 1.0 / (ROPE_THETA ** (jnp.arange(0, HEAD_DIM, 2, dtype=f32) / HEAD_DIM))
    ang = jnp.arange(seq, dtype=f32)[:, None] * inv[None, :]
    cos, sin = jnp.cos(ang), jnp.sin(ang)
    reps = LANES // HEAD_DIM
    return (jnp.tile(cos, (1, 2 * reps)), jnp.tile(jnp.concatenate([-sin, sin], axis=1), (1, reps)))


def kernel(x, norm1_g, w_in, b_gate, qn_a, kn_a, lam_q1, lam_k1, lam_q2, lam_k2, subln_g, qn_b, kn_b,
           w_a_proj, w_b_proj, w_out, norm2_g, w_up, conv_w, conv_b, w_down):
    bsz, seq, d = x.shape
    depth = w_in.shape[0]
    assert seq % max(TM_INPROJ, TM_MERGE, TM_MLP) == 0 and d % LANES == 0
    cos_t, sin_t = _rope_tables(seq)
    seg = jnp.arange(LANES) // HEAD_DIM
    bd = (seg[:, None] == seg[None, :]).astype(bf16)
    reps = LANES // HEAD_DIM
    qscale = HEAD_DIM ** -0.5 * LOG2E
    for l in range(depth):
        lam_init = 0.8 - 0.6 * math.exp(-0.3 * l)
        gains = jnp.stack([jnp.tile(qn_a[l], reps) * qscale, jnp.tile(kn_a[l], reps),
                           jnp.tile(qn_b[l], reps) * qscale, jnp.tile(kn_b[l], reps)])
        q_all, k_all, vt_all, kmean, w_up_bf, w_down_bf = _inproj(
            x, norm1_g[l][None], w_in[l], gains, cos_t, sin_t, bd, (w_up[l], w_down[l]), tm=TM_INPROJ)
        kmean = kmean.reshape(bsz, seq // BLK, B_WIDTH)
        lamvec = jnp.stack([lam_q1[l], lam_k1[l], lam_q2[l], lam_k2[l]])
        gmax = jnp.max(jnp.abs(gains), axis=1) * (HEAD_DIM ** 0.5)
        subg_cols = jnp.tile(subln_g[l][:, None], (1, BLK))
        oa = _attention(q_all, k_all, vt_all, (lamvec, subg_cols),
                        gmax[0] * gmax[1] * BOUND_SLACK, moba=False, lam_init=lam_init)
        ob = _attention(q_all, k_all, vt_all, (kmean,),
                        gmax[2] * gmax[3] * BOUND_SLACK, moba=True, lam_init=lam_init)
        x1, h2 = _merge(oa, ob, x, norm1_g[l][None], w_in[l], b_gate[l][None],
                        w_a_proj[l], w_b_proj[l], w_out[l], norm2_g[l][None], tm=TM_MERGE)
        x = _mlp(h2.reshape(bsz * seq, d), x1.reshape(bsz * seq, d), w_up_bf, conv_w[l],
                 conv_b[l][None], w_down_bf, seq=seq, tm=TM_MLP).reshape(bsz, seq, d)
    return x
```

```python
import functools
import math

import jax
import jax.numpy as jnp
from jax import lax
from jax.experimental import pallas as pl
from jax.experimental.pallas import tpu as pltpu

HEAD_DIM = 64
A_HEADS = 4
A_VDIM = 2 * HEAD_DIM
A_WIDTH = A_HEADS * A_VDIM
B_HEADS = 8
B_WIDTH = B_HEADS * HEAD_DIM
B_BLOCK = 256
B_TOPK = 3
CONV_W = 3
ROPE_THETA = 10000.0
EPS = 1e-6

LANES = 128
BLK = B_BLOCK
PAIRS = A_WIDTH // LANES
LOG2E = 1.4426950408889634
NEG_BIG = -1e30
POS_BIG = 1e30
VMEM_LIMIT = 56 * 1024 * 1024
BF16_SUBLANES = 16
TM_INPROJ = 1024
TM_MERGE = 512
TM_MLP = 512

f32 = jnp.float32
bf16 = jnp.bfloat16


def _nt_dot(a, b):
    return lax.dot_general(a, b, (((1,), (1,)), ((), ())), preferred_element_type=f32)


def _inproj_kernel(x_ref, g1_ref, w_ref, gain_ref, cos_ref, sin_ref, bd_ref, *rest, tm, n_late):
    late_f32 = rest[:n_late]
    q_ref, k_ref, vt_ref, kmean_ref = rest[n_late:n_late + 4]
    late_bf16 = rest[n_late + 4:2 * n_late + 4]
    h_scr = rest[-1]
    for src, dst in zip(late_f32, late_bf16):
        dst[...] = src[...].astype(bf16)
    nblk = tm // BLK
    x = x_ref[0]
    ms = jnp.mean(x * x, axis=-1, keepdims=True)
    h_scr[...] = (x * lax.rsqrt(ms + EPS) * g1_ref[...]).astype(bf16)
    cos = cos_ref[...]
    sin = sin_ref[...]
    bd = bd_ref[...]
    lane = lax.broadcasted_iota(jnp.int32, (tm, LANES), 1)
    first_half = (lane & (HEAD_DIM - 1)) < (HEAD_DIM // 2)
    low_half = lane < HEAD_DIM

    def col_tile(j):
        return jnp.dot(h_scr[...], w_ref[:, j * A_WIDTH:(j + 1) * A_WIDTH].astype(bf16),
                       preferred_element_type=f32)

    def norm_rope(y, gain):
        ss = jnp.dot((y * y).astype(bf16), bd, preferred_element_type=f32)
        yn = y * lax.rsqrt(ss * (1.0 / HEAD_DIM) + EPS) * gain
        rot = jnp.where(first_half, pltpu.roll(yn, LANES - HEAD_DIM // 2, 1),
                        pltpu.roll(yn, HEAD_DIM // 2, 1))
        return yn * cos + rot * sin

    for branch in range(2):
        acc_q = col_tile(3 * branch)
        for c in range(PAIRS):
            out = norm_rope(acc_q[:, c * LANES:(c + 1) * LANES], gain_ref[2 * branch:2 * branch + 1, :])
            lo = jnp.where(low_half, out, 0.0).astype(bf16)
            hi = jnp.where(low_half, 0.0, out).astype(bf16)
            for bi in range(nblk):
                q_ref[0, branch * PAIRS + c, bi, :BLK] = lo[bi * BLK:(bi + 1) * BLK]
                q_ref[0, branch * PAIRS + c, bi, BLK:] = hi[bi * BLK:(bi + 1) * BLK]
        acc_k = col_tile(3 * branch + 1)
        for c in range(PAIRS):
            out = norm_rope(acc_k[:, c * LANES:(c + 1) * LANES],
                            gain_ref[2 * branch + 1:2 * branch + 2, :])
            for bi in range(nblk):
                blk = out[bi * BLK:(bi + 1) * BLK]
                k_ref[0, branch * PAIRS + c, bi] = blk.astype(bf16)
                if branch == 1:
                    kmean_ref[0, bi:bi + 1, c * LANES:(c + 1) * LANES] = jnp.mean(
                        blk, axis=0, keepdims=True)
        acc_v = col_tile(3 * branch + 2)
        for c in range(PAIRS):
            for bi in range(nblk):
                blk = acc_v[bi * BLK:(bi + 1) * BLK, c * LANES:(c + 1) * LANES]
                vt_ref[0, branch * PAIRS + c, bi] = blk.T.astype(bf16)


def _inproj(x, g1, w_in, gains, cos_t, sin_t, bd, late_weights, *, tm):
    bsz, seq, d = x.shape
    steps = bsz * (seq // tm)
    slab = lambda b, i: (b * (seq // tm) + i, 0)
    late_specs = [pl.BlockSpec((w.shape[0] // steps, w.shape[1]), slab) for w in late_weights]
    assert all(w.shape[0] % (steps * BF16_SUBLANES) == 0 for w in late_weights)
    n_cols = 3 * A_WIDTH + 3 * B_WIDTH
    nblk = tm // BLK
    spt = seq // tm
    nb = seq // BLK
    q_shape = jax.ShapeDtypeStruct((bsz, 2 * PAIRS, nb, 2 * BLK, LANES), bf16)
    qk_shape = jax.ShapeDtypeStruct((bsz, 2 * PAIRS, nb, BLK, LANES), bf16)
    vt_shape = jax.ShapeDtypeStruct((bsz, 2 * PAIRS, nb, LANES, BLK), bf16)
    q_spec = pl.BlockSpec((1, 2 * PAIRS, nblk, 2 * BLK, LANES), lambda b, i: (b, 0, i, 0, 0))
    qk_spec = pl.BlockSpec((1, 2 * PAIRS, nblk, BLK, LANES), lambda b, i: (b, 0, i, 0, 0))
    vt_spec = pl.BlockSpec((1, 2 * PAIRS, nblk, LANES, BLK), lambda b, i: (b, 0, i, 0, 0))
    return pl.pallas_call(
        functools.partial(_inproj_kernel, tm=tm, n_late=len(late_weights)),
        grid=(bsz, spt),
        in_specs=[
            pl.BlockSpec((1, tm, d), lambda b, i: (b, i, 0)),
            pl.BlockSpec((1, d), lambda b, i: (0, 0)),
            pl.BlockSpec((d, n_cols), lambda b, i: (0, 0), pipeline_mode=pl.Buffered(1)),
            pl.BlockSpec((4, LANES), lambda b, i: (0, 0)),
            pl.BlockSpec((tm, LANES), lambda b, i: (i, 0)),
            pl.BlockSpec((tm, LANES), lambda b, i: (i, 0)),
            pl.BlockSpec((LANES, LANES), lambda b, i: (0, 0)),
        ] + late_specs,
        out_specs=[
            q_spec, qk_spec, vt_spec,
            pl.BlockSpec((1, nblk, B_WIDTH), lambda b, i: (b * spt + i, 0, 0)),
        ] + late_specs,
        out_shape=[
            q_shape, qk_shape, vt_shape,
            jax.ShapeDtypeStruct((bsz * spt, nblk, B_WIDTH), f32),
        ] + [jax.ShapeDtypeStruct(w.shape, bf16) for w in late_weights],
        scratch_shapes=[pltpu.VMEM((tm, d), bf16)],
        compiler_params=pltpu.CompilerParams(
            dimension_semantics=("arbitrary", "arbitrary"), vmem_limit_bytes=VMEM_LIMIT),
        name="inproj",
    )(x, g1, w_in, gains, cos_t, sin_t, bd, *late_weights)


TRIP_STEPS = 34
QK_AHEAD = 6
TILE_UNROLL = 16
BOUND_SLACK = 1.0 + 2.0 ** -6
MAX_FIXED_BOUND = 60.0


def _attn_kernel(itab_ref, jtab_ref, otab_ref, *refs, moba, online, lam_init, n_steps, n_masked):
    if moba:
        (q_ref, k_ref, vt_ref, mref_ref, kmean_ref, o_ref,
         m_scr, l_scr, acc_scr, s_scr, sel_scr) = refs
    else:
        (q_ref, k_ref, vt_ref, mref_ref, lam_ref, subg_ref, o_ref,
         m_scr, l_scr, acc_scr, s_scr) = refs
    nb = k_ref.shape[2]
    acc_rows = acc_scr.shape[1]
    key_minus_row = (lax.broadcasted_iota(jnp.int32, (BLK, 2 * BLK), 0)
                     - (lax.broadcasted_iota(jnp.int32, (BLK, 2 * BLK), 1) & (BLK - 1)))

    if moba:
        km = kmean_ref[0]
        km_parts = []
        for _ in range(3):
            part = km.astype(bf16)
            km_parts.append(part)
            km = km - part.astype(f32)

    nidx = lax.broadcasted_iota(jnp.int32, (nb, 2 * BLK), 0)
    for i in range(nb):
        if online:
            m_scr[i] = jnp.full((1, 2 * BLK), NEG_BIG, f32)
        l_scr[i] = jnp.zeros((1, 2 * BLK), f32)
        acc_scr[i] = jnp.zeros((acc_rows, 2 * BLK), f32)
        if moba:
            keep = nidx <= i
            if i > B_TOPK:
                qs = q_ref[0, 0, i]
                g = _nt_dot(km_parts[0], qs) + _nt_dot(km_parts[1], qs) + _nt_dot(km_parts[2], qs)
                cnt = jnp.zeros((nb, 2 * BLK), f32)
                for mm in range(i):
                    row = g[mm:mm + 1, :]
                    beats = (row > g) | ((row == g) & (nidx > mm))
                    cnt = cnt + jnp.where(beats, 1.0, 0.0)
                keep = ((cnt < float(B_TOPK)) & (nidx < i)) | (nidx == i)
            sel_scr[i] = jnp.where(keep, 1.0, 0.0)

    ring = s_scr.shape[0]

    def qk(step, u):
        s = _nt_dot(k_ref[0, 0, jtab_ref[step]], q_ref[0, 0, itab_ref[step]])
        if u < n_masked:
            s = jnp.where(key_minus_row <= otab_ref[step], s, NEG_BIG)
        s_scr[u % ring] = s

    def update(step, u):
        i = itab_ref[step]
        j = jtab_ref[step]
        s = s_scr[u % ring]
        if moba:
            chosen = sel_scr[i, pl.ds(j, 1), :] > 0.5
        if online:
            m_old = m_scr[i]
            m_new = jnp.maximum(m_old, jnp.max(s, axis=0, keepdims=True))
            if moba:
                m_new = jnp.where(chosen, m_new, m_old)
            m_sub = m_new
            alpha = jnp.exp2(m_old - m_new)
            m_scr[i] = m_new
        else:
            m_sub = mref_ref[...]
        if moba:
            m_sub = jnp.where(chosen, m_sub, POS_BIG)
        p = jnp.exp2(s - m_sub)
        psum = jnp.sum(p, axis=0, keepdims=True)
        pb = p.astype(bf16)
        vt = vt_ref[0, 0, j]
        pv = jnp.dot(vt, pb, preferred_element_type=f32)
        if online:
            l_scr[i] = alpha * l_scr[i] + psum
            acc_scr[i] = alpha * acc_scr[i] + pv
        else:
            l_scr[i] = l_scr[i] + psum
            acc_scr[i] = acc_scr[i] + pv

    ahead = ring - 1

    def trip(it, carry):
        base = it * TRIP_STEPS
        for u in range(ahead):
            qk(base + u, u)
        for u in range(TRIP_STEPS):
            if u + ahead < TRIP_STEPS:
                qk(base + u + ahead, u + ahead)
            update(base + u, u)
        return carry

    lax.fori_loop(0, n_steps // TRIP_STEPS, trip, 0)

    if not moba:
        lv = lam_ref[...]
        lam = (jnp.exp(jnp.sum(lv[0:1] * lv[1:2], axis=-1, keepdims=True))
               - jnp.exp(jnp.sum(lv[2:3] * lv[3:4], axis=-1, keepdims=True)) + lam_init)

    def finish_tiles(t, carry):
        for r in range(TILE_UNROLL):
            i = t * TILE_UNROLL + r
            o = acc_scr[i] * (1.0 / l_scr[i])
            rows = pl.ds(pl.multiple_of(i * BLK, BLK), BLK)
            if moba:
                ot = jnp.concatenate([o[:HEAD_DIM, :BLK], o[HEAD_DIM:, BLK:]], axis=0).T
                o_ref[0, rows, :] = ot.astype(bf16)
            else:
                od = o[:, :BLK] - lam * o[:, BLK:]
                ms = jnp.mean(od * od, axis=0, keepdims=True)
                od = od * lax.rsqrt(ms + EPS) * subg_ref[...] * (1.0 - lam_init)
                o_ref[0, rows, :] = od.T.astype(bf16)
        return carry

    lax.fori_loop(0, nb // TILE_UNROLL, finish_tiles, 0)


def _step_tables(nb):
    per_trip = TRIP_STEPS
    diag = [(i, i, 0) for i in range(nb)]
    rest = [(i, j, BLK) for j in range(nb) for i in range(j + 1, nb)]
    n_steps = len(diag) + len(rest)
    assert n_steps % per_trip == 0
    n_trips = n_steps // per_trip
    n_masked = -(-len(diag) // n_trips)
    assert n_masked <= per_trip
    steps = []
    for _ in range(n_trips):
        for _ in range(n_masked):
            steps.append(diag.pop(0) if diag else rest.pop(0))
        for _ in range(per_trip - n_masked):
            steps.append(rest.pop(0))
    assert not diag and not rest
    return [jnp.asarray([s[c] for s in steps], jnp.int32) for c in range(3)], n_steps, n_masked


def _attention(q_all, k_all, vt_all, aux, score_bound, *, moba, lam_init):
    bsz, _, nb, _, _ = k_all.shape
    off = PAIRS if moba else 0
    assert nb % TILE_UNROLL == 0
    (itab, jtab, otab), n_steps, n_masked = _step_tables(nb)
    const = lambda b, h, it, jt, ot: (0, 0)
    in_specs = [
        pl.BlockSpec((1, 1, nb, 2 * BLK, LANES), lambda b, h, it, jt, ot: (b, h + off, 0, 0, 0)),
        pl.BlockSpec((1, 1, nb, BLK, LANES), lambda b, h, it, jt, ot: (b, h + off, 0, 0, 0)),
        pl.BlockSpec((1, 1, nb, LANES, BLK), lambda b, h, it, jt, ot: (b, h + off, 0, 0, 0)),
        pl.BlockSpec((1, 2 * BLK), const),
    ]
    scratch = [pltpu.VMEM((nb, 1, 2 * BLK), f32), pltpu.VMEM((nb, 1, 2 * BLK), f32),
               pltpu.VMEM((nb, LANES, 2 * BLK), f32),
               pltpu.VMEM((QK_AHEAD + 1, BLK, 2 * BLK), f32)]
    if moba:
        (kmean,) = aux
        in_specs.append(pl.BlockSpec((1, nb, LANES), lambda b, h, it, jt, ot: (b, 0, h)))
        scratch.append(pltpu.VMEM((nb, nb, 2 * BLK), f32))
    else:
        lamvec, subg = aux
        in_specs.append(pl.BlockSpec(lamvec.shape, const))
        in_specs.append(pl.BlockSpec(subg.shape, const))
    mref = jnp.full((1, 2 * BLK), score_bound, f32)

    def run(online):
        return pl.pallas_call(
            functools.partial(_attn_kernel, moba=moba, online=online, lam_init=lam_init,
                              n_steps=n_steps, n_masked=n_masked),
            grid_spec=pltpu.PrefetchScalarGridSpec(
                num_scalar_prefetch=3,
                grid=(bsz, PAIRS),
                in_specs=in_specs,
                out_specs=pl.BlockSpec((1, nb * BLK, LANES), lambda b, h, it, jt, ot: (b, 0, h)),
                scratch_shapes=scratch),
            out_shape=jax.ShapeDtypeStruct((bsz, nb * BLK, PAIRS * LANES), bf16),
            compiler_params=pltpu.CompilerParams(
                dimension_semantics=("arbitrary", "arbitrary"), vmem_limit_bytes=VMEM_LIMIT),
            name=("moba_attn" if moba else "diff_attn") + ("_online" if online else ""),
        )(itab, jtab, otab, q_all, k_all, vt_all, mref, *aux)

    return lax.cond(score_bound <= MAX_FIXED_BOUND, lambda: run(False), lambda: run(True))


def _merge_kernel(oa_ref, ob_ref, x_ref, g1_ref, wga_ref, wgb_ref, bg_ref, wa_ref, wb_ref, wo_ref,
                  g2_ref, x1_ref, h2_ref):
    d = x_ref.shape[-1]
    x = x_ref[0]
    ms = jnp.mean(x * x, axis=-1, keepdims=True)
    h = (x * lax.rsqrt(ms + EPS) * g1_ref[...]).astype(bf16)
    merged = None
    for br, (wg_ref, o_ref, w_ref) in enumerate(((wga_ref, oa_ref, wa_ref), (wgb_ref, ob_ref, wb_ref))):
        gate = jax.nn.sigmoid(jnp.dot(h, wg_ref[...].astype(bf16), preferred_element_type=f32)
                              + bg_ref[:, br * d:(br + 1) * d])
        term = gate * jnp.dot(o_ref[0], w_ref[...].astype(bf16), preferred_element_type=f32)
        merged = term if merged is None else merged + term
    x1 = x + jnp.dot(merged.astype(bf16), wo_ref[...].astype(bf16), preferred_element_type=f32)
    x1_ref[0] = x1
    ms = jnp.mean(x1 * x1, axis=-1, keepdims=True)
    h2_ref[0] = (x1 * lax.rsqrt(ms + EPS) * g2_ref[...]).astype(bf16)


def _merge(oa, ob, x, g1, w_in, bg, wa, wb, wo, g2, *, tm):
    bsz, seq, d = x.shape
    gate_blk = (w_in.shape[1] - 2 * d) // d
    assert gate_blk * d + 2 * d == w_in.shape[1]
    const = lambda b, i: (0, 0)
    once = pl.Buffered(1)
    tile = lambda b, i: (b, i, 0)
    return pl.pallas_call(
        _merge_kernel,
        grid=(bsz, seq // tm),
        in_specs=[
            pl.BlockSpec((1, tm, oa.shape[-1]), tile),
            pl.BlockSpec((1, tm, ob.shape[-1]), tile),
            pl.BlockSpec((1, tm, d), tile),
            pl.BlockSpec((1, d), const),
            pl.BlockSpec((d, d), lambda b, i: (0, gate_blk), pipeline_mode=once),
            pl.BlockSpec((d, d), lambda b, i: (0, gate_blk + 1), pipeline_mode=once),
            pl.BlockSpec(bg.shape, const),
            pl.BlockSpec(wa.shape, const, pipeline_mode=once),
            pl.BlockSpec(wb.shape, const, pipeline_mode=once),
            pl.BlockSpec(wo.shape, const, pipeline_mode=once),
            pl.BlockSpec((1, d), const),
        ],
        out_specs=[pl.BlockSpec((1, tm, d), tile), pl.BlockSpec((1, tm, d), tile)],
        out_shape=[jax.ShapeDtypeStruct((bsz, seq, d), f32),
                   jax.ShapeDtypeStruct((bsz, seq, d), bf16)],
        compiler_params=pltpu.CompilerParams(
            dimension_semantics=("arbitrary", "arbitrary"), vmem_limit_bytes=VMEM_LIMIT),
        name="merge_outproj",
    )(oa, ob, x, g1, w_in, w_in, bg, wa, wb, wo, g2)


HALO = 8
assert HALO >= CONV_W - 1


def _mlp_kernel(h2_ref, x1_ref, wup_ref, cw_ref, cb_ref, wd_ref, o_ref, u_scr, *, tm, tiles_per_seq):
    i = pl.program_id(0)
    dff = wd_ref.shape[0]

    @pl.when(i % tiles_per_seq == 0)
    def _():
        u_scr[:HALO, :] = jnp.zeros((HALO, dff), f32)

    ug = jnp.dot(h2_ref[...], wup_ref[...], preferred_element_type=f32)
    u, g = ug[:, :dff], ug[:, dff:]
    u_scr[HALO:, :] = u
    y = cb_ref[...]
    for j in range(CONV_W):
        y = y + u_scr[pl.ds(HALO - (CONV_W - 1) + j, tm), :] * cw_ref[j:j + 1, :]
    u_scr[:HALO, :] = u[tm - HALO:, :]
    act = (jax.nn.gelu(y) * g).astype(bf16)
    o_ref[...] = x1_ref[...] + jnp.dot(act, wd_ref[...], preferred_element_type=f32)


def _mlp(h2, x1, w_up_bf, conv_w, conv_b, w_down_bf, *, seq, tm):
    n, d = h2.shape
    dff = w_down_bf.shape[0]
    const = lambda i: (0, 0)
    return pl.pallas_call(
        functools.partial(_mlp_kernel, tm=tm, tiles_per_seq=seq // tm),
        grid=(n // tm,),
        in_specs=[
            pl.BlockSpec((tm, d), lambda i: (i, 0)),
            pl.BlockSpec((tm, d), lambda i: (i, 0)),
            pl.BlockSpec(w_up_bf.shape, const, pipeline_mode=pl.Buffered(1)),
            pl.BlockSpec(conv_w.shape, const),
            pl.BlockSpec(conv_b.shape, const),
            pl.BlockSpec(w_down_bf.shape, const, pipeline_mode=pl.Buffered(1)),
        ],
        out_specs=pl.BlockSpec((tm, d), lambda i: (i, 0)),
        out_shape=jax.ShapeDtypeStruct((n, d), f32),
        scratch_shapes=[pltpu.VMEM((tm + HALO, dff), f32)],
        compiler_params=pltpu.CompilerParams(
            dimension_semantics=("arbitrary",), vmem_limit_bytes=VMEM_LIMIT),
        name="conv_mlp",
    )(h2, x1, w_up_bf, conv_w, conv_b, w_down_bf)


def _rope_tables(seq):
    inv = 1.0 / (ROPE_THETA ** (jnp.arange(0, HEAD_DIM, 2, dtype=f32) / HEAD_DIM))
    ang = jnp.arange(seq, dtype=f32)[:, None] * inv[None, :]
    cos, sin = jnp.cos(ang), jnp.sin(ang)
    reps = LANES // HEAD_DIM
    return (jnp.tile(cos, (1, 2 * reps)), jnp.tile(jnp.concatenate([-sin, sin], axis=1), (1, reps)))


def kernel(x, norm1_g, w_in, b_gate, qn_a, kn_a, lam_q1, lam_k1, lam_q2, lam_k2, subln_g, qn_b, kn_b,
           w_a_proj, w_b_proj, w_out, norm2_g, w_up, conv_w, conv_b, w_down):
    bsz, seq, d = x.shape
    depth = w_in.shape[0]
    assert seq % max(TM_INPROJ, TM_MERGE, TM_MLP) == 0 and d % LANES == 0
    cos_t, sin_t = _rope_tables(seq)
    seg = jnp.arange(LANES) // HEAD_DIM
    bd = (seg[:, None] == seg[None, :]).astype(bf16)
    reps = LANES // HEAD_DIM
    qscale = HEAD_DIM ** -0.5 * LOG2E
    for l in range(depth):
        lam_init = 0.8 - 0.6 * math.exp(-0.3 * l)
        gains = jnp.stack([jnp.tile(qn_a[l], reps) * qscale, jnp.tile(kn_a[l], reps),
                           jnp.tile(qn_b[l], reps) * qscale, jnp.tile(kn_b[l], reps)])
        q_all, k_all, vt_all, kmean, w_up_bf, w_down_bf = _inproj(
            x, norm1_g[l][None], w_in[l], gains, cos_t, sin_t, bd, (w_up[l], w_down[l]), tm=TM_INPROJ)
        kmean = kmean.reshape(bsz, seq // BLK, B_WIDTH)
        lamvec = jnp.stack([lam_q1[l], lam_k1[l], lam_q2[l], lam_k2[l]])
        gmax = jnp.max(jnp.abs(gains), axis=1) * (HEAD_DIM ** 0.5)
        subg_cols = jnp.tile(subln_g[l][:, None], (1, BLK))
        oa = _attention(q_all, k_all, vt_all, (lamvec, subg_cols),
                        gmax[0] * gmax[1] * BOUND_SLACK, moba=False, lam_init=lam_init)
        ob = _attention(q_all, k_all, vt_all, (kmean,),
                        gmax[2] * gmax[3] * BOUND_SLACK, moba=True, lam_init=lam_init)
        x1, h2 = _merge(oa, ob, x, norm1_g[l][None], w_in[l], b_gate[l][None],
                        w_a_proj[l], w_b_proj[l], w_out[l], norm2_g[l][None], tm=TM_MERGE)
        x = _mlp(h2.reshape(bsz * seq, d), x1.reshape(bsz * seq, d), w_up_bf, conv_w[l],
                 conv_b[l][None], w_down_bf, seq=seq, tm=TM_MLP).reshape(bsz, seq, d)
    return x
```

```python
import functools
import math

import jax
import jax.numpy as jnp
from jax import lax
from jax.experimental import pallas as pl
from jax.experimental.pallas import tpu as pltpu

HEAD_DIM = 64
A_HEADS = 4
A_VDIM = 2 * HEAD_DIM
A_WIDTH = A_HEADS * A_VDIM
B_HEADS = 8
B_WIDTH = B_HEADS * HEAD_DIM
B_BLOCK = 256
B_TOPK = 3
CONV_W = 3
ROPE_THETA = 10000.0
EPS = 1e-6

LANES = 128
BLK = B_BLOCK
PAIRS = A_WIDTH // LANES
LOG2E = 1.4426950408889634
NEG_BIG = -1e30
POS_BIG = 1e30
VMEM_LIMIT = 56 * 1024 * 1024
BF16_SUBLANES = 16
TM_INPROJ = 1024
TM_MERGE = 1024
TM_MLP = 512

f32 = jnp.float32
bf16 = jnp.bfloat16


def _nt_dot(a, b):
    return lax.dot_general(a, b, (((1,), (1,)), ((), ())), preferred_element_type=f32)


def _inproj_kernel(x_ref, g1_ref, w_ref, gain_ref, cos_ref, sin_ref, bd_ref, *rest, tm, n_late):
    late_f32 = rest[:n_late]
    q_ref, k_ref, vt_ref, kmean_ref = rest[n_late:n_late + 4]
    late_bf16 = rest[n_late + 4:2 * n_late + 4]
    h_scr = rest[-1]
    for src, dst in zip(late_f32, late_bf16):
        dst[...] = src[...].astype(bf16)
    nblk = tm // BLK
    x = x_ref[0]
    ms = jnp.mean(x * x, axis=-1, keepdims=True)
    h_scr[...] = (x * lax.rsqrt(ms + EPS) * g1_ref[...]).astype(bf16)
    cos = cos_ref[...]
    sin = sin_ref[...]
    bd = bd_ref[...]
    lane = lax.broadcasted_iota(jnp.int32, (tm, LANES), 1)
    first_half = (lane & (HEAD_DIM - 1)) < (HEAD_DIM // 2)
    low_half = lane < HEAD_DIM

    def col_tile(j):
        return jnp.dot(h_scr[...], w_ref[:, j * A_WIDTH:(j + 1) * A_WIDTH].astype(bf16),
                       preferred_element_type=f32)

    def norm_rope(y, gain):
        ss = jnp.dot((y * y).astype(bf16), bd, preferred_element_type=f32)
        yn = y * lax.rsqrt(ss * (1.0 / HEAD_DIM) + EPS) * gain
        rot = jnp.where(first_half, pltpu.roll(yn, LANES - HEAD_DIM // 2, 1),
                        pltpu.roll(yn, HEAD_DIM // 2, 1))
        return yn * cos + rot * sin

    for branch in range(2):
        acc_q = col_tile(3 * branch)
        for c in range(PAIRS):
            out = norm_rope(acc_q[:, c * LANES:(c + 1) * LANES], gain_ref[2 * branch:2 * branch + 1, :])
            lo = jnp.where(low_half, out, 0.0).astype(bf16)
            hi = jnp.where(low_half, 0.0, out).astype(bf16)
            for bi in range(nblk):
                q_ref[0, branch * PAIRS + c, bi, :BLK] = lo[bi * BLK:(bi + 1) * BLK]
                q_ref[0, branch * PAIRS + c, bi, BLK:] = hi[bi * BLK:(bi + 1) * BLK]
        acc_k = col_tile(3 * branch + 1)
        for c in range(PAIRS):
            out = norm_rope(acc_k[:, c * LANES:(c + 1) * LANES],
                            gain_ref[2 * branch + 1:2 * branch + 2, :])
            for bi in range(nblk):
                blk = out[bi * BLK:(bi + 1) * BLK]
                k_ref[0, branch * PAIRS + c, bi] = blk.astype(bf16)
                if branch == 1:
                    kmean_ref[0, bi:bi + 1, c * LANES:(c + 1) * LANES] = jnp.mean(
                        blk, axis=0, keepdims=True)
        acc_v = col_tile(3 * branch + 2)
        for c in range(PAIRS):
            for bi in range(nblk):
                blk = acc_v[bi * BLK:(bi + 1) * BLK, c * LANES:(c + 1) * LANES]
                vt_ref[0, branch * PAIRS + c, bi] = blk.T.astype(bf16)


def _inproj(x, g1, w_in, gains, cos_t, sin_t, bd, late_weights, *, tm):
    bsz, seq, d = x.shape
    steps = bsz * (seq // tm)
    slab = lambda b, i: (b * (seq // tm) + i, 0)
    late_specs = [pl.BlockSpec((w.shape[0] // steps, w.shape[1]), slab) for w in late_weights]
    assert all(w.shape[0] % (steps * BF16_SUBLANES) == 0 for w in late_weights)
    n_cols = 3 * A_WIDTH + 3 * B_WIDTH
    nblk = tm // BLK
    spt = seq // tm
    nb = seq // BLK
    q_shape = jax.ShapeDtypeStruct((bsz, 2 * PAIRS, nb, 2 * BLK, LANES), bf16)
    qk_shape = jax.ShapeDtypeStruct((bsz, 2 * PAIRS, nb, BLK, LANES), bf16)
    vt_shape = jax.ShapeDtypeStruct((bsz, 2 * PAIRS, nb, LANES, BLK), bf16)
    q_spec = pl.BlockSpec((1, 2 * PAIRS, nblk, 2 * BLK, LANES), lambda b, i: (b, 0, i, 0, 0))
    qk_spec = pl.BlockSpec((1, 2 * PAIRS, nblk, BLK, LANES), lambda b, i: (b, 0, i, 0, 0))
    vt_spec = pl.BlockSpec((1, 2 * PAIRS, nblk, LANES, BLK), lambda b, i: (b, 0, i, 0, 0))
    return pl.pallas_call(
        functools.partial(_inproj_kernel, tm=tm, n_late=len(late_weights)),
        grid=(bsz, spt),
        in_specs=[
            pl.BlockSpec((1, tm, d), lambda b, i: (b, i, 0)),
            pl.BlockSpec((1, d), lambda b, i: (0, 0)),
            pl.BlockSpec((d, n_cols), lambda b, i: (0, 0), pipeline_mode=pl.Buffered(1)),
            pl.BlockSpec((4, LANES), lambda b, i: (0, 0)),
            pl.BlockSpec((tm, LANES), lambda b, i: (i, 0)),
            pl.BlockSpec((tm, LANES), lambda b, i: (i, 0)),
            pl.BlockSpec((LANES, LANES), lambda b, i: (0, 0)),
        ] + late_specs,
        out_specs=[
            q_spec, qk_spec, vt_spec,
            pl.BlockSpec((1, nblk, B_WIDTH), lambda b, i: (b * spt + i, 0, 0)),
        ] + late_specs,
        out_shape=[
            q_shape, qk_shape, vt_shape,
            jax.ShapeDtypeStruct((bsz * spt, nblk, B_WIDTH), f32),
        ] + [jax.ShapeDtypeStruct(w.shape, bf16) for w in late_weights],
        scratch_shapes=[pltpu.VMEM((tm, d), bf16)],
        compiler_params=pltpu.CompilerParams(
            dimension_semantics=("arbitrary", "arbitrary"), vmem_limit_bytes=VMEM_LIMIT),
        name="inproj",
    )(x, g1, w_in, gains, cos_t, sin_t, bd, *late_weights)


TRIP_STEPS = 34
QK_AHEAD = 6
TILE_UNROLL = 4
BOUND_SLACK = 1.0 + 2.0 ** -6
MAX_FIXED_BOUND = 60.0


def _attn_kernel(itab_ref, jtab_ref, otab_ref, *refs, moba, online, lam_init, n_steps, n_masked):
    if moba:
        (q_ref, k_ref, vt_ref, mref_ref, kmean_ref, o_ref,
         m_scr, l_scr, acc_scr, s_scr, sel_scr) = refs
    else:
        (q_ref, k_ref, vt_ref, mref_ref, lam_ref, subg_ref, o_ref,
         m_scr, l_scr, acc_scr, s_scr) = refs
    nb = k_ref.shape[2]
    acc_rows = acc_scr.shape[1]
    key_minus_row = (lax.broadcasted_iota(jnp.int32, (BLK, 2 * BLK), 0)
                     - (lax.broadcasted_iota(jnp.int32, (BLK, 2 * BLK), 1) & (BLK - 1)))

    if moba:
        km = kmean_ref[0]
        km_parts = []
        for _ in range(3):
            part = km.astype(bf16)
            km_parts.append(part)
            km = km - part.astype(f32)

    nidx = lax.broadcasted_iota(jnp.int32, (nb, 2 * BLK), 0)
    for i in range(nb):
        if online:
            m_scr[i] = jnp.full((1, 2 * BLK), NEG_BIG, f32)
        l_scr[i] = jnp.zeros((1, 2 * BLK), f32)
        acc_scr[i] = jnp.zeros((acc_rows, 2 * BLK), f32)
        if moba:
            keep = nidx <= i
            if i > B_TOPK:
                qs = q_ref[0, 0, i]
                g = _nt_dot(km_parts[0], qs) + _nt_dot(km_parts[1], qs) + _nt_dot(km_parts[2], qs)
                cnt = jnp.zeros((nb, 2 * BLK), f32)
                for mm in range(i):
                    row = g[mm:mm + 1, :]
                    beats = (row > g) | ((row == g) & (nidx > mm))
                    cnt = cnt + jnp.where(beats, 1.0, 0.0)
                keep = ((cnt < float(B_TOPK)) & (nidx < i)) | (nidx == i)
            sel_scr[i] = jnp.where(keep, 1.0, 0.0)

    ring = s_scr.shape[0]

    def qk(step, u):
        s = _nt_dot(k_ref[0, 0, jtab_ref[step]], q_ref[0, 0, itab_ref[step]])
        if u < n_masked:
            s = jnp.where(key_minus_row <= otab_ref[step], s, NEG_BIG)
        s_scr[u % ring] = s

    def update(step, u):
        i = itab_ref[step]
        j = jtab_ref[step]
        s = s_scr[u % ring]
        if moba:
            chosen = sel_scr[i, pl.ds(j, 1), :] > 0.5
        if online:
            m_old = m_scr[i]
            m_new = jnp.maximum(m_old, jnp.max(s, axis=0, keepdims=True))
            if moba:
                m_new = jnp.where(chosen, m_new, m_old)
            m_sub = m_new
            alpha = jnp.exp2(m_old - m_new)
            m_scr[i] = m_new
        else:
            m_sub = mref_ref[...]
        if moba:
            m_sub = jnp.where(chosen, m_sub, POS_BIG)
        p = jnp.exp2(s - m_sub)
        psum = jnp.sum(p, axis=0, keepdims=True)
        pb = p.astype(bf16)
        vt = vt_ref[0, 0, j]
        pv = jnp.dot(vt, pb, preferred_element_type=f32)
        if online:
            l_scr[i] = alpha * l_scr[i] + psum
            acc_scr[i] = alpha * acc_scr[i] + pv
        else:
            l_scr[i] = l_scr[i] + psum
            acc_scr[i] = acc_scr[i] + pv

    ahead = ring - 1

    def trip(it, carry):
        base = it * TRIP_STEPS
        for u in range(ahead):
            qk(base + u, u)
        for u in range(TRIP_STEPS):
            if u + ahead < TRIP_STEPS:
                qk(base + u + ahead, u + ahead)
            update(base + u, u)
        return carry

    lax.fori_loop(0, n_steps // TRIP_STEPS, trip, 0)

    if not moba:
        lv = lam_ref[...]
        lam = (jnp.exp(jnp.sum(lv[0:1] * lv[1:2], axis=-1, keepdims=True))
               - jnp.exp(jnp.sum(lv[2:3] * lv[3:4], axis=-1, keepdims=True)) + lam_init)

    def finish_tiles(t, carry):
        for r in range(TILE_UNROLL):
            i = t * TILE_UNROLL + r
            o = acc_scr[i] * (1.0 / l_scr[i])
            if moba:
                od = jnp.concatenate([o[:HEAD_DIM, :BLK], o[HEAD_DIM:, BLK:]], axis=0)
            else:
                od = o[:, :BLK] - lam * o[:, BLK:]
                ms = jnp.mean(od * od, axis=0, keepdims=True)
                od = od * lax.rsqrt(ms + EPS) * subg_ref[...] * (1.0 - lam_init)
            o_ref[0, 0, i] = od.astype(bf16)
        return carry

    lax.fori_loop(0, nb // TILE_UNROLL, finish_tiles, 0)


def _step_tables(nb):
    per_trip = TRIP_STEPS
    diag = [(i, i, 0) for i in range(nb)]
    rest = [(i, j, BLK) for j in range(nb) for i in range(j + 1, nb)]
    n_steps = len(diag) + len(rest)
    assert n_steps % per_trip == 0
    n_trips = n_steps // per_trip
    n_masked = -(-len(diag) // n_trips)
    assert n_masked <= per_trip
    steps = []
    for _ in range(n_trips):
        for _ in range(n_masked):
            steps.append(diag.pop(0) if diag else rest.pop(0))
        for _ in range(per_trip - n_masked):
            steps.append(rest.pop(0))
    assert not diag and not rest
    return [jnp.asarray([s[c] for s in steps], jnp.int32) for c in range(3)], n_steps, n_masked


def _attention(q_all, k_all, vt_all, aux, score_bound, *, moba, lam_init):
    bsz, _, nb, _, _ = k_all.shape
    off = PAIRS if moba else 0
    assert nb % TILE_UNROLL == 0
    (itab, jtab, otab), n_steps, n_masked = _step_tables(nb)
    const = lambda b, h, it, jt, ot: (0, 0)
    in_specs = [
        pl.BlockSpec((1, 1, nb, 2 * BLK, LANES), lambda b, h, it, jt, ot: (b, h + off, 0, 0, 0)),
        pl.BlockSpec((1, 1, nb, BLK, LANES), lambda b, h, it, jt, ot: (b, h + off, 0, 0, 0)),
        pl.BlockSpec((1, 1, nb, LANES, BLK), lambda b, h, it, jt, ot: (b, h + off, 0, 0, 0)),
        pl.BlockSpec((1, 2 * BLK), const),
    ]
    scratch = [pltpu.VMEM((nb, 1, 2 * BLK), f32), pltpu.VMEM((nb, 1, 2 * BLK), f32),
               pltpu.VMEM((nb, LANES, 2 * BLK), f32),
               pltpu.VMEM((QK_AHEAD + 1, BLK, 2 * BLK), f32)]
    if moba:
        (kmean,) = aux
        in_specs.append(pl.BlockSpec((1, nb, LANES), lambda b, h, it, jt, ot: (b, 0, h)))
        scratch.append(pltpu.VMEM((nb, nb, 2 * BLK), f32))
    else:
        lamvec, subg = aux
        in_specs.append(pl.BlockSpec(lamvec.shape, const))
        in_specs.append(pl.BlockSpec(subg.shape, const))
    mref = jnp.full((1, 2 * BLK), score_bound, f32)

    def run(online):
        return pl.pallas_call(
            functools.partial(_attn_kernel, moba=moba, online=online, lam_init=lam_init,
                              n_steps=n_steps, n_masked=n_masked),
            grid_spec=pltpu.PrefetchScalarGridSpec(
                num_scalar_prefetch=3,
                grid=(bsz, PAIRS),
                in_specs=in_specs,
                out_specs=pl.BlockSpec((1, 1, nb, LANES, BLK),
                                       lambda b, h, it, jt, ot: (b, h, 0, 0, 0)),
                scratch_shapes=scratch),
            out_shape=jax.ShapeDtypeStruct((bsz, PAIRS, nb, LANES, BLK), bf16),
            compiler_params=pltpu.CompilerParams(
                dimension_semantics=("arbitrary", "arbitrary"), vmem_limit_bytes=VMEM_LIMIT),
            name=("moba_attn" if moba else "diff_attn") + ("_online" if online else ""),
        )(itab, jtab, otab, q_all, k_all, vt_all, mref, *aux)

    return lax.cond(score_bound <= MAX_FIXED_BOUND, lambda: run(False), lambda: run(True))


def _merge_kernel(oa_ref, ob_ref, x_ref, g1_ref, wga_ref, wgb_ref, bg_ref, wa_ref, wb_ref, wo_ref,
                  g2_ref, x1_ref, h2_ref):
    d = x_ref.shape[-1]
    x = x_ref[0]
    ms = jnp.mean(x * x, axis=-1, keepdims=True)
    h = (x * lax.rsqrt(ms + EPS) * g1_ref[...]).astype(bf16)
    merged = None
    for br, (wg_ref, o_ref, w_ref) in enumerate(((wga_ref, oa_ref, wa_ref), (wgb_ref, ob_ref, wb_ref))):
        gate = jax.nn.sigmoid(jnp.dot(h, wg_ref[...].astype(bf16), preferred_element_type=f32)
                              + bg_ref[:, br * d:(br + 1) * d])
        o_t = jnp.concatenate(
            [jnp.concatenate([o_ref[0, c, bi] for bi in range(o_ref.shape[2])], axis=1)
             for c in range(o_ref.shape[1])], axis=0)
        term = gate * lax.dot_general(o_t, w_ref[...].astype(bf16), (((0,), (0,)), ((), ())),
                                      preferred_element_type=f32)
        merged = term if merged is None else merged + term
    x1 = x + jnp.dot(merged.astype(bf16), wo_ref[...].astype(bf16), preferred_element_type=f32)
    x1_ref[0] = x1
    ms = jnp.mean(x1 * x1, axis=-1, keepdims=True)
    h2_ref[0] = (x1 * lax.rsqrt(ms + EPS) * g2_ref[...]).astype(bf16)


def _merge(oa, ob, x, g1, w_in, bg, wa, wb, wo, g2, *, tm):
    bsz, seq, d = x.shape
    gate_blk = (w_in.shape[1] - 2 * d) // d
    assert gate_blk * d + 2 * d == w_in.shape[1]
    const = lambda b, i: (0, 0)
    once = pl.Buffered(1)
    tile = lambda b, i: (b, i, 0)
    return pl.pallas_call(
        _merge_kernel,
        grid=(bsz, seq // tm),
        in_specs=[
            pl.BlockSpec((1, PAIRS, tm // BLK, LANES, BLK), lambda b, i: (b, 0, i, 0, 0)),
            pl.BlockSpec((1, PAIRS, tm // BLK, LANES, BLK), lambda b, i: (b, 0, i, 0, 0)),
            pl.BlockSpec((1, tm, d), tile),
            pl.BlockSpec((1, d), const),
            pl.BlockSpec((d, d), lambda b, i: (0, gate_blk), pipeline_mode=once),
            pl.BlockSpec((d, d), lambda b, i: (0, gate_blk + 1), pipeline_mode=once),
            pl.BlockSpec(bg.shape, const),
            pl.BlockSpec(wa.shape, const, pipeline_mode=once),
            pl.BlockSpec(wb.shape, const, pipeline_mode=once),
            pl.BlockSpec(wo.shape, const, pipeline_mode=once),
            pl.BlockSpec((1, d), const),
        ],
        out_specs=[pl.BlockSpec((1, tm, d), tile), pl.BlockSpec((1, tm, d), tile)],
        out_shape=[jax.ShapeDtypeStruct((bsz, seq, d), f32),
                   jax.ShapeDtypeStruct((bsz, seq, d), bf16)],
        compiler_params=pltpu.CompilerParams(
            dimension_semantics=("arbitrary", "arbitrary"), vmem_limit_bytes=VMEM_LIMIT),
        name="merge_outproj",
    )(oa, ob, x, g1, w_in, w_in, bg, wa, wb, wo, g2)


HALO = 8
assert HALO >= CONV_W - 1


def _mlp_kernel(h2_ref, x1_ref, wup_ref, cw_ref, cb_ref, wd_ref, o_ref, u_scr, *, tm, tiles_per_seq):
    i = pl.program_id(0)
    dff = wd_ref.shape[0]

    @pl.when(i % tiles_per_seq == 0)
    def _():
        u_scr[:HALO, :] = jnp.zeros((HALO, dff), f32)

    ug = jnp.dot(h2_ref[...], wup_ref[...], preferred_element_type=f32)
    u, g = ug[:, :dff], ug[:, dff:]
    u_scr[HALO:, :] = u
    y = cb_ref[...]
    for j in range(CONV_W):
        y = y + u_scr[pl.ds(HALO - (CONV_W - 1) + j, tm), :] * cw_ref[j:j + 1, :]
    u_scr[:HALO, :] = u[tm - HALO:, :]
    act = (jax.nn.gelu(y) * g).astype(bf16)
    o_ref[...] = x1_ref[...] + jnp.dot(act, wd_ref[...], preferred_element_type=f32)


def _mlp(h2, x1, w_up_bf, conv_w, conv_b, w_down_bf, *, seq, tm):
    n, d = h2.shape
    dff = w_down_bf.shape[0]
    const = lambda i: (0, 0)
    return pl.pallas_call(
        functools.partial(_mlp_kernel, tm=tm, tiles_per_seq=seq // tm),
        grid=(n // tm,),
        in_specs=[
            pl.BlockSpec((tm, d), lambda i: (i, 0)),
            pl.BlockSpec((tm, d), lambda i: (i, 0)),
            pl.BlockSpec(w_up_bf.shape, const, pipeline_mode=pl.Buffered(1)),
            pl.BlockSpec(conv_w.shape, const),
            pl.BlockSpec(conv_b.shape, const),
            pl.BlockSpec(w_down_bf.shape, const, pipeline_mode=pl.Buffered(1)),
        ],
        out_specs=pl.BlockSpec((tm, d), lambda i: (i, 0)),
        out_shape=jax.ShapeDtypeStruct((n, d), f32),
        scratch_shapes=[pltpu.VMEM((tm + HALO, dff), f32)],
        compiler_params=pltpu.CompilerParams(
            dimension_semantics=("arbitrary",), vmem_limit_bytes=VMEM_LIMIT),
        name="conv_mlp",
    )(h2, x1, w_up_bf, conv_w, conv_b, w_down_bf)


def _rope_tables(seq):
    inv = 1.0 / (ROPE_THETA ** (jnp.arange(0, HEAD_DIM, 2, dtype=f32) / HEAD_DIM))
    ang = jnp.arange(seq, dtype=f32)[:, None] * inv[None, :]
    cos, sin = jnp.cos(ang), jnp.sin(ang)
    reps = LANES // HEAD_DIM
    return (jnp.tile(cos, (1, 2 * reps)), jnp.tile(jnp.concatenate([-sin, sin], axis=1), (1, reps)))


def kernel(x, norm1_g, w_in, b_gate, qn_a, kn_a, lam_q1, lam_k1, lam_q2, lam_k2, subln_g, qn_b, kn_b,
           w_a_proj, w_b_proj, w_out, norm2_g, w_up, conv_w, conv_b, w_down):
    bsz, seq, d = x.shape
    depth = w_in.shape[0]
    assert seq % max(TM_INPROJ, TM_MERGE, TM_MLP) == 0 and d % LANES == 0
    cos_t, sin_t = _rope_tables(seq)
    seg = jnp.arange(LANES) // HEAD_DIM
    bd = (seg[:, None] == seg[None, :]).astype(bf16)
    reps = LANES // HEAD_DIM
    qscale = HEAD_DIM ** -0.5 * LOG2E
    for l in range(depth):
        lam_init = 0.8 - 0.6 * math.exp(-0.3 * l)
        gains = jnp.stack([jnp.tile(qn_a[l], reps) * qscale, jnp.tile(kn_a[l], reps),
                           jnp.tile(qn_b[l], reps) * qscale, jnp.tile(kn_b[l], reps)])
        q_all, k_all, vt_all, kmean, w_up_bf, w_down_bf = _inproj(
            x, norm1_g[l][None], w_in[l], gains, cos_t, sin_t, bd, (w_up[l], w_down[l]), tm=TM_INPROJ)
        kmean = kmean.reshape(bsz, seq // BLK, B_WIDTH)
        lamvec = jnp.stack([lam_q1[l], lam_k1[l], lam_q2[l], lam_k2[l]])
        gmax = jnp.max(jnp.abs(gains), axis=1) * (HEAD_DIM ** 0.5)
        subg_cols = jnp.tile(subln_g[l][:, None], (1, BLK))
        oa = _attention(q_all, k_all, vt_all, (lamvec, subg_cols),
                        gmax[0] * gmax[1] * BOUND_SLACK, moba=False, lam_init=lam_init)
        ob = _attention(q_all, k_all, vt_all, (kmean,),
                        gmax[2] * gmax[3] * BOUND_SLACK, moba=True, lam_init=lam_init)
        x1, h2 = _merge(oa, ob, x, norm1_g[l][None], w_in[l], b_gate[l][None],
                        w_a_proj[l], w_b_proj[l], w_out[l], norm2_g[l][None], tm=TM_MERGE)
        x = _mlp(h2.reshape(bsz * seq, d), x1.reshape(bsz * seq, d), w_up_bf, conv_w[l],
                 conv_b[l][None], w_down_bf, seq=seq, tm=TM_MLP).reshape(bsz, seq, d)
    return x
```

```python
import functools
import math

import jax
import jax.numpy as jnp
from jax import lax
from jax.experimental import pallas as pl
from jax.experimental.pallas import tpu as pltpu

HEAD_DIM = 64
A_HEADS = 4
A_VDIM = 2 * HEAD_DIM
A_WIDTH = A_HEADS * A_VDIM
B_HEADS = 8
B_WIDTH = B_HEADS * HEAD_DIM
B_BLOCK = 256
B_TOPK = 3
CONV_W = 3
ROPE_THETA = 10000.0
EPS = 1e-6

LANES = 128
BLK = B_BLOCK
PAIRS = A_WIDTH // LANES
LOG2E = 1.4426950408889634
NEG_BIG = -1e30
POS_BIG = 1e30
VMEM_LIMIT = 56 * 1024 * 1024
BF16_SUBLANES = 16
TM_INPROJ = 1024
TM_MERGE = 1024
TM_MLP = 512

f32 = jnp.float32
bf16 = jnp.bfloat16


def _nt_dot(a, b):
    return lax.dot_general(a, b, (((1,), (1,)), ((), ())), preferred_element_type=f32)


def _inproj_kernel(x_ref, g1_ref, w_ref, gain_ref, cos_ref, sin_ref, bd_ref, *rest, tm, n_late):
    late_f32 = rest[:n_late]
    q_ref, k_ref, vt_ref, kmean_ref = rest[n_late:n_late + 4]
    late_bf16 = rest[n_late + 4:2 * n_late + 4]
    h_scr = rest[-1]
    for src, dst in zip(late_f32, late_bf16):
        dst[...] = src[...].astype(bf16)
    nblk = tm // BLK
    x = x_ref[0]
    ms = jnp.mean(x * x, axis=-1, keepdims=True)
    h_scr[...] = (x * lax.rsqrt(ms + EPS) * g1_ref[...]).astype(bf16)
    cos = cos_ref[...]
    sin = sin_ref[...]
    bd = bd_ref[...]
    lane = lax.broadcasted_iota(jnp.int32, (tm, LANES), 1)
    first_half = (lane & (HEAD_DIM - 1)) < (HEAD_DIM // 2)
    low_half = lane < HEAD_DIM

    def col_tile(j):
        return jnp.dot(h_scr[...], w_ref[:, j * A_WIDTH:(j + 1) * A_WIDTH].astype(bf16),
                       preferred_element_type=f32)

    def norm_rope(y, gain):
        ss = jnp.dot((y * y).astype(bf16), bd, preferred_element_type=f32)
        yn = y * lax.rsqrt(ss * (1.0 / HEAD_DIM) + EPS) * gain
        rot = jnp.where(first_half, pltpu.roll(yn, LANES - HEAD_DIM // 2, 1),
                        pltpu.roll(yn, HEAD_DIM // 2, 1))
        return yn * cos + rot * sin

    for branch in range(2):
        acc_q = col_tile(3 * branch)
        for c in range(PAIRS):
            out = norm_rope(acc_q[:, c * LANES:(c + 1) * LANES], gain_ref[2 * branch:2 * branch + 1, :])
            lo = jnp.where(low_half, out, 0.0).astype(bf16)
            hi = jnp.where(low_half, 0.0, out).astype(bf16)
            for bi in range(nblk):
                q_ref[0, branch * PAIRS + c, bi, :BLK] = lo[bi * BLK:(bi + 1) * BLK]
                q_ref[0, branch * PAIRS + c, bi, BLK:] = hi[bi * BLK:(bi + 1) * BLK]
        acc_k = col_tile(3 * branch + 1)
        for c in range(PAIRS):
            out = norm_rope(acc_k[:, c * LANES:(c + 1) * LANES],
                            gain_ref[2 * branch + 1:2 * branch + 2, :])
            for bi in range(nblk):
                blk = out[bi * BLK:(bi + 1) * BLK]
                k_ref[0, branch * PAIRS + c, bi] = blk.astype(bf16)
                if branch == 1:
                    kmean_ref[0, bi:bi + 1, c * LANES:(c + 1) * LANES] = jnp.mean(
                        blk, axis=0, keepdims=True)
        acc_v = col_tile(3 * branch + 2)
        for c in range(PAIRS):
            for bi in range(nblk):
                blk = acc_v[bi * BLK:(bi + 1) * BLK, c * LANES:(c + 1) * LANES]
                vt_ref[0, branch * PAIRS + c, bi] = blk.T.astype(bf16)


def _inproj(x, g1, w_in, gains, cos_t, sin_t, bd, late_weights, *, tm):
    bsz, seq, d = x.shape
    steps = bsz * (seq // tm)
    slab = lambda b, i: (b * (seq // tm) + i, 0)
    late_specs = [pl.BlockSpec((w.shape[0] // steps, w.shape[1]), slab) for w in late_weights]
    assert all(w.shape[0] % (steps * BF16_SUBLANES) == 0 for w in late_weights)
    n_cols = 3 * A_WIDTH + 3 * B_WIDTH
    nblk = tm // BLK
    spt = seq // tm
    nb = seq // BLK
    q_shape = jax.ShapeDtypeStruct((bsz, 2 * PAIRS, nb, 2 * BLK, LANES), bf16)
    qk_shape = jax.ShapeDtypeStruct((bsz, 2 * PAIRS, nb, BLK, LANES), bf16)
    vt_shape = jax.ShapeDtypeStruct((bsz, 2 * PAIRS, nb, LANES, BLK), bf16)
    q_spec = pl.BlockSpec((1, 2 * PAIRS, nblk, 2 * BLK, LANES), lambda b, i: (b, 0, i, 0, 0))
    qk_spec = pl.BlockSpec((1, 2 * PAIRS, nblk, BLK, LANES), lambda b, i: (b, 0, i, 0, 0))
    vt_spec = pl.BlockSpec((1, 2 * PAIRS, nblk, LANES, BLK), lambda b, i: (b, 0, i, 0, 0))
    return pl.pallas_call(
        functools.partial(_inproj_kernel, tm=tm, n_late=len(late_weights)),
        grid=(bsz, spt),
        in_specs=[
            pl.BlockSpec((1, tm, d), lambda b, i: (b, i, 0)),
            pl.BlockSpec((1, d), lambda b, i: (0, 0)),
            pl.BlockSpec((d, n_cols), lambda b, i: (0, 0), pipeline_mode=pl.Buffered(1)),
            pl.BlockSpec((4, LANES), lambda b, i: (0, 0)),
            pl.BlockSpec((tm, LANES), lambda b, i: (i, 0)),
            pl.BlockSpec((tm, LANES), lambda b, i: (i, 0)),
            pl.BlockSpec((LANES, LANES), lambda b, i: (0, 0)),
        ] + late_specs,
        out_specs=[
            q_spec, qk_spec, vt_spec,
            pl.BlockSpec((1, nblk, B_WIDTH), lambda b, i: (b * spt + i, 0, 0)),
        ] + late_specs,
        out_shape=[
            q_shape, qk_shape, vt_shape,
            jax.ShapeDtypeStruct((bsz * spt, nblk, B_WIDTH), f32),
        ] + [jax.ShapeDtypeStruct(w.shape, bf16) for w in late_weights],
        scratch_shapes=[pltpu.VMEM((tm, d), bf16)],
        compiler_params=pltpu.CompilerParams(
            dimension_semantics=("arbitrary", "arbitrary"), vmem_limit_bytes=VMEM_LIMIT),
        name="inproj",
    )(x, g1, w_in, gains, cos_t, sin_t, bd, *late_weights)


TRIP_STEPS = 34
QK_AHEAD = 6
TILE_UNROLL = 16
BOUND_SLACK = 1.0 + 2.0 ** -6
MAX_FIXED_BOUND = 60.0


def _attn_kernel(itab_ref, jtab_ref, otab_ref, *refs, moba, online, lam_init, n_steps, n_masked):
    if moba:
        (q_ref, k_ref, vt_ref, mref_ref, kmean_ref, o_ref,
         m_scr, l_scr, acc_scr, s_scr, sel_scr) = refs
    else:
        (q_ref, k_ref, vt_ref, mref_ref, lam_ref, subg_ref, o_ref,
         m_scr, l_scr, acc_scr, s_scr) = refs
    nb = k_ref.shape[2]
    acc_rows = acc_scr.shape[1]
    key_minus_row = (lax.broadcasted_iota(jnp.int32, (BLK, 2 * BLK), 0)
                     - (lax.broadcasted_iota(jnp.int32, (BLK, 2 * BLK), 1) & (BLK - 1)))

    if moba:
        km = kmean_ref[0]
        km_parts = []
        for _ in range(3):
            part = km.astype(bf16)
            km_parts.append(part)
            km = km - part.astype(f32)

    nidx = lax.broadcasted_iota(jnp.int32, (nb, 2 * BLK), 0)
    for i in range(nb):
        if online:
            m_scr[i] = jnp.full((1, 2 * BLK), NEG_BIG, f32)
        l_scr[i] = jnp.zeros((1, 2 * BLK), f32)
        acc_scr[i] = jnp.zeros((acc_rows, 2 * BLK), f32)
        if moba:
            keep = nidx <= i
            if i > B_TOPK:
                qs = q_ref[0, 0, i]
                g = _nt_dot(km_parts[0], qs) + _nt_dot(km_parts[1], qs) + _nt_dot(km_parts[2], qs)
                cnt = jnp.zeros((nb, 2 * BLK), f32)
                for mm in range(i):
                    row = g[mm:mm + 1, :]
                    beats = (row > g) | ((row == g) & (nidx > mm))
                    cnt = cnt + jnp.where(beats, 1.0, 0.0)
                keep = ((cnt < float(B_TOPK)) & (nidx < i)) | (nidx == i)
            sel_scr[i] = jnp.where(keep, 1.0, 0.0)

    ring = s_scr.shape[0]

    def qk(step, u):
        s = _nt_dot(k_ref[0, 0, jtab_ref[step]], q_ref[0, 0, itab_ref[step]])
        if u < n_masked:
            s = jnp.where(key_minus_row <= otab_ref[step], s, NEG_BIG)
        s_scr[u % ring] = s

    def update(step, u):
        i = itab_ref[step]
        j = jtab_ref[step]
        s = s_scr[u % ring]
        if moba:
            chosen = sel_scr[i, pl.ds(j, 1), :] > 0.5
        if online:
            m_old = m_scr[i]
            m_new = jnp.maximum(m_old, jnp.max(s, axis=0, keepdims=True))
            if moba:
                m_new = jnp.where(chosen, m_new, m_old)
            m_sub = m_new
            alpha = jnp.exp2(m_old - m_new)
            m_scr[i] = m_new
        else:
            m_sub = mref_ref[...]
        if moba:
            m_sub = jnp.where(chosen, m_sub, POS_BIG)
        p = jnp.exp2(s - m_sub)
        psum = jnp.sum(p, axis=0, keepdims=True)
        pb = p.astype(bf16)
        vt = vt_ref[0, 0, j]
        pv = jnp.dot(vt, pb, preferred_element_type=f32)
        if online:
            l_scr[i] = alpha * l_scr[i] + psum
            acc_scr[i] = alpha * acc_scr[i] + pv
        else:
            l_scr[i] = l_scr[i] + psum
            acc_scr[i] = acc_scr[i] + pv

    ahead = ring - 1

    def trip(it, carry):
        base = it * TRIP_STEPS
        for u in range(ahead):
            qk(base + u, u)
        for u in range(TRIP_STEPS):
            if u + ahead < TRIP_STEPS:
                qk(base + u + ahead, u + ahead)
            update(base + u, u)
        return carry

    lax.fori_loop(0, n_steps // TRIP_STEPS, trip, 0)

    if not moba:
        lv = lam_ref[...]
        lam = (jnp.exp(jnp.sum(lv[0:1] * lv[1:2], axis=-1, keepdims=True))
               - jnp.exp(jnp.sum(lv[2:3] * lv[3:4], axis=-1, keepdims=True)) + lam_init)

    def finish_tiles(t, carry):
        for r in range(TILE_UNROLL):
            i = t * TILE_UNROLL + r
            o = acc_scr[i] * (1.0 / l_scr[i])
            if moba:
                od = jnp.concatenate([o[:HEAD_DIM, :BLK], o[HEAD_DIM:, BLK:]], axis=0)
            else:
                od = o[:, :BLK] - lam * o[:, BLK:]
                ms = jnp.mean(od * od, axis=0, keepdims=True)
                od = od * lax.rsqrt(ms + EPS) * subg_ref[...] * (1.0 - lam_init)
            o_ref[0, 0, i] = od.astype(bf16)
        return carry

    lax.fori_loop(0, nb // TILE_UNROLL, finish_tiles, 0)


def _step_tables(nb):
    per_trip = TRIP_STEPS
    diag = [(i, i, 0) for i in range(nb)]
    rest = [(i, j, BLK) for j in range(nb) for i in range(j + 1, nb)]
    n_steps = len(diag) + len(rest)
    assert n_steps % per_trip == 0
    n_trips = n_steps // per_trip
    n_masked = -(-len(diag) // n_trips)
    assert n_masked <= per_trip
    steps = []
    for _ in range(n_trips):
        for _ in range(n_masked):
            steps.append(diag.pop(0) if diag else rest.pop(0))
        for _ in range(per_trip - n_masked):
            steps.append(rest.pop(0))
    assert not diag and not rest
    return [jnp.asarray([s[c] for s in steps], jnp.int32) for c in range(3)], n_steps, n_masked


def _attention(q_all, k_all, vt_all, aux, score_bound, *, moba, lam_init):
    bsz, _, nb, _, _ = k_all.shape
    off = PAIRS if moba else 0
    assert nb % TILE_UNROLL == 0
    (itab, jtab, otab), n_steps, n_masked = _step_tables(nb)
    const = lambda b, h, it, jt, ot: (0, 0)
    in_specs = [
        pl.BlockSpec((1, 1, nb, 2 * BLK, LANES), lambda b, h, it, jt, ot: (b, h + off, 0, 0, 0)),
        pl.BlockSpec((1, 1, nb, BLK, LANES), lambda b, h, it, jt, ot: (b, h + off, 0, 0, 0)),
        pl.BlockSpec((1, 1, nb, LANES, BLK), lambda b, h, it, jt, ot: (b, h + off, 0, 0, 0)),
        pl.BlockSpec((1, 2 * BLK), const),
    ]
    scratch = [pltpu.VMEM((nb, 1, 2 * BLK), f32), pltpu.VMEM((nb, 1, 2 * BLK), f32),
               pltpu.VMEM((nb, LANES, 2 * BLK), f32),
               pltpu.VMEM((QK_AHEAD + 1, BLK, 2 * BLK), f32)]
    if moba:
        (kmean,) = aux
        in_specs.append(pl.BlockSpec((1, nb, LANES), lambda b, h, it, jt, ot: (b, 0, h)))
        scratch.append(pltpu.VMEM((nb, nb, 2 * BLK), f32))
    else:
        lamvec, subg = aux
        in_specs.append(pl.BlockSpec(lamvec.shape, const))
        in_specs.append(pl.BlockSpec(subg.shape, const))
    mref = jnp.full((1, 2 * BLK), score_bound, f32)

    def run(online):
        return pl.pallas_call(
            functools.partial(_attn_kernel, moba=moba, online=online, lam_init=lam_init,
                              n_steps=n_steps, n_masked=n_masked),
            grid_spec=pltpu.PrefetchScalarGridSpec(
                num_scalar_prefetch=3,
                grid=(bsz, PAIRS),
                in_specs=in_specs,
                out_specs=pl.BlockSpec((1, 1, nb, LANES, BLK),
                                       lambda b, h, it, jt, ot: (b, h, 0, 0, 0)),
                scratch_shapes=scratch),
            out_shape=jax.ShapeDtypeStruct((bsz, PAIRS, nb, LANES, BLK), bf16),
            compiler_params=pltpu.CompilerParams(
                dimension_semantics=("arbitrary", "arbitrary"), vmem_limit_bytes=VMEM_LIMIT),
            name=("moba_attn" if moba else "diff_attn") + ("_online" if online else ""),
        )(itab, jtab, otab, q_all, k_all, vt_all, mref, *aux)

    return lax.cond(score_bound <= MAX_FIXED_BOUND, lambda: run(False), lambda: run(True))


def _merge_kernel(oa_ref, ob_ref, x_ref, g1_ref, wga_ref, wgb_ref, bg_ref, wa_ref, wb_ref, wo_ref,
                  g2_ref, x1_ref, h2_ref):
    d = x_ref.shape[-1]
    x = x_ref[0]
    ms = jnp.mean(x * x, axis=-1, keepdims=True)
    h = (x * lax.rsqrt(ms + EPS) * g1_ref[...]).astype(bf16)
    merged = None
    for br, (wg_ref, o_ref, w_ref) in enumerate(((wga_ref, oa_ref, wa_ref), (wgb_ref, ob_ref, wb_ref))):
        gate = jax.nn.sigmoid(jnp.dot(h, wg_ref[...].astype(bf16), preferred_element_type=f32)
                              + bg_ref[:, br * d:(br + 1) * d])
        o_t = jnp.concatenate(
            [jnp.concatenate([o_ref[0, c, bi] for bi in range(o_ref.shape[2])], axis=1)
             for c in range(o_ref.shape[1])], axis=0)
        term = gate * lax.dot_general(o_t, w_ref[...].astype(bf16), (((0,), (0,)), ((), ())),
                                      preferred_element_type=f32)
        merged = term if merged is None else merged + term
    x1 = x + jnp.dot(merged.astype(bf16), wo_ref[...].astype(bf16), preferred_element_type=f32)
    x1_ref[0] = x1
    ms = jnp.mean(x1 * x1, axis=-1, keepdims=True)
    h2_ref[0] = (x1 * lax.rsqrt(ms + EPS) * g2_ref[...]).astype(bf16)


def _merge(oa, ob, x, g1, w_in, bg, wa, wb, wo, g2, *, tm):
    bsz, seq, d = x.shape
    gate_blk = (w_in.shape[1] - 2 * d) // d
    assert gate_blk * d + 2 * d == w_in.shape[1]
    const = lambda b, i: (0, 0)
    once = pl.Buffered(1)
    tile = lambda b, i: (b, i, 0)
    return pl.pallas_call(
        _merge_kernel,
        grid=(bsz, seq // tm),
        in_specs=[
            pl.BlockSpec((1, PAIRS, tm // BLK, LANES, BLK), lambda b, i: (b, 0, i, 0, 0)),
            pl.BlockSpec((1, PAIRS, tm // BLK, LANES, BLK), lambda b, i: (b, 0, i, 0, 0)),
            pl.BlockSpec((1, tm, d), tile),
            pl.BlockSpec((1, d), const),
            pl.BlockSpec((d, d), lambda b, i: (0, gate_blk), pipeline_mode=once),
            pl.BlockSpec((d, d), lambda b, i: (0, gate_blk + 1), pipeline_mode=once),
            pl.BlockSpec(bg.shape, const),
            pl.BlockSpec(wa.shape, const, pipeline_mode=once),
            pl.BlockSpec(wb.shape, const, pipeline_mode=once),
            pl.BlockSpec(wo.shape, const, pipeline_mode=once),
            pl.BlockSpec((1, d), const),
        ],
        out_specs=[pl.BlockSpec((1, tm, d), tile), pl.BlockSpec((1, tm, d), tile)],
        out_shape=[jax.ShapeDtypeStruct((bsz, seq, d), f32),
                   jax.ShapeDtypeStruct((bsz, seq, d), bf16)],
        compiler_params=pltpu.CompilerParams(
            dimension_semantics=("arbitrary", "arbitrary"), vmem_limit_bytes=VMEM_LIMIT),
        name="merge_outproj",
    )(oa, ob, x, g1, w_in, w_in, bg, wa, wb, wo, g2)


HALO = 8
assert HALO >= CONV_W - 1


def _mlp_kernel(h2_ref, x1_ref, wup_ref, cw_ref, cb_ref, wd_ref, o_ref, u_scr, *, tm, tiles_per_seq):
    i = pl.program_id(0)
    dff = wd_ref.shape[0]

    @pl.when(i % tiles_per_seq == 0)
    def _():
        u_scr[:HALO, :] = jnp.zeros((HALO, dff), f32)

    ug = jnp.dot(h2_ref[...], wup_ref[...], preferred_element_type=f32)
    u, g = ug[:, :dff], ug[:, dff:]
    u_scr[HALO:, :] = u
    y = cb_ref[...]
    for j in range(CONV_W):
        y = y + u_scr[pl.ds(HALO - (CONV_W - 1) + j, tm), :] * cw_ref[j:j + 1, :]
    u_scr[:HALO, :] = u[tm - HALO:, :]
    act = (jax.nn.gelu(y) * g).astype(bf16)
    o_ref[...] = x1_ref[...] + jnp.dot(act, wd_ref[...], preferred_element_type=f32)


def _mlp(h2, x1, w_up_bf, conv_w, conv_b, w_down_bf, *, seq, tm):
    n, d = h2.shape
    dff = w_down_bf.shape[0]
    const = lambda i: (0, 0)
    return pl.pallas_call(
        functools.partial(_mlp_kernel, tm=tm, tiles_per_seq=seq // tm),
        grid=(n // tm,),
        in_specs=[
            pl.BlockSpec((tm, d), lambda i: (i, 0)),
            pl.BlockSpec((tm, d), lambda i: (i, 0)),
            pl.BlockSpec(w_up_bf.shape, const, pipeline_mode=pl.Buffered(1)),
            pl.BlockSpec(conv_w.shape, const),
            pl.BlockSpec(conv_b.shape, const),
            pl.BlockSpec(w_down_bf.shape, const, pipeline_mode=pl.Buffered(1)),
        ],
        out_specs=pl.BlockSpec((tm, d), lambda i: (i, 0)),
        out_shape=jax.ShapeDtypeStruct((n, d), f32),
        scratch_shapes=[pltpu.VMEM((tm + HALO, dff), f32)],
        compiler_params=pltpu.CompilerParams(
            dimension_semantics=("arbitrary",), vmem_limit_bytes=VMEM_LIMIT),
        name="conv_mlp",
    )(h2, x1, w_up_bf, conv_w, conv_b, w_down_bf)


def _rope_tables(seq):
    inv = 1.0 / (ROPE_THETA ** (jnp.arange(0, HEAD_DIM, 2, dtype=f32) / HEAD_DIM))
    ang = jnp.arange(seq, dtype=f32)[:, None] * inv[None, :]
    cos, sin = jnp.cos(ang), jnp.sin(ang)
    reps = LANES // HEAD_DIM
    return (jnp.tile(cos, (1, 2 * reps)), jnp.tile(jnp.concatenate([-sin, sin], axis=1), (1, reps)))


def kernel(x, norm1_g, w_in, b_gate, qn_a, kn_a, lam_q1, lam_k1, lam_q2, lam_k2, subln_g, qn_b, kn_b,
           w_a_proj, w_b_proj, w_out, norm2_g, w_up, conv_w, conv_b, w_down):
    bsz, seq, d = x.shape
    depth = w_in.shape[0]
    assert seq % max(TM_INPROJ, TM_MERGE, TM_MLP) == 0 and d % LANES == 0
    cos_t, sin_t = _rope_tables(seq)
    seg = jnp.arange(LANES) // HEAD_DIM
    bd = (seg[:, None] == seg[None, :]).astype(bf16)
    reps = LANES // HEAD_DIM
    qscale = HEAD_DIM ** -0.5 * LOG2E
    for l in range(depth):
        lam_init = 0.8 - 0.6 * math.exp(-0.3 * l)
        gains = jnp.stack([jnp.tile(qn_a[l], reps) * qscale, jnp.tile(kn_a[l], reps),
                           jnp.tile(qn_b[l], reps) * qscale, jnp.tile(kn_b[l], reps)])
        q_all, k_all, vt_all, kmean, w_up_bf, w_down_bf = _inproj(
            x, norm1_g[l][None], w_in[l], gains, cos_t, sin_t, bd, (w_up[l], w_down[l]), tm=TM_INPROJ)
        kmean = kmean.reshape(bsz, seq // BLK, B_WIDTH)
        lamvec = jnp.stack([lam_q1[l], lam_k1[l], lam_q2[l], lam_k2[l]])
        gmax = jnp.max(jnp.abs(gains), axis=1) * (HEAD_DIM ** 0.5)
        subg_cols = jnp.tile(subln_g[l][:, None], (1, BLK))
        oa = _attention(q_all, k_all, vt_all, (lamvec, subg_cols),
                        gmax[0] * gmax[1] * BOUND_SLACK, moba=False, lam_init=lam_init)
        ob = _attention(q_all, k_all, vt_all, (kmean,),
                        gmax[2] * gmax[3] * BOUND_SLACK, moba=True, lam_init=lam_init)
        x1, h2 = _merge(oa, ob, x, norm1_g[l][None], w_in[l], b_gate[l][None],
                        w_a_proj[l], w_b_proj[l], w_out[l], norm2_g[l][None], tm=TM_MERGE)
        x = _mlp(h2.reshape(bsz * seq, d), x1.reshape(bsz * seq, d), w_up_bf, conv_w[l],
                 conv_b[l][None], w_down_bf, seq=seq, tm=TM_MLP).reshape(bsz, seq, d)
    return x
```

```python
import functools
import math

import jax
import jax.numpy as jnp
from jax import lax
from jax.experimental import pallas as pl
from jax.experimental.pallas import tpu as pltpu

HEAD_DIM = 64
A_HEADS = 4
A_VDIM = 2 * HEAD_DIM
A_WIDTH = A_HEADS * A_VDIM
B_HEADS = 8
B_WIDTH = B_HEADS * HEAD_DIM
B_BLOCK = 256
B_TOPK = 3
CONV_W = 3
ROPE_THETA = 10000.0
EPS = 1e-6

LANES = 128
BLK = B_BLOCK
PAIRS = A_WIDTH // LANES
LOG2E = 1.4426950408889634
NEG_BIG = -1e30
POS_BIG = 1e30
VMEM_LIMIT = 56 * 1024 * 1024
BF16_SUBLANES = 16
TM_INPROJ = 1024
TM_MERGE = 1024
TM_MLP = 512

f32 = jnp.float32
bf16 = jnp.bfloat16


def _nt_dot(a, b):
    return lax.dot_general(a, b, (((1,), (1,)), ((), ())), preferred_element_type=f32)


def _inproj_kernel(x_ref, g1_ref, w_ref, gain_ref, cos_ref, sin_ref, bd_ref, *rest, tm, n_late):
    late_f32 = rest[:n_late]
    q_ref, k_ref, vt_ref, kmean_ref = rest[n_late:n_late + 4]
    late_bf16 = rest[n_late + 4:2 * n_late + 4]
    h_scr = rest[-1]
    for src, dst in zip(late_f32, late_bf16):
        dst[...] = src[...].astype(bf16)
    nblk = tm // BLK
    x = x_ref[0]
    ms = jnp.mean(x * x, axis=-1, keepdims=True)
    h_scr[...] = (x * lax.rsqrt(ms + EPS) * g1_ref[...]).astype(bf16)
    cos = cos_ref[...]
    sin = sin_ref[...]
    bd = bd_ref[...]
    lane = lax.broadcasted_iota(jnp.int32, (tm, LANES), 1)
    first_half = (lane & (HEAD_DIM - 1)) < (HEAD_DIM // 2)
    low_half = lane < HEAD_DIM

    def col_tile(j):
        return jnp.dot(h_scr[...], w_ref[:, j * A_WIDTH:(j + 1) * A_WIDTH].astype(bf16),
                       preferred_element_type=f32)

    def norm_rope(y, gain):
        ss = jnp.dot((y * y).astype(bf16), bd, preferred_element_type=f32)
        yn = y * lax.rsqrt(ss * (1.0 / HEAD_DIM) + EPS) * gain
        rot = jnp.where(first_half, pltpu.roll(yn, LANES - HEAD_DIM // 2, 1),
                        pltpu.roll(yn, HEAD_DIM // 2, 1))
        return yn * cos + rot * sin

    for branch in range(2):
        acc_q = col_tile(3 * branch)
        for c in range(PAIRS):
            out = norm_rope(acc_q[:, c * LANES:(c + 1) * LANES], gain_ref[2 * branch:2 * branch + 1, :])
            lo = jnp.where(low_half, out, 0.0).astype(bf16)
            hi = jnp.where(low_half, 0.0, out).astype(bf16)
            for bi in range(nblk):
                q_ref[0, branch * PAIRS + c, bi, :BLK] = lo[bi * BLK:(bi + 1) * BLK]
                q_ref[0, branch * PAIRS + c, bi, BLK:] = hi[bi * BLK:(bi + 1) * BLK]
        acc_k = col_tile(3 * branch + 1)
        for c in range(PAIRS):
            out = norm_rope(acc_k[:, c * LANES:(c + 1) * LANES],
                            gain_ref[2 * branch + 1:2 * branch + 2, :])
            for bi in range(nblk):
                blk = out[bi * BLK:(bi + 1) * BLK]
                k_ref[0, branch * PAIRS + c, bi] = blk.astype(bf16)
                if branch == 1:
                    kmean_ref[0, bi:bi + 1, c * LANES:(c + 1) * LANES] = jnp.mean(
                        blk, axis=0, keepdims=True)
        acc_v = col_tile(3 * branch + 2)
        for c in range(PAIRS):
            for bi in range(nblk):
                blk = acc_v[bi * BLK:(bi + 1) * BLK, c * LANES:(c + 1) * LANES]
                vt_ref[0, branch * PAIRS + c, bi] = blk.T.astype(bf16)


def _inproj(x, g1, w_in, gains, cos_t, sin_t, bd, late_weights, *, tm):
    bsz, seq, d = x.shape
    steps = bsz * (seq // tm)
    slab = lambda b, i: (b * (seq // tm) + i, 0)
    late_specs = [pl.BlockSpec((w.shape[0] // steps, w.shape[1]), slab) for w in late_weights]
    assert all(w.shape[0] % (steps * BF16_SUBLANES) == 0 for w in late_weights)
    n_cols = 3 * A_WIDTH + 3 * B_WIDTH
    nblk = tm // BLK
    spt = seq // tm
    nb = seq // BLK
    q_shape = jax.ShapeDtypeStruct((bsz, 2 * PAIRS, nb, 2 * BLK, LANES), bf16)
    qk_shape = jax.ShapeDtypeStruct((bsz, 2 * PAIRS, nb, BLK, LANES), bf16)
    vt_shape = jax.ShapeDtypeStruct((bsz, 2 * PAIRS, nb, LANES, BLK), bf16)
    q_spec = pl.BlockSpec((1, 2 * PAIRS, nblk, 2 * BLK, LANES), lambda b, i: (b, 0, i, 0, 0))
    qk_spec = pl.BlockSpec((1, 2 * PAIRS, nblk, BLK, LANES), lambda b, i: (b, 0, i, 0, 0))
    vt_spec = pl.BlockSpec((1, 2 * PAIRS, nblk, LANES, BLK), lambda b, i: (b, 0, i, 0, 0))
    return pl.pallas_call(
        functools.partial(_inproj_kernel, tm=tm, n_late=len(late_weights)),
        grid=(bsz, spt),
        in_specs=[
            pl.BlockSpec((1, tm, d), lambda b, i: (b, i, 0)),
            pl.BlockSpec((1, d), lambda b, i: (0, 0)),
            pl.BlockSpec((d, n_cols), lambda b, i: (0, 0), pipeline_mode=pl.Buffered(1)),
            pl.BlockSpec((4, LANES), lambda b, i: (0, 0)),
            pl.BlockSpec((tm, LANES), lambda b, i: (i, 0)),
            pl.BlockSpec((tm, LANES), lambda b, i: (i, 0)),
            pl.BlockSpec((LANES, LANES), lambda b, i: (0, 0)),
        ] + late_specs,
        out_specs=[
            q_spec, qk_spec, vt_spec,
            pl.BlockSpec((1, nblk, B_WIDTH), lambda b, i: (b * spt + i, 0, 0)),
        ] + late_specs,
        out_shape=[
            q_shape, qk_shape, vt_shape,
            jax.ShapeDtypeStruct((bsz * spt, nblk, B_WIDTH), f32),
        ] + [jax.ShapeDtypeStruct(w.shape, bf16) for w in late_weights],
        scratch_shapes=[pltpu.VMEM((tm, d), bf16)],
        compiler_params=pltpu.CompilerParams(
            dimension_semantics=("arbitrary", "arbitrary"), vmem_limit_bytes=VMEM_LIMIT),
        name="inproj",
    )(x, g1, w_in, gains, cos_t, sin_t, bd, *late_weights)


TRIP_STEPS = 34
QK_AHEAD = 6
TILE_UNROLL = 16
BOUND_SLACK = 1.0 + 2.0 ** -6
MAX_FIXED_BOUND = 60.0


def _attn_kernel(itab_ref, jtab_ref, otab_ref, *refs, moba, online, lam_init, n_steps, n_masked):
    if moba:
        (q_ref, k_ref, vt_ref, mref_ref, kmean_ref, o_ref,
         m_scr, l_scr, acc_scr, s_scr, sel_scr) = refs
    else:
        (q_ref, k_ref, vt_ref, mref_ref, lam_ref, subg_ref, o_ref,
         m_scr, l_scr, acc_scr, s_scr) = refs
    nb = k_ref.shape[2]
    acc_rows = acc_scr.shape[1]
    key_minus_row = (lax.broadcasted_iota(jnp.int32, (BLK, 2 * BLK), 0)
                     - (lax.broadcasted_iota(jnp.int32, (BLK, 2 * BLK), 1) & (BLK - 1)))

    if moba:
        km = kmean_ref[0]
        km_parts = []
        for _ in range(3):
            part = km.astype(bf16)
            km_parts.append(part)
            km = km - part.astype(f32)

    nidx = lax.broadcasted_iota(jnp.int32, (nb, 2 * BLK), 0)
    for i in range(nb):
        if online:
            m_scr[i] = jnp.full((1, 2 * BLK), NEG_BIG, f32)
        l_scr[i] = jnp.zeros((1, 2 * BLK), f32)
        acc_scr[i] = jnp.zeros((acc_rows, 2 * BLK), f32)
        if moba:
            keep = nidx <= i
            if i > B_TOPK:
                qs = q_ref[0, 0, i]
                g = _nt_dot(km_parts[0], qs) + _nt_dot(km_parts[1], qs) + _nt_dot(km_parts[2], qs)
                cnt = jnp.zeros((nb, 2 * BLK), f32)
                for mm in range(i):
                    row = g[mm:mm + 1, :]
                    beats = (row > g) | ((row == g) & (nidx > mm))
                    cnt = cnt + jnp.where(beats, 1.0, 0.0)
                keep = ((cnt < float(B_TOPK)) & (nidx < i)) | (nidx == i)
            sel_scr[i] = jnp.where(keep, 1.0, 0.0)

    ring = s_scr.shape[0]

    def qk(step, u):
        s = _nt_dot(k_ref[0, 0, jtab_ref[step]], q_ref[0, 0, itab_ref[step]])
        if u < n_masked:
            s = jnp.where(key_minus_row <= otab_ref[step], s, NEG_BIG)
        s_scr[u % ring] = s

    def update(step, u):
        i = itab_ref[step]
        j = jtab_ref[step]
        s = s_scr[u % ring]
        if moba:
            chosen = sel_scr[i, pl.ds(j, 1), :] > 0.5
        if online:
            m_old = m_scr[i]
            m_new = jnp.maximum(m_old, jnp.max(s, axis=0, keepdims=True))
            if moba:
                m_new = jnp.where(chosen, m_new, m_old)
            m_sub = m_new
            alpha = jnp.exp2(m_old - m_new)
            m_scr[i] = m_new
        else:
            m_sub = mref_ref[...]
        if moba:
            m_sub = jnp.where(chosen, m_sub, POS_BIG)
        p = jnp.exp2(s - m_sub)
        psum = jnp.sum(p, axis=0, keepdims=True)
        pb = p.astype(bf16)
        vt = vt_ref[0, 0, j]
        pv = jnp.dot(vt, pb, preferred_element_type=f32)
        if online:
            l_scr[i] = alpha * l_scr[i] + psum
            acc_scr[i] = alpha * acc_scr[i] + pv
        else:
            l_scr[i] = l_scr[i] + psum
            acc_scr[i] = acc_scr[i] + pv

    ahead = ring - 1

    def trip(it, carry):
        base = it * TRIP_STEPS
        for u in range(ahead):
            qk(base + u, u)
        for u in range(TRIP_STEPS):
            if u + ahead < TRIP_STEPS:
                qk(base + u + ahead, u + ahead)
            update(base + u, u)
        return carry

    lax.fori_loop(0, n_steps // TRIP_STEPS, trip, 0)

    if not moba:
        lv = lam_ref[...]
        lam = (jnp.exp(jnp.sum(lv[0:1] * lv[1:2], axis=-1, keepdims=True))
               - jnp.exp(jnp.sum(lv[2:3] * lv[3:4], axis=-1, keepdims=True)) + lam_init)

    def finish_tiles(t, carry):
        for r in range(TILE_UNROLL):
            i = t * TILE_UNROLL + r
            o = acc_scr[i] * (1.0 / l_scr[i])
            if moba:
                od = jnp.concatenate([o[:HEAD_DIM, :BLK], o[HEAD_DIM:, BLK:]], axis=0)
            else:
                od = o[:, :BLK] - lam * o[:, BLK:]
                ms = jnp.mean(od * od, axis=0, keepdims=True)
                od = od * lax.rsqrt(ms + EPS) * subg_ref[...] * (1.0 - lam_init)
            o_ref[0, 0, i] = od.astype(bf16)
        return carry

    for t in range(nb // TILE_UNROLL):
        finish_tiles(t, 0)


def _step_tables(nb):
    per_trip = TRIP_STEPS
    diag = [(i, i, 0) for i in range(nb)]
    rest = [(i, j, BLK) for j in range(nb) for i in range(j + 1, nb)]
    n_steps = len(diag) + len(rest)
    assert n_steps % per_trip == 0
    n_trips = n_steps // per_trip
    n_masked = -(-len(diag) // n_trips)
    assert n_masked <= per_trip
    steps = []
    for _ in range(n_trips):
        for _ in range(n_masked):
            steps.append(diag.pop(0) if diag else rest.pop(0))
        for _ in range(per_trip - n_masked):
            steps.append(rest.pop(0))
    assert not diag and not rest
    return [jnp.asarray([s[c] for s in steps], jnp.int32) for c in range(3)], n_steps, n_masked


def _attention(q_all, k_all, vt_all, aux, score_bound, *, moba, lam_init):
    bsz, _, nb, _, _ = k_all.shape
    off = PAIRS if moba else 0
    assert nb % TILE_UNROLL == 0
    (itab, jtab, otab), n_steps, n_masked = _step_tables(nb)
    const = lambda b, h, it, jt, ot: (0, 0)
    in_specs = [
        pl.BlockSpec((1, 1, nb, 2 * BLK, LANES), lambda b, h, it, jt, ot: (b, h + off, 0, 0, 0)),
        pl.BlockSpec((1, 1, nb, BLK, LANES), lambda b, h, it, jt, ot: (b, h + off, 0, 0, 0)),
        pl.BlockSpec((1, 1, nb, LANES, BLK), lambda b, h, it, jt, ot: (b, h + off, 0, 0, 0)),
        pl.BlockSpec((1, 2 * BLK), const),
    ]
    scratch = [pltpu.VMEM((nb, 1, 2 * BLK), f32), pltpu.VMEM((nb, 1, 2 * BLK), f32),
               pltpu.VMEM((nb, LANES, 2 * BLK), f32),
               pltpu.VMEM((QK_AHEAD + 1, BLK, 2 * BLK), f32)]
    if moba:
        (kmean,) = aux
        in_specs.append(pl.BlockSpec((1, nb, LANES), lambda b, h, it, jt, ot: (b, 0, h)))
        scratch.append(pltpu.VMEM((nb, nb, 2 * BLK), f32))
    else:
        lamvec, subg = aux
        in_specs.append(pl.BlockSpec(lamvec.shape, const))
        in_specs.append(pl.BlockSpec(subg.shape, const))
    mref = jnp.full((1, 2 * BLK), score_bound, f32)

    def run(online):
        return pl.pallas_call(
            functools.partial(_attn_kernel, moba=moba, online=online, lam_init=lam_init,
                              n_steps=n_steps, n_masked=n_masked),
            grid_spec=pltpu.PrefetchScalarGridSpec(
                num_scalar_prefetch=3,
                grid=(bsz, PAIRS),
                in_specs=in_specs,
                out_specs=pl.BlockSpec((1, 1, nb, LANES, BLK),
                                       lambda b, h, it, jt, ot: (b, h, 0, 0, 0)),
                scratch_shapes=scratch),
            out_shape=jax.ShapeDtypeStruct((bsz, PAIRS, nb, LANES, BLK), bf16),
            compiler_params=pltpu.CompilerParams(
                dimension_semantics=("arbitrary", "arbitrary"), vmem_limit_bytes=VMEM_LIMIT),
            name=("moba_attn" if moba else "diff_attn") + ("_online" if online else ""),
        )(itab, jtab, otab, q_all, k_all, vt_all, mref, *aux)

    return lax.cond(score_bound <= MAX_FIXED_BOUND, lambda: run(False), lambda: run(True))


def _merge_kernel(oa_ref, ob_ref, x_ref, g1_ref, wga_ref, wgb_ref, bg_ref, wa_ref, wb_ref, wo_ref,
                  g2_ref, x1_ref, h2_ref):
    d = x_ref.shape[-1]
    x = x_ref[0]
    ms = jnp.mean(x * x, axis=-1, keepdims=True)
    h = (x * lax.rsqrt(ms + EPS) * g1_ref[...]).astype(bf16)
    merged = None
    for br, (wg_ref, o_ref, w_ref) in enumerate(((wga_ref, oa_ref, wa_ref), (wgb_ref, ob_ref, wb_ref))):
        gate = jax.nn.sigmoid(jnp.dot(h, wg_ref[...].astype(bf16), preferred_element_type=f32)
                              + bg_ref[:, br * d:(br + 1) * d])
        o_t = jnp.concatenate(
            [jnp.concatenate([o_ref[0, c, bi] for bi in range(o_ref.shape[2])], axis=1)
             for c in range(o_ref.shape[1])], axis=0)
        term = gate * lax.dot_general(o_t, w_ref[...].astype(bf16), (((0,), (0,)), ((), ())),
                                      preferred_element_type=f32)
        merged = term if merged is None else merged + term
    x1 = x + jnp.dot(merged.astype(bf16), wo_ref[...].astype(bf16), preferred_element_type=f32)
    x1_ref[0] = x1
    ms = jnp.mean(x1 * x1, axis=-1, keepdims=True)
    h2_ref[0] = (x1 * lax.rsqrt(ms + EPS) * g2_ref[...]).astype(bf16)


def _merge(oa, ob, x, g1, w_in, bg, wa, wb, wo, g2, *, tm):
    bsz, seq, d = x.shape
    gate_blk = (w_in.shape[1] - 2 * d) // d
    assert gate_blk * d + 2 * d == w_in.shape[1]
    const = lambda b, i: (0, 0)
    once = pl.Buffered(1)
    tile = lambda b, i: (b, i, 0)
    return pl.pallas_call(
        _merge_kernel,
        grid=(bsz, seq // tm),
        in_specs=[
            pl.BlockSpec((1, PAIRS, tm // BLK, LANES, BLK), lambda b, i: (b, 0, i, 0, 0)),
            pl.BlockSpec((1, PAIRS, tm // BLK, LANES, BLK), lambda b, i: (b, 0, i, 0, 0)),
            pl.BlockSpec((1, tm, d), tile),
            pl.BlockSpec((1, d), const),
            pl.BlockSpec((d, d), lambda b, i: (0, gate_blk), pipeline_mode=once),
            pl.BlockSpec((d, d), lambda b, i: (0, gate_blk + 1), pipeline_mode=once),
            pl.BlockSpec(bg.shape, const),
            pl.BlockSpec(wa.shape, const, pipeline_mode=once),
            pl.BlockSpec(wb.shape, const, pipeline_mode=once),
            pl.BlockSpec(wo.shape, const, pipeline_mode=once),
            pl.BlockSpec((1, d), const),
        ],
        out_specs=[pl.BlockSpec((1, tm, d), tile), pl.BlockSpec((1, tm, d), tile)],
        out_shape=[jax.ShapeDtypeStruct((bsz, seq, d), f32),
                   jax.ShapeDtypeStruct((bsz, seq, d), bf16)],
        compiler_params=pltpu.CompilerParams(
            dimension_semantics=("arbitrary", "arbitrary"), vmem_limit_bytes=VMEM_LIMIT),
        name="merge_outproj",
    )(oa, ob, x, g1, w_in, w_in, bg, wa, wb, wo, g2)


HALO = 8
assert HALO >= CONV_W - 1


def _mlp_kernel(h2_ref, x1_ref, wup_ref, cw_ref, cb_ref, wd_ref, o_ref, u_scr, *, tm, tiles_per_seq):
    i = pl.program_id(0)
    dff = wd_ref.shape[0]

    @pl.when(i % tiles_per_seq == 0)
    def _():
        u_scr[:HALO, :] = jnp.zeros((HALO, dff), f32)

    ug = jnp.dot(h2_ref[...], wup_ref[...], preferred_element_type=f32)
    u, g = ug[:, :dff], ug[:, dff:]
    u_scr[HALO:, :] = u
    y = cb_ref[...]
    for j in range(CONV_W):
        y = y + u_scr[pl.ds(HALO - (CONV_W - 1) + j, tm), :] * cw_ref[j:j + 1, :]
    u_scr[:HALO, :] = u[tm - HALO:, :]
    act = (jax.nn.gelu(y) * g).astype(bf16)
    o_ref[...] = x1_ref[...] + jnp.dot(act, wd_ref[...], preferred_element_type=f32)


def _mlp(h2, x1, w_up_bf, conv_w, conv_b, w_down_bf, *, seq, tm):
    n, d = h2.shape
    dff = w_down_bf.shape[0]
    const = lambda i: (0, 0)
    return pl.pallas_call(
        functools.partial(_mlp_kernel, tm=tm, tiles_per_seq=seq // tm),
        grid=(n // tm,),
        in_specs=[
            pl.BlockSpec((tm, d), lambda i: (i, 0)),
            pl.BlockSpec((tm, d), lambda i: (i, 0)),
            pl.BlockSpec(w_up_bf.shape, const, pipeline_mode=pl.Buffered(1)),
            pl.BlockSpec(conv_w.shape, const),
            pl.BlockSpec(conv_b.shape, const),
            pl.BlockSpec(w_down_bf.shape, const, pipeline_mode=pl.Buffered(1)),
        ],
        out_specs=pl.BlockSpec((tm, d), lambda i: (i, 0)),
        out_shape=jax.ShapeDtypeStruct((n, d), f32),
        scratch_shapes=[pltpu.VMEM((tm + HALO, dff), f32)],
        compiler_params=pltpu.CompilerParams(
            dimension_semantics=("arbitrary",), vmem_limit_bytes=VMEM_LIMIT),
        name="conv_mlp",
    )(h2, x1, w_up_bf, conv_w, conv_b, w_down_bf)


def _rope_tables(seq):
    inv = 1.0 / (ROPE_THETA ** (jnp.arange(0, HEAD_DIM, 2, dtype=f32) / HEAD_DIM))
    ang = jnp.arange(seq, dtype=f32)[:, None] * inv[None, :]
    cos, sin = jnp.cos(ang), jnp.sin(ang)
    reps = LANES // HEAD_DIM
    return (jnp.tile(cos, (1, 2 * reps)), jnp.tile(jnp.concatenate([-sin, sin], axis=1), (1, reps)))


def kernel(x, norm1_g, w_in, b_gate, qn_a, kn_a, lam_q1, lam_k1, lam_q2, lam_k2, subln_g, qn_b, kn_b,
           w_a_proj, w_b_proj, w_out, norm2_g, w_up, conv_w, conv_b, w_down):
    bsz, seq, d = x.shape
    depth = w_in.shape[0]
    assert seq % max(TM_INPROJ, TM_MERGE, TM_MLP) == 0 and d % LANES == 0
    cos_t, sin_t = _rope_tables(seq)
    seg = jnp.arange(LANES) // HEAD_DIM
    bd = (seg[:, None] == seg[None, :]).astype(bf16)
    reps = LANES // HEAD_DIM
    qscale = HEAD_DIM ** -0.5 * LOG2E
    for l in range(depth):
        lam_init = 0.8 - 0.6 * math.exp(-0.3 * l)
        gains = jnp.stack([jnp.tile(qn_a[l], reps) * qscale, jnp.tile(kn_a[l], reps),
                           jnp.tile(qn_b[l], reps) * qscale, jnp.tile(kn_b[l], reps)])
        q_all, k_all, vt_all, kmean, w_up_bf, w_down_bf = _inproj(
            x, norm1_g[l][None], w_in[l], gains, cos_t, sin_t, bd, (w_up[l], w_down[l]), tm=TM_INPROJ)
        kmean = kmean.reshape(bsz, seq // BLK, B_WIDTH)
        lamvec = jnp.stack([lam_q1[l], lam_k1[l], lam_q2[l], lam_k2[l]])
        gmax = jnp.max(jnp.abs(gains), axis=1) * (HEAD_DIM ** 0.5)
        subg_cols = jnp.tile(subln_g[l][:, None], (1, BLK))
        oa = _attention(q_all, k_all, vt_all, (lamvec, subg_cols),
                        gmax[0] * gmax[1] * BOUND_SLACK, moba=False, lam_init=lam_init)
        ob = _attention(q_all, k_all, vt_all, (kmean,),
                        gmax[2] * gmax[3] * BOUND_SLACK, moba=True, lam_init=lam_init)
        x1, h2 = _merge(oa, ob, x, norm1_g[l][None], w_in[l], b_gate[l][None],
                        w_a_proj[l], w_b_proj[l], w_out[l], norm2_g[l][None], tm=TM_MERGE)
        x = _mlp(h2.reshape(bsz * seq, d), x1.reshape(bsz * seq, d), w_up_bf, conv_w[l],
                 conv_b[l][None], w_down_bf, seq=seq, tm=TM_MLP).reshape(bsz, seq, d)
    return x
```
